```python
import jax, jax.numpy as jnp
from jax import lax
import numpy as np

D_MODEL = 2048
BATCH = 1
SEQ = 8192
DEPTH = 4

GRID_W = 64
CTX_LEN = 256
HEAD_DIM = 128
MIX_WIDTH = D_MODEL
POOL_WIDTH = MIX_WIDTH // 4
POOL_WINDOWS = (2, 4, 8, 16)
POOL_GROUP = POOL_WIDTH // len(POOL_WINDOWS)
N_Q_HEADS = (MIX_WIDTH // 2) // HEAD_DIM
N_KV_HEADS = N_Q_HEADS // 4
GQA_GROUP = N_Q_HEADS // N_KV_HEADS
WINDOW = 128
ATTN_BLOCK = 128
ATTN_SCALE = HEAD_DIM ** -0.5
ROPE_BASE = 10000.0
ROPE_FREQS = HEAD_DIM // 4
CONV_WIDTH = MIX_WIDTH // 4
CONV_K = 31
N_BRANCH = 3
GATE_WIDTH = N_BRANCH * D_MODEL
POOL_OFF = GATE_WIDTH
Q_OFF = POOL_OFF + POOL_WIDTH
K_OFF = Q_OFF + N_Q_HEADS * HEAD_DIM
V_OFF = K_OFF + N_KV_HEADS * HEAD_DIM
CONV_OFF = V_OFF + N_KV_HEADS * HEAD_DIM
IN_WIDTH = CONV_OFF + 2 * CONV_WIDTH
KV_WIDTH = N_KV_HEADS * HEAD_DIM
FFN_DIM = ((8 * D_MODEL // 3 + 255) // 256) * 256
N_EXPERTS = 8
TOP_K = 2
MOE_BLOCK = 128
N_MOD = 6
EPS = 1e-6
NEG_INF = -1e30
N_DENSE = (DEPTH + 1) // 2
N_MOE = DEPTH // 2

kernel_name = 'hybrid_pool_swa_conv_moe_dit'

f32 = jnp.float32


def rms_norm(x, g):
    xf = x.astype(f32)
    y = xf * lax.rsqrt(jnp.mean(xf * xf, axis=-1, keepdims=True) + EPS)
    return (y * g.astype(f32)).astype(x.dtype)


def layer_norm(x, g, b):
    xf = x.astype(f32)
    xc = xf - jnp.mean(xf, axis=-1, keepdims=True)
    y = xc * lax.rsqrt(jnp.mean(xc * xc, axis=-1, keepdims=True) + EPS)
    return (y * g.astype(f32) + b.astype(f32)).astype(x.dtype)


def modulate(u, shift, scale):
    return u * (1 + scale) + shift


def swiglu(x, w1, w3, w2):
    return (jax.nn.silu(x @ w1) * (x @ w3)) @ w2


def axial_rope_tables(S):
    rows = S // GRID_W
    row = jnp.broadcast_to(jnp.arange(rows)[:, None], (rows, GRID_W)).reshape(S).astype(f32)
    col = jnp.broadcast_to(jnp.arange(GRID_W)[None, :], (rows, GRID_W)).reshape(S).astype(f32)
    inv = ROPE_BASE ** (-jnp.arange(ROPE_FREQS, dtype=f32) / ROPE_FREQS)
    ang = jnp.stack([row[:, None] * inv, col[:, None] * inv], axis=1)
    return jnp.cos(ang), jnp.sin(ang)


def apply_axial_rope(x, cos, sin):
    xs = x.astype(f32).reshape(x.shape[:-1] + (2, 2, ROPE_FREQS))
    x1, x2 = xs[..., 0, :], xs[..., 1, :]
    cb, sb = cos[:, None], sin[:, None]
    out = jnp.stack([x1 * cb - x2 * sb, x2 * cb + x1 * sb], axis=-2)
    return out.reshape(x.shape).astype(x.dtype)


def pool_mixer(u, pool_w, pool_scale):
    B, L, _ = u.shape
    uf = u.astype(f32).reshape(B, L, len(POOL_WINDOWS), POOL_GROUP)
    cs = jnp.pad(jnp.cumsum(uf, axis=1), ((0, 0), (1, 0), (0, 0), (0, 0)))
    t = jnp.arange(L)
    diffs = []
    for gi, w in enumerate(POOL_WINDOWS):
        lo = jnp.clip(t - w // 2, 0, L)
        hi = jnp.clip(t - w // 2 + w, 0, L)
        win_sum = cs[:, hi, gi] - cs[:, lo, gi]
        diffs.append(win_sum / (hi - lo).astype(f32)[None, :, None] - uf[:, :, gi])
    d = jnp.stack(diffs, axis=2).astype(u.dtype)
    y = jnp.einsum('blgc,gce->blge', d, pool_w).reshape(B, L, POOL_WIDTH)
    return y * pool_scale


def conv_module(u, conv_dw, conv_db, ln_g, ln_b):
    a, b = jnp.split(u, 2, axis=-1)
    g = a * jax.nn.sigmoid(b)
    y = lax.conv_general_dilated(g, conv_dw[:, None, :], window_strides=(1,),
                                 padding=[(CONV_K // 2, CONV_K // 2)],
                                 dimension_numbers=('NWC', 'WIO', 'NWC'),
                                 feature_group_count=CONV_WIDTH) + conv_db
    return jax.nn.silu(layer_norm(y, ln_g, ln_b))


def windowed_attention(q, k, v, kc, vc, sinks):
    B, S = q.shape[:2]
    C = kc.shape[1]
    nb = S // ATTN_BLOCK
    nw = WINDOW // ATTN_BLOCK
    kw_len = (2 * nw + 1) * ATTN_BLOCK
    qb = q.reshape(B, nb, ATTN_BLOCK, N_KV_HEADS, GQA_GROUP, HEAD_DIM)
    pad = ((0, 0), (nw * ATTN_BLOCK, nw * ATTN_BLOCK), (0, 0), (0, 0))

    def band(t):
        tp = jnp.pad(t, pad).reshape(B, nb + 2 * nw, ATTN_BLOCK, N_KV_HEADS, HEAD_DIM)
        return jnp.concatenate([tp[:, j:j + nb] for j in range(2 * nw + 1)], axis=2)

    kb, vb = band(k), band(v)
    s_loc = jnp.einsum('bnqhgd,bnkhd->bnhgqk', qb, kb, preferred_element_type=f32) * ATTN_SCALE
    qpos = jnp.arange(nb)[:, None] * ATTN_BLOCK + jnp.arange(ATTN_BLOCK)[None, :]
    kpos = jnp.arange(nb)[:, None] * ATTN_BLOCK - nw * ATTN_BLOCK + jnp.arange(kw_len)[None, :]
    valid = ((jnp.abs(kpos[:, None, :] - qpos[:, :, None]) <= WINDOW)
             & (kpos[:, None, :] >= 0) & (kpos[:, None, :] < S))
    s_loc = jnp.where(valid[None, :, None, None], s_loc, NEG_INF)
    s_ctx = jnp.einsum('bnqhgd,bchd->bnhgqc', qb, kc, preferred_element_type=f32) * ATTN_SCALE
    sink = jnp.broadcast_to(sinks.astype(f32).reshape(N_KV_HEADS, GQA_GROUP)[None, None, :, :, None, None],
                            s_loc.shape[:-1] + (1,))
    p = jax.nn.softmax(jnp.concatenate([s_loc, s_ctx, sink], axis=-1), axis=-1).astype(v.dtype)
    o = (jnp.einsum('bnhgqk,bnkhd->bnqhgd', p[..., :kw_len], vb)
         + jnp.einsum('bnhgqc,bchd->bnqhgd', p[..., kw_len:kw_len + C], vc))
    return o.reshape(B, S, N_Q_HEADS * HEAD_DIM)


def context_attention(qc, kc, vc, sinks):
    B, C = qc.shape[:2]
    qg = qc.reshape(B, C, N_KV_HEADS, GQA_GROUP, HEAD_DIM)
    s = jnp.einsum('bqhgd,bkhd->bhgqk', qg, kc, preferred_element_type=f32) * ATTN_SCALE
    sink = jnp.broadcast_to(sinks.astype(f32).reshape(N_KV_HEADS, GQA_GROUP)[None, :, :, None, None],
                            s.shape[:-1] + (1,))
    p = jax.nn.softmax(jnp.concatenate([s, sink], axis=-1), axis=-1)[..., :C].astype(vc.dtype)
    o = jnp.einsum('bhgqk,bkhd->bqhgd', p, vc)
    return o.reshape(B, C, N_Q_HEADS * HEAD_DIM)


def merge_branches(z, attn_o, b_gate, pool_w, pool_scale, w_pool_up, w_attn_up,
                   conv_dw, conv_db, conv_ln_g, conv_ln_b, w_conv_up, w_out):
    gates = jax.nn.sigmoid((z[..., :GATE_WIDTH] + b_gate).astype(f32)).astype(z.dtype)
    g_pool, g_attn, g_conv = jnp.split(gates, N_BRANCH, axis=-1)
    y_pool = pool_mixer(z[..., POOL_OFF:Q_OFF], pool_w, pool_scale) @ w_pool_up
    y_attn = attn_o @ w_attn_up
    y_conv = conv_module(z[..., CONV_OFF:], conv_dw, conv_db, conv_ln_g, conv_ln_b) @ w_conv_up
    return (g_pool * y_pool + g_attn * y_attn + g_conv * y_conv) @ w_out


def token_mixer(u, uc, cos, sin, w_in, b_gate, pool_w, pool_scale, w_pool_up, attn_sink, w_attn_up,
                conv_dw, conv_db, conv_ln_g, conv_ln_b, w_conv_up, w_out, with_ctx):
    B, S, _ = u.shape
    C = uc.shape[1]
    z = u @ w_in
    if with_ctx:
        zc = uc @ w_in
        kv_off = K_OFF
    else:
        zc = uc @ w_in[:, K_OFF:CONV_OFF]
        kv_off = 0
    kc = zc[..., kv_off:kv_off + KV_WIDTH].reshape(B, C, N_KV_HEADS, HEAD_DIM)
    vc = zc[..., kv_off + KV_WIDTH:kv_off + 2 * KV_WIDTH].reshape(B, C, N_KV_HEADS, HEAD_DIM)
    q = apply_axial_rope(z[..., Q_OFF:K_OFF].reshape(B, S, N_Q_HEADS, HEAD_DIM), cos, sin)
    k = apply_axial_rope(z[..., K_OFF:V_OFF].reshape(B, S, N_KV_HEADS, HEAD_DIM), cos, sin)
    v = z[..., V_OFF:CONV_OFF].reshape(B, S, N_KV_HEADS, HEAD_DIM)
    o = windowed_attention(q, k, v, kc, vc, attn_sink)
    bp = (b_gate, pool_w, pool_scale, w_pool_up, w_attn_up, conv_dw, conv_db, conv_ln_g, conv_ln_b, w_conv_up, w_out)
    y = merge_branches(z, o, *bp)
    if not with_ctx:
        return y, None
    qc = zc[..., Q_OFF:K_OFF].reshape(B, C, N_Q_HEADS, HEAD_DIM)
    oc = context_attention(qc, kc, vc, attn_sink)
    yc = merge_branches(zc, oc, *bp)
    return y, yc


def moe_swiglu(h, router_w, w1, w3, w2):
    N, D = h.shape
    logits = jnp.matmul(h, router_w, preferred_element_type=f32)
    top_val, top_idx = lax.top_k(logits, TOP_K)
    gate = jax.nn.softmax(top_val, axis=-1)
    A = N * TOP_K
    e_flat = top_idx.reshape(A).astype(jnp.int32)
    tok_flat = jnp.repeat(jnp.arange(N, dtype=jnp.int32), TOP_K)
    order = jnp.argsort(e_flat)
    e_s, tok_s, gate_s = e_flat[order], tok_flat[order], gate.reshape(A)[order]
    counts = jnp.zeros((N_EXPERTS,), jnp.int32).at[e_flat].add(1)
    padded = (counts + MOE_BLOCK - 1) // MOE_BLOCK * MOE_BLOCK
    ends = jnp.cumsum(counts)
    pends = jnp.cumsum(padded)
    dest = (pends - padded)[e_s] + jnp.arange(A, dtype=jnp.int32) - (ends - counts)[e_s]
    n_blocks = -(-A // MOE_BLOCK) + N_EXPERTS
    buf = jnp.zeros((n_blocks * MOE_BLOCK, D), h.dtype).at[dest].set(h[tok_s])
    blk_e = jnp.minimum(jnp.searchsorted(pends, jnp.arange(n_blocks, dtype=jnp.int32) * MOE_BLOCK, side='right'),
                        N_EXPERTS - 1)

    def expert_block(args):
        xb, e = args
        return swiglu(xb, w1[e], w3[e], w2[e])

    yb = lax.map(expert_block, (buf.reshape(n_blocks, MOE_BLOCK, D), blk_e))
    y = yb.reshape(-1, D)[dest] * gate_s[:, None].astype(h.dtype)
    return jnp.zeros_like(h).at[tok_s].add(y)


def channel_mix(u, layer, ffn_w1, ffn_w3, ffn_w2, router_w, moe_w1, moe_w3, moe_w2):
    j = layer // 2
    if layer % 2 == 0:
        return swiglu(u, ffn_w1[j], ffn_w3[j], ffn_w2[j])
    return moe_swiglu(u.reshape(-1, u.shape[-1]), router_w[j], moe_w1[j], moe_w3[j], moe_w2[j]).reshape(u.shape)


def setup_inputs(seed: int = 0) -> dict:
    key = jax.random.key(seed)
    ks = jax.random.split(key, 27)
    D = D_MODEL

    def nrm(k, shape, scale):
        return jax.random.normal(k, shape, jnp.float32) * scale

    return {
        'x': nrm(ks[0], (BATCH, SEQ, D), 1.0),
        'c': nrm(ks[1], (BATCH, D), 1.0),
        'ctx': nrm(ks[2], (BATCH, CTX_LEN, D), 1.0),
        'c_ctx': nrm(ks[3], (D,), 1.0),
        'w_mod': nrm(ks[4], (DEPTH, D, N_MOD * D), 0.5 * D ** -0.5),
        'b_mod': nrm(ks[5], (DEPTH, N_MOD * D), 0.01),
        'norm_g': 1.0 + nrm(ks[6], (DEPTH, 4, D), 0.05),
        'w_in': nrm(ks[7], (DEPTH, D, IN_WIDTH), D ** -0.5),
        'b_gate': nrm(ks[8], (DEPTH, GATE_WIDTH), 0.01),
        'pool_w': nrm(ks[9], (DEPTH, len(POOL_WINDOWS), POOL_GROUP, POOL_GROUP), POOL_GROUP ** -0.5),
        'pool_scale': 1.0 + nrm(ks[10], (DEPTH, POOL_WIDTH), 0.05),
        'w_pool_up': nrm(ks[11], (DEPTH, POOL_WIDTH, D), POOL_WIDTH ** -0.5),
        'attn_sink': nrm(ks[12], (DEPTH, N_Q_HEADS), 0.5),
        'w_attn_up': nrm(ks[13], (DEPTH, N_Q_HEADS * HEAD_DIM, D), (N_Q_HEADS * HEAD_DIM) ** -0.5),
        'conv_dw': nrm(ks[14], (DEPTH, CONV_K, CONV_WIDTH), CONV_K ** -0.5),
        'conv_db': nrm(ks[15], (DEPTH, CONV_WIDTH), 0.01),
        'conv_ln_g': 1.0 + nrm(ks[16], (DEPTH, CONV_WIDTH), 0.05),
        'conv_ln_b': nrm(ks[17], (DEPTH, CONV_WIDTH), 0.01),
        'w_conv_up': nrm(ks[18], (DEPTH, CONV_WIDTH, D), CONV_WIDTH ** -0.5),
        'w_out': nrm(ks[19], (DEPTH, D, D), D ** -0.5),
        'ffn_w1': nrm(ks[20], (N_DENSE, D, FFN_DIM), D ** -0.5),
        'ffn_w3': nrm(ks[21], (N_DENSE, D, FFN_DIM), D ** -0.5),
        'ffn_w2': nrm(ks[22], (N_DENSE, FFN_DIM, D), FFN_DIM ** -0.5),
        'router_w': nrm(ks[23], (N_MOE, D, N_EXPERTS), D ** -0.5),
        'moe_w1': nrm(ks[24], (N_MOE, N_EXPERTS, D, FFN_DIM), D ** -0.5),
        'moe_w3': nrm(ks[25], (N_MOE, N_EXPERTS, D, FFN_DIM), D ** -0.5),
        'moe_w2': nrm(ks[26], (N_MOE, N_EXPERTS, FFN_DIM, D), FFN_DIM ** -0.5),
    }


def reference(x, c, ctx, c_ctx, w_mod, b_mod, norm_g, w_in, b_gate, pool_w, pool_scale, w_pool_up,
              attn_sink, w_attn_up, conv_dw, conv_db, conv_ln_g, conv_ln_b, w_conv_up, w_out,
              ffn_w1, ffn_w3, ffn_w2, router_w, moe_w1, moe_w3, moe_w2):
    B, S, D = x.shape
    cos, sin = axial_rope_tables(S)
    s_lat = jax.nn.silu(c)
    s_ctx = jax.nn.silu(c_ctx)
    ffn_params = (ffn_w1, ffn_w3, ffn_w2, router_w, moe_w1, moe_w3, moe_w2)
    h, hc = x, ctx
    for i in range(DEPTH):
        last = i == DEPTH - 1
        mod = (s_lat @ w_mod[i] + b_mod[i])[:, None, :]
        sh1, sc1, g1, sh2, sc2, g2 = jnp.split(mod, N_MOD, axis=-1)
        n_cm = 2 if last else N_MOD
        cm = jnp.split(s_ctx @ w_mod[i][:, :n_cm * D] + b_mod[i][:n_cm * D], n_cm)
        u = modulate(rms_norm(h, norm_g[i, 0]), sh1, sc1)
        uc = modulate(rms_norm(hc, norm_g[i, 0]), cm[0], cm[1])
        y, yc = token_mixer(u, uc, cos, sin, w_in[i], b_gate[i], pool_w[i], pool_scale[i], w_pool_up[i],
                            attn_sink[i], w_attn_up[i], conv_dw[i], conv_db[i], conv_ln_g[i], conv_ln_b[i],
                            w_conv_up[i], w_out[i], not last)
        h = h + g1 * rms_norm(y, norm_g[i, 1])
        u = modulate(rms_norm(h, norm_g[i, 2]), sh2, sc2)
        h = h + g2 * rms_norm(channel_mix(u, i, *ffn_params), norm_g[i, 3])
        if not last:
            hc = hc + cm[2] * rms_norm(yc, norm_g[i, 1])
            uc = modulate(rms_norm(hc, norm_g[i, 2]), cm[3], cm[4])
            hc = hc + cm[5] * rms_norm(channel_mix(uc, i, *ffn_params), norm_g[i, 3])
    return h
```

```python
import functools

import jax
import jax.numpy as jnp
from jax import lax
from jax.experimental import pallas as pl
from jax.experimental.pallas import tpu as pltpu

f32 = jnp.float32
bf16 = jnp.bfloat16

D = 2048
S = 8192
C = 256
T = S + C
DEPTH = 4
GRID_W = 64
HEAD_DIM = 128
POOL_WIDTH = 512
POOL_WINDOWS = (2, 4, 8, 16)
POOL_GROUP = 128
N_Q_HEADS = 8
N_KV_HEADS = 2
GQA = 4
WINDOW = 128
ATTN_SCALE = HEAD_DIM ** -0.5
ROPE_BASE = 10000.0
ROPE_FREQS = 32
CONV_WIDTH = 512
CONV_K = 31
GATE_WIDTH = 3 * D
POOL_OFF = GATE_WIDTH
Q_OFF = POOL_OFF + POOL_WIDTH
K_OFF = Q_OFF + N_Q_HEADS * HEAD_DIM
V_OFF = K_OFF + N_KV_HEADS * HEAD_DIM
CONV_OFF = V_OFF + N_KV_HEADS * HEAD_DIM
IN_WIDTH = CONV_OFF + 2 * CONV_WIDTH
FFN = 5632
N_EXPERTS = 8
N_MOD = 6
EPS = 1e-6
NEG_INF = -1e30

VMEM_LIMIT_BYTES = 56 * 1024 * 1024

TM = 768
NM = T // TM
TS = 256
NS = T // TS
NS_LAT = S // TS
HALO = 16
TN_IN = 1024
TN_MERGE = 256
TF = 256
TN_MOD = 1024
ROUTER_PAD = 128
ROW_CHUNK = 128
MOD_ROWS = 16


def _cparams(semantics):
    return pltpu.CompilerParams(dimension_semantics=semantics,
                                vmem_limit_bytes=VMEM_LIMIT_BYTES)


def _sigmoid(x):
    return 1.0 / (1.0 + jnp.exp(-x))


def _silu(x):
    return x * _sigmoid(x)


def _mod_rows(mod_ref, row0, nrows):
    rows = row0 + lax.broadcasted_iota(jnp.int32, (nrows, 1), 0)
    return jnp.where(rows >= S, mod_ref[1:2, :], mod_ref[0:1, :])


def _rms(x, g):
    return x * lax.rsqrt(jnp.mean(x * x, axis=-1, keepdims=True) + EPS) * g


def _for_row_chunks(fn):
    def step(r, carry):
        off = pl.multiple_of(r * ROW_CHUNK, ROW_CHUNK)
        fn(pl.ds(off, ROW_CHUNK), off)
        return carry
    lax.fori_loop(0, TM // ROW_CHUNK, step, 0)


def _norm_mod_tile(h_ref, g, sh_ref, sc_ref, m, store):
    def chunk(rows, off):
        row0 = m * TM + off
        u = (_rms(h_ref[rows, :], g) * (1.0 + _mod_rows(sc_ref, row0, ROW_CHUNK))
             + _mod_rows(sh_ref, row0, ROW_CHUNK))
        store(rows, u)
    _for_row_chunks(chunk)


def _residual_tile(h_ref, y_ref, g, gate_ref, m):
    def chunk(rows, off):
        y_ref[rows, :] = h_ref[rows, :] + _mod_rows(gate_ref, m * TM + off, ROW_CHUNK) * _rms(y_ref[rows, :], g)
    _for_row_chunks(chunk)


def _dot(a, b):
    return jnp.dot(a, b, preferred_element_type=f32)


def _mod_kernel(cc_ref, w_ref, b_ref, o_ref):
    s = _silu(cc_ref[...])
    o_ref[...] = _dot(s.astype(bf16), w_ref[...].astype(bf16)) + b_ref[...]


def _modulation(cc, w_mod, b_mod):
    nn = (N_MOD * D) // TN_MOD
    return pl.pallas_call(
        _mod_kernel,
        grid=(DEPTH, nn),
        in_specs=[
            pl.BlockSpec((MOD_ROWS, D), lambda l, n: (0, 0)),
            pl.BlockSpec((None, D, TN_MOD), lambda l, n: (l, 0, n)),
            pl.BlockSpec((None, 1, TN_MOD), lambda l, n: (l, 0, n)),
        ],
        out_specs=pl.BlockSpec((None, MOD_ROWS, TN_MOD), lambda l, n: (l, 0, n)),
        out_shape=jax.ShapeDtypeStruct((DEPTH, MOD_ROWS, N_MOD * D), f32),
        compiler_params=_cparams(("arbitrary", "arbitrary")),
        name="modulation",
    )(cc, w_mod, b_mod.reshape(DEPTH, 1, N_MOD * D))


def _win_kernel(h_ref, ng_ref, sh_ref, sc_ref, w_ref, z_ref, u_ref):
    m = pl.program_id(0)

    @pl.when(pl.program_id(1) == 0)
    def _():
        def store(rows, u):
            u_ref[rows, :] = u.astype(bf16)
        _norm_mod_tile(h_ref, ng_ref[0:1, :], sh_ref, sc_ref, m, store)

    z_ref[...] = _dot(u_ref[...], w_ref[...].astype(bf16)).astype(bf16)


def _in_proj(h, norm_g, mod, w_in, layer):
    return pl.pallas_call(
        _win_kernel,
        grid=(NM, IN_WIDTH // TN_IN),
        in_specs=[
            pl.BlockSpec((TM, D), lambda m, n: (m, 0)),
            pl.BlockSpec((None, 4, D), lambda m, n: (layer, 0, 0)),
            pl.BlockSpec((None, MOD_ROWS, D), lambda m, n: (layer, 0, 0)),
            pl.BlockSpec((None, MOD_ROWS, D), lambda m, n: (layer, 0, 1)),
            pl.BlockSpec((None, D, TN_IN), lambda m, n: (layer, 0, n)),
        ],
        out_specs=pl.BlockSpec((TM, TN_IN), lambda m, n: (m, n)),
        out_shape=jax.ShapeDtypeStruct((T, IN_WIDTH), bf16),
        scratch_shapes=[pltpu.VMEM((TM, D), bf16)],
        compiler_params=_cparams(("arbitrary", "arbitrary")),
        name="in_proj",
    )(h, norm_g, mod, mod, w_in)


def _seqmix_kernel(zp_c, zp_p, zp_n, zc_c, zc_p, zc_n, pw_ref, ps_ref, dw_ref, db_ref,
                   lng_ref, lnb_ref, po_ref, co_ref, xe, ge):
    i = pl.program_id(0)
    has_prev = jnp.where(jnp.logical_and(i != 0, i != NS_LAT), 1.0, 0.0).astype(f32)
    has_next = jnp.where(i < NS_LAT - 1, 1.0, 0.0).astype(f32)
    is_lat = i < NS_LAT
    pos = jnp.where(is_lat, i * TS, 0) + lax.broadcasted_iota(jnp.int32, (TS, 1), 0)
    seq_len = jnp.where(is_lat, S, C)

    xe[0:HALO, :] = zp_p[...].astype(f32) * has_prev
    xe[HALO:HALO + TS, :] = zp_c[...].astype(f32)
    xe[HALO + TS:, :] = zp_n[...].astype(f32) * has_next
    outs = []
    for g, w in enumerate(POOL_WINDOWS):
        cols = slice(g * POOL_GROUP, (g + 1) * POOL_GROUP)
        acc = xe[pl.ds(HALO - w // 2, TS), cols]
        for d in range(-w // 2 + 1, w // 2):
            acc = acc + xe[pl.ds(HALO + d, TS), cols]
        lo = jnp.clip(pos - w // 2, 0, seq_len)
        hi = jnp.clip(pos - w // 2 + w, 0, seq_len)
        diff = acc / (hi - lo).astype(f32) - xe[HALO:HALO + TS, cols]
        outs.append(_dot(diff.astype(bf16), pw_ref[g].astype(bf16)))
    po_ref[...] = (jnp.concatenate(outs, axis=1) * ps_ref[...]).astype(bf16)

    def glu(ref):
        zz = ref[...].astype(f32)
        return zz[:, :CONV_WIDTH] * _sigmoid(zz[:, CONV_WIDTH:])

    ge[0:HALO, :] = glu(zc_p) * has_prev
    ge[HALO:HALO + TS, :] = glu(zc_c)
    ge[HALO + TS:, :] = glu(zc_n) * has_next
    base = HALO - CONV_K // 2
    acc = ge[pl.ds(base, TS), :] * dw_ref[0:1, :]
    for k in range(1, CONV_K):
        acc = acc + ge[pl.ds(base + k, TS), :] * dw_ref[k:k + 1, :]
    y = acc + db_ref[...]
    yc = y - jnp.mean(y, axis=-1, keepdims=True)
    yn = yc * lax.rsqrt(jnp.mean(yc * yc, axis=-1, keepdims=True) + EPS) * lng_ref[...] + lnb_ref[...]
    co_ref[...] = _silu(yn).astype(bf16)


def _seq_mixers(z, pool_w, pool_scale, conv_dw, conv_db, conv_ln_g, conv_ln_b, layer):
    rb = TS // HALO
    last = T // HALO - 1
    pcol = POOL_OFF // POOL_WIDTH
    ccol = CONV_OFF // (2 * CONV_WIDTH)

    def prev(i):
        return jnp.maximum(i * rb - 1, 0)

    def nxt(i):
        return jnp.minimum((i + 1) * rb, last)

    vec = lambda width: pl.BlockSpec((None, 1, width), lambda i: (layer, 0, 0))
    return pl.pallas_call(
        _seqmix_kernel,
        grid=(NS,),
        in_specs=[
            pl.BlockSpec((TS, POOL_WIDTH), lambda i: (i, pcol)),
            pl.BlockSpec((HALO, POOL_WIDTH), lambda i: (prev(i), pcol)),
            pl.BlockSpec((HALO, POOL_WIDTH), lambda i: (nxt(i), pcol)),
            pl.BlockSpec((TS, 2 * CONV_WIDTH), lambda i: (i, ccol)),
            pl.BlockSpec((HALO, 2 * CONV_WIDTH), lambda i: (prev(i), ccol)),
            pl.BlockSpec((HALO, 2 * CONV_WIDTH), lambda i: (nxt(i), ccol)),
            pl.BlockSpec((None, 4, POOL_GROUP, POOL_GROUP), lambda i: (layer, 0, 0, 0)),
            vec(POOL_WIDTH),
            pl.BlockSpec((None, CONV_K, CONV_WIDTH), lambda i: (layer, 0, 0)),
            vec(CONV_WIDTH), vec(CONV_WIDTH), vec(CONV_WIDTH),
        ],
        out_specs=[
            pl.BlockSpec((TS, POOL_WIDTH), lambda i: (i, 0)),
            pl.BlockSpec((TS, CONV_WIDTH), lambda i: (i, 0)),
        ],
        out_shape=[
            jax.ShapeDtypeStruct((T, POOL_WIDTH), bf16),
            jax.ShapeDtypeStruct((T, CONV_WIDTH), bf16),
        ],
        scratch_shapes=[
            pltpu.VMEM((TS + 2 * HALO, POOL_WIDTH), f32),
            pltpu.VMEM((TS + 2 * HALO, CONV_WIDTH), f32),
        ],
        compiler_params=_cparams(("arbitrary",)),
        name="seq_mixers",
    )(z, z, z, z, z, z, pool_w, pool_scale.reshape(DEPTH, 1, POOL_WIDTH), conv_dw,
      conv_db.reshape(DEPTH, 1, CONV_WIDTH), conv_ln_g.reshape(DEPTH, 1, CONV_WIDTH),
      conv_ln_b.reshape(DEPTH, 1, CONV_WIDTH))


def _rope(x, cos_t, sin_t):
    lane = lax.broadcasted_iota(jnp.int32, x.shape, 1) & (2 * ROPE_FREQS - 1)
    partner = jnp.where(lane < ROPE_FREQS, pltpu.roll(x, HEAD_DIM - ROPE_FREQS, 1),
                        pltpu.roll(x, ROPE_FREQS, 1))
    return x * cos_t + partner * sin_t


def _attn_kernel(sink_ref, q_ref, kc_ref, kp_ref, kn_ref, vc_ref, vp_ref, vn_ref, kx_ref, vx_ref,
                 cq_ref, sq_ref, cp_ref, sp_ref, cn_ref, sn_ref, o_ref, *, layer):
    hk = pl.program_id(0)
    i = pl.program_id(1)
    cq = cq_ref[...]
    sq = sq_ref[...]
    qs = []
    for j in range(GQA):
        qj = q_ref[:, j * HEAD_DIM:(j + 1) * HEAD_DIM].astype(f32)
        qs.append(_rope(qj, cq, sq).astype(bf16))
    q = jnp.concatenate(qs, axis=0)
    k = jnp.concatenate([
        _rope(kp_ref[...].astype(f32), cp_ref[...], sp_ref[...]).astype(bf16),
        _rope(kc_ref[...].astype(f32), cq, sq).astype(bf16),
        _rope(kn_ref[...].astype(f32), cn_ref[...], sn_ref[...]).astype(bf16)], axis=0)
    v = jnp.concatenate([vp_ref[...], vc_ref[...], vn_ref[...]], axis=0)
    kw = k.shape[0]

    nt = (((1,), (1,)), ((), ()))
    s_loc = lax.dot_general(q, k, nt, preferred_element_type=f32) * ATTN_SCALE
    s_ctx = lax.dot_general(q, kx_ref[...], nt, preferred_element_type=f32) * ATTN_SCALE

    qrow = lax.broadcasted_iota(jnp.int32, (GQA * TS, 1), 0) & (TS - 1)
    krel = lax.broadcasted_iota(jnp.int32, (1, kw), 1) - WINDOW
    kpos = i * TS + krel
    k_ok = jnp.logical_and(jnp.logical_and(kpos >= 0, kpos < S), i < NS_LAT)
    valid = jnp.logical_and(jnp.abs(krel - qrow) <= WINDOW, k_ok)
    s_loc = jnp.where(valid, s_loc, NEG_INF)

    sink = jnp.concatenate(
        [jnp.full((TS, 1), sink_ref[layer, hk * GQA + j], f32) for j in range(GQA)], axis=0)
    mx = jnp.maximum(jnp.maximum(jnp.max(s_loc, axis=-1, keepdims=True),
                                 jnp.max(s_ctx, axis=-1, keepdims=True)), sink)
    p_loc = jnp.exp(s_loc - mx)
    p_ctx = jnp.exp(s_ctx - mx)
    denom = (jnp.sum(p_loc, axis=-1, keepdims=True) + jnp.sum(p_ctx, axis=-1, keepdims=True)
             + jnp.exp(sink - mx))
    o = (_dot(p_loc.astype(bf16), v) + _dot(p_ctx.astype(bf16), vx_ref[...])) / denom
    for j in range(GQA):
        o_ref[:, j * HEAD_DIM:(j + 1) * HEAD_DIM] = o[j * TS:(j + 1) * TS, :].astype(bf16)


def _attention(z, attn_sink, cos_t, sin_t, layer):
    qcol = Q_OFF // (GQA * HEAD_DIM)
    kcol = K_OFF // HEAD_DIM
    vcol = V_OFF // HEAD_DIM
    hb = TS // WINDOW
    last = T // WINDOW - 1
    ctx_blk = S // TS

    def prev(i):
        return jnp.maximum(i * hb - 1, 0)

    def nxt(i):
        return jnp.minimum((i + 1) * hb, last)

    def kv_specs(col):
        return [
            pl.BlockSpec((TS, HEAD_DIM), lambda h, i: (i, col + h)),
            pl.BlockSpec((WINDOW, HEAD_DIM), lambda h, i: (prev(i), col + h)),
            pl.BlockSpec((WINDOW, HEAD_DIM), lambda h, i: (nxt(i), col + h)),
        ]

    tab_specs = [
        pl.BlockSpec((TS, HEAD_DIM), lambda h, i: (i, 0)),
        pl.BlockSpec((TS, HEAD_DIM), lambda h, i: (i, 0)),
        pl.BlockSpec((WINDOW, HEAD_DIM), lambda h, i: (prev(i), 0)),
        pl.BlockSpec((WINDOW, HEAD_DIM), lambda h, i: (prev(i), 0)),
        pl.BlockSpec((WINDOW, HEAD_DIM), lambda h, i: (nxt(i), 0)),
        pl.BlockSpec((WINDOW, HEAD_DIM), lambda h, i: (nxt(i), 0)),
    ]
    return pl.pallas_call(
        functools.partial(_attn_kernel, layer=layer),
        grid=(N_KV_HEADS, NS),
        in_specs=[pl.BlockSpec(memory_space=pltpu.SMEM),
                  pl.BlockSpec((TS, GQA * HEAD_DIM), lambda h, i: (i, qcol + h))]
                 + kv_specs(kcol) + kv_specs(vcol)
                 + [pl.BlockSpec((TS, HEAD_DIM), lambda h, i: (ctx_blk, kcol + h)),
                    pl.BlockSpec((TS, HEAD_DIM), lambda h, i: (ctx_blk, vcol + h))]
                 + tab_specs,
        out_specs=pl.BlockSpec((TS, GQA * HEAD_DIM), lambda h, i: (i, h)),
        out_shape=jax.ShapeDtypeStruct((T, N_Q_HEADS * HEAD_DIM), bf16),
        compiler_params=_cparams(("arbitrary", "arbitrary")),
        name="attention",
    )(attn_sink, z, z, z, z, z, z, z, z, z, cos_t, sin_t, cos_t, sin_t, cos_t, sin_t)


def _merge_kernel(h_ref, p_ref, a_ref, c_ref, zg0, zg1, zg2, bg0, bg1, bg2, wp, wa, wc, wo,
                  ng_ref, g1_ref, o_ref):
    m = pl.program_id(0)
    n = pl.program_id(1)

    def gate(zg, bg):
        return _sigmoid(zg[...].astype(f32) + bg[...])

    merged = (gate(zg0, bg0) * _dot(p_ref[...], wp[...].astype(bf16))
              + gate(zg1, bg1) * _dot(a_ref[...], wa[...].astype(bf16))
              + gate(zg2, bg2) * _dot(c_ref[...], wc[...].astype(bf16)))
    part = _dot(merged.astype(bf16), wo[...].astype(bf16))

    @pl.when(n == 0)
    def _():
        o_ref[...] = part

    @pl.when(n > 0)
    def _():
        o_ref[...] += part

    @pl.when(n == pl.num_programs(1) - 1)
    def _():
        _residual_tile(h_ref, o_ref, ng_ref[1:2, :], g1_ref, m)


def _merge(h, z, pool_o, attn_o, conv_o, b_gate, w_pool_up, w_attn_up, w_conv_up, w_out,
           norm_g, mod, layer):
    nb = D // TN_MERGE
    bg = b_gate.reshape(DEPTH, 1, GATE_WIDTH)
    zgate = lambda b: pl.BlockSpec((TM, TN_MERGE), lambda m, n: (m, b * nb + n))
    bgate = lambda b: pl.BlockSpec((None, 1, TN_MERGE), lambda m, n: (layer, 0, b * nb + n))
    wup = lambda k: pl.BlockSpec((None, k, TN_MERGE), lambda m, n: (layer, 0, n))
    return pl.pallas_call(
        _merge_kernel,
        grid=(NM, nb),
        in_specs=[
            pl.BlockSpec((TM, D), lambda m, n: (m, 0), pipeline_mode=pl.Buffered(1)),
            pl.BlockSpec((TM, POOL_WIDTH), lambda m, n: (m, 0)),
            pl.BlockSpec((TM, N_Q_HEADS * HEAD_DIM), lambda m, n: (m, 0)),
            pl.BlockSpec((TM, CONV_WIDTH), lambda m, n: (m, 0)),
            zgate(0), zgate(1), zgate(2), bgate(0), bgate(1), bgate(2),
            wup(POOL_WIDTH), wup(N_Q_HEADS * HEAD_DIM), wup(CONV_WIDTH),
            pl.BlockSpec((None, TN_MERGE, D), lambda m, n: (layer, n, 0)),
            pl.BlockSpec((None, 4, D), lambda m, n: (layer, 0, 0)),
            pl.BlockSpec((None, MOD_ROWS, D), lambda m, n: (layer, 0, 2)),
        ],
        out_specs=pl.BlockSpec((TM, D), lambda m, n: (m, 0)),
        out_shape=jax.ShapeDtypeStruct((T, D), f32),
        compiler_params=_cparams(("arbitrary", "arbitrary")),
        name="merge",
    )(h, pool_o, attn_o, conv_o, z, z, z, bg, bg, bg, w_pool_up, w_attn_up, w_conv_up, w_out,
      norm_g, mod)


def _ffn_prologue(h_ref, ng_ref, sh_ref, sc_ref, u_ref, m):
    def store(rows, u):
        u_ref[rows, :] = u.astype(bf16)
    _norm_mod_tile(h_ref, ng_ref[2:3, :], sh_ref, sc_ref, m, store)


def _ffn_epilogue(h_ref, ng_ref, g2_ref, o_ref, m):
    _residual_tile(h_ref, o_ref, ng_ref[3:4, :], g2_ref, m)


def _ffn_kernel(h_ref, ng_ref, sh_ref, sc_ref, g2_ref, w1, w3, w2, o_ref, u_ref):
    m = pl.program_id(0)
    f = pl.program_id(1)

    @pl.when(f == 0)
    def _():
        _ffn_prologue(h_ref, ng_ref, sh_ref, sc_ref, u_ref, m)

    u = u_ref[...]
    act = _silu(_dot(u, w1[...].astype(bf16))) * _dot(u, w3[...].astype(bf16))
    part = _dot(act.astype(bf16), w2[...].astype(bf16))

    @pl.when(f == 0)
    def _():
        o_ref[...] = part

    @pl.when(f > 0)
    def _():
        o_ref[...] += part

    @pl.when(f == pl.num_programs(1) - 1)
    def _():
        _ffn_epilogue(h_ref, ng_ref, g2_ref, o_ref, m)


def _mod_specs(layer, chunks):
    specs = [pl.BlockSpec((TM, D), lambda m, *r: (m, 0), pipeline_mode=pl.Buffered(1)),
             pl.BlockSpec((None, 4, D), lambda m, *r: (layer, 0, 0))]
    for ch in chunks:
        specs.append(pl.BlockSpec((None, MOD_ROWS, D), lambda m, *r, ch=ch: (layer, 0, ch)))
    return specs


def _dense_ffn(h, norm_g, mod, w1, w3, w2, layer):
    j = layer // 2
    return pl.pallas_call(
        _ffn_kernel,
        grid=(NM, FFN // TF),
        in_specs=_mod_specs(layer, (3, 4, 5)) + [
            pl.BlockSpec((None, D, TF), lambda m, f: (j, 0, f)),
            pl.BlockSpec((None, D, TF), lambda m, f: (j, 0, f)),
            pl.BlockSpec((None, TF, D), lambda m, f: (j, f, 0)),
        ],
        out_specs=pl.BlockSpec((TM, D), lambda m, f: (m, 0)),
        out_shape=jax.ShapeDtypeStruct((T, D), f32),
        scratch_shapes=[pltpu.VMEM((TM, D), bf16)],
        compiler_params=_cparams(("arbitrary", "arbitrary")),
        name="dense_ffn",
    )(h, norm_g, mod, mod, mod, w1, w3, w2)


def _split_bf16(x):
    hi = x.astype(bf16)
    lo = (x - hi.astype(f32)).astype(bf16)
    return hi, lo


def _router_kernel(h_ref, ng_ref, sh_ref, sc_ref, rw_ref, g_ref):
    m = pl.program_id(0)
    wh, wl = _split_bf16(rw_ref[...])

    def store(rows, u):
        uh, ul = _split_bf16(u)
        logits = _dot(uh, wh) + (_dot(uh, wl) + _dot(ul, wh))
        lane = lax.broadcasted_iota(jnp.int32, logits.shape, 1)
        lg = jnp.where(lane < N_EXPERTS, logits, -jnp.inf)
        m1 = jnp.max(lg, axis=-1, keepdims=True)
        i1 = jnp.min(jnp.where(lg == m1, lane, ROUTER_PAD), axis=-1, keepdims=True)
        lg2 = jnp.where(lane == i1, -jnp.inf, lg)
        m2 = jnp.max(lg2, axis=-1, keepdims=True)
        i2 = jnp.min(jnp.where(lg2 == m2, lane, ROUTER_PAD), axis=-1, keepdims=True)
        e2 = jnp.exp(m2 - m1)
        den = 1.0 + e2
        g_ref[rows, :] = jnp.where(lane == i1, 1.0 / den, 0.0) + jnp.where(lane == i2, e2 / den, 0.0)

    _norm_mod_tile(h_ref, ng_ref[2:3, :], sh_ref, sc_ref, m, store)


def _router(h, norm_g, mod, router_w_pad, layer):
    j = layer // 2
    return pl.pallas_call(
        _router_kernel,
        grid=(NM,),
        in_specs=_mod_specs(layer, (3, 4)) + [
            pl.BlockSpec((None, D, ROUTER_PAD), lambda m: (j, 0, 0)),
        ],
        out_specs=pl.BlockSpec((TM, ROUTER_PAD), lambda m: (m, 0)),
        out_shape=jax.ShapeDtypeStruct((T, ROUTER_PAD), f32),
        compiler_params=_cparams(("arbitrary",)),
        name="router",
    )(h, norm_g, mod, mod, router_w_pad)


def _moe_kernel(h_ref, ng_ref, sh_ref, sc_ref, g2_ref, gate_ref, w1, w3, w2, o_ref, u_ref):
    m = pl.program_id(0)
    e = pl.program_id(1)
    f = pl.program_id(2)
    first = jnp.logical_and(e == 0, f == 0)

    @pl.when(first)
    def _():
        _ffn_prologue(h_ref, ng_ref, sh_ref, sc_ref, u_ref, m)

    lane = lax.broadcasted_iota(jnp.int32, (TM, ROUTER_PAD), 1)
    gcol = jnp.sum(jnp.where(lane == e, gate_ref[...], 0.0), axis=-1, keepdims=True)
    u = u_ref[...]
    act = _silu(_dot(u, w1[...].astype(bf16))) * _dot(u, w3[...].astype(bf16)) * gcol
    part = _dot(act.astype(bf16), w2[...].astype(bf16))

    @pl.when(first)
    def _():
        o_ref[...] = part

    @pl.when(jnp.logical_not(first))
    def _():
        o_ref[...] += part

    @pl.when(jnp.logical_and(e == pl.num_programs(1) - 1, f == pl.num_programs(2) - 1))
    def _():
        _ffn_epilogue(h_ref, ng_ref, g2_ref, o_ref, m)


def _moe_ffn(h, gates, norm_g, mod, w1, w3, w2, layer):
    j = layer // 2
    return pl.pallas_call(
        _moe_kernel,
        grid=(NM, N_EXPERTS, FFN // TF),
        in_specs=_mod_specs(layer, (3, 4, 5)) + [
            pl.BlockSpec((TM, ROUTER_PAD), lambda m, e, f: (m, 0)),
            pl.BlockSpec((None, None, D, TF), lambda m, e, f: (j, e, 0, f)),
            pl.BlockSpec((None, None, D, TF), lambda m, e, f: (j, e, 0, f)),
            pl.BlockSpec((None, None, TF, D), lambda m, e, f: (j, e, f, 0)),
        ],
        out_specs=pl.BlockSpec((TM, D), lambda m, e, f: (m, 0)),
        out_shape=jax.ShapeDtypeStruct((T, D), f32),
        scratch_shapes=[pltpu.VMEM((TM, D), bf16)],
        compiler_params=_cparams(("arbitrary", "arbitrary", "arbitrary")),
        name="moe_ffn",
    )(h, norm_g, mod, mod, mod, gates, w1, w3, w2)


def _rope_tables():
    t = jnp.arange(S)
    row = (t // GRID_W).astype(f32)
    col = (t % GRID_W).astype(f32)
    inv = ROPE_BASE ** (-jnp.arange(ROPE_FREQS, dtype=f32) / ROPE_FREQS)
    a0 = row[:, None] * inv
    a1 = col[:, None] * inv
    cos_t = jnp.concatenate([jnp.cos(a0), jnp.cos(a0), jnp.cos(a1), jnp.cos(a1)], axis=1)
    sin_t = jnp.concatenate([-jnp.sin(a0), jnp.sin(a0), -jnp.sin(a1), jnp.sin(a1)], axis=1)
    cos_t = jnp.concatenate([cos_t, jnp.ones((C, HEAD_DIM), f32)], axis=0)
    sin_t = jnp.concatenate([sin_t, jnp.zeros((C, HEAD_DIM), f32)], axis=0)
    return cos_t, sin_t


def kernel(x, c, ctx, c_ctx, w_mod, b_mod, norm_g, w_in, b_gate, pool_w, pool_scale, w_pool_up,
           attn_sink, w_attn_up, conv_dw, conv_db, conv_ln_g, conv_ln_b, w_conv_up, w_out,
           ffn_w1, ffn_w3, ffn_w2, router_w, moe_w1, moe_w3, moe_w2):
    assert x.shape == (1, S, D) and ctx.shape == (1, C, D)
    cos_t, sin_t = _rope_tables()
    cc = jnp.concatenate([c, c_ctx[None, :], jnp.zeros((MOD_ROWS - 2, D), f32)], axis=0)
    mod = _modulation(cc, w_mod, b_mod)
    router_w_pad = jnp.pad(router_w, ((0, 0), (0, 0), (0, ROUTER_PAD - N_EXPERTS)))
    h = jnp.concatenate([x[0], ctx[0]], axis=0)
    for i in range(DEPTH):
        z = _in_proj(h, norm_g, mod, w_in, i)
        pool_o, conv_o = _seq_mixers(z, pool_w, pool_scale, conv_dw, conv_db, conv_ln_g, conv_ln_b, i)
        attn_o = _attention(z, attn_sink, cos_t, sin_t, i)
        h = _merge(h, z, pool_o, attn_o, conv_o, b_gate, w_pool_up, w_attn_up, w_conv_up, w_out,
                   norm_g, mod, i)
        if i % 2 == 0:
            h = _dense_ffn(h, norm_g, mod, ffn_w1, ffn_w3, ffn_w2, i)
        else:
            gates = _router(h, norm_g, mod, router_w_pad, i)
            h = _moe_ffn(h, gates, norm_g, mod, moe_w1, moe_w3, moe_w2, i)
    return h[:S][None]
```

```python
import functools

import jax
import jax.numpy as jnp
from jax import lax
from jax.experimental import pallas as pl
from jax.experimental.pallas import tpu as pltpu

f32 = jnp.float32
bf16 = jnp.bfloat16

D = 2048
S = 8192
C = 256
T = S + C
DEPTH = 4
GRID_W = 64
HEAD_DIM = 128
POOL_WIDTH = 512
POOL_WINDOWS = (2, 4, 8, 16)
POOL_GROUP = 128
N_Q_HEADS = 8
N_KV_HEADS = 2
GQA = 4
WINDOW = 128
ATTN_SCALE = HEAD_DIM ** -0.5
ROPE_BASE = 10000.0
ROPE_FREQS = 32
CONV_WIDTH = 512
CONV_K = 31
GATE_WIDTH = 3 * D
POOL_OFF = GATE_WIDTH
Q_OFF = POOL_OFF + POOL_WIDTH
K_OFF = Q_OFF + N_Q_HEADS * HEAD_DIM
V_OFF = K_OFF + N_KV_HEADS * HEAD_DIM
CONV_OFF = V_OFF + N_KV_HEADS * HEAD_DIM
IN_WIDTH = CONV_OFF + 2 * CONV_WIDTH
FFN = 5632
N_EXPERTS = 8
N_MOD = 6
EPS = 1e-6
NEG_INF = -1e30

VMEM_LIMIT_BYTES = 56 * 1024 * 1024

TM = 768
NM = T // TM
TS = 256
NS = T // TS
NS_LAT = S // TS
HALO = 16
TN_IN = 1024
TN_MERGE = 256
TF = 256
TN_MOD = 1024
ROUTER_PAD = 128
ROW_CHUNK = 128
TG = 512
NT = (2 * T + N_EXPERTS * (TG - 1) + TG - 1) // TG
MOD_ROWS = 16


def _cparams(semantics):
    return pltpu.CompilerParams(dimension_semantics=semantics,
                                vmem_limit_bytes=VMEM_LIMIT_BYTES)


def _sigmoid(x):
    return 1.0 / (1.0 + jnp.exp(-x))


def _silu(x):
    return x * _sigmoid(x)


def _mod_rows(mod_ref, row0, nrows):
    rows = row0 + lax.broadcasted_iota(jnp.int32, (nrows, 1), 0)
    return jnp.where(rows >= S, mod_ref[1:2, :], mod_ref[0:1, :])


def _rms(x, g):
    return x * lax.rsqrt(jnp.mean(x * x, axis=-1, keepdims=True) + EPS) * g


def _for_row_chunks(fn):
    def step(r, carry):
        off = pl.multiple_of(r * ROW_CHUNK, ROW_CHUNK)
        fn(pl.ds(off, ROW_CHUNK), off)
        return carry
    lax.fori_loop(0, TM // ROW_CHUNK, step, 0)


def _norm_mod_tile(h_ref, g, sh_ref, sc_ref, m, store):
    def chunk(rows, off):
        row0 = m * TM + off
        u = (_rms(h_ref[rows, :], g) * (1.0 + _mod_rows(sc_ref, row0, ROW_CHUNK))
             + _mod_rows(sh_ref, row0, ROW_CHUNK))
        store(rows, u)
    _for_row_chunks(chunk)


def _residual_tile(h_ref, y_ref, g, gate_ref, m):
    def chunk(rows, off):
        y_ref[rows, :] = h_ref[rows, :] + _mod_rows(gate_ref, m * TM + off, ROW_CHUNK) * _rms(y_ref[rows, :], g)
    _for_row_chunks(chunk)


def _dot(a, b):
    return jnp.dot(a, b, preferred_element_type=f32)


def _mod_kernel(cc_ref, w_ref, b_ref, o_ref):
    s = _silu(cc_ref[...])
    o_ref[...] = _dot(s.astype(bf16), w_ref[...].astype(bf16)) + b_ref[...]


def _modulation(cc, w_mod, b_mod):
    nn = (N_MOD * D) // TN_MOD
    return pl.pallas_call(
        _mod_kernel,
        grid=(DEPTH, nn),
        in_specs=[
            pl.BlockSpec((MOD_ROWS, D), lambda l, n: (0, 0)),
            pl.BlockSpec((None, D, TN_MOD), lambda l, n: (l, 0, n)),
            pl.BlockSpec((None, 1, TN_MOD), lambda l, n: (l, 0, n)),
        ],
        out_specs=pl.BlockSpec((None, MOD_ROWS, TN_MOD), lambda l, n: (l, 0, n)),
        out_shape=jax.ShapeDtypeStruct((DEPTH, MOD_ROWS, N_MOD * D), f32),
        compiler_params=_cparams(("arbitrary", "arbitrary")),
        name="modulation",
    )(cc, w_mod, b_mod.reshape(DEPTH, 1, N_MOD * D))


def _win_kernel(h_ref, ng_ref, sh_ref, sc_ref, w_ref, z_ref, u_ref):
    m = pl.program_id(0)

    @pl.when(pl.program_id(1) == 0)
    def _():
        def store(rows, u):
            u_ref[rows, :] = u.astype(bf16)
        _norm_mod_tile(h_ref, ng_ref[0:1, :], sh_ref, sc_ref, m, store)

    z_ref[...] = _dot(u_ref[...], w_ref[...].astype(bf16)).astype(bf16)


def _in_proj(h, norm_g, mod, w_in, layer):
    return pl.pallas_call(
        _win_kernel,
        grid=(NM, IN_WIDTH // TN_IN),
        in_specs=[
            pl.BlockSpec((TM, D), lambda m, n: (m, 0)),
            pl.BlockSpec((None, 4, D), lambda m, n: (layer, 0, 0)),
            pl.BlockSpec((None, MOD_ROWS, D), lambda m, n: (layer, 0, 0)),
            pl.BlockSpec((None, MOD_ROWS, D), lambda m, n: (layer, 0, 1)),
            pl.BlockSpec((None, D, TN_IN), lambda m, n: (layer, 0, n)),
        ],
        out_specs=pl.BlockSpec((TM, TN_IN), lambda m, n: (m, n)),
        out_shape=jax.ShapeDtypeStruct((T, IN_WIDTH), bf16),
        scratch_shapes=[pltpu.VMEM((TM, D), bf16)],
        compiler_params=_cparams(("arbitrary", "arbitrary")),
        name="in_proj",
    )(h, norm_g, mod, mod, w_in)


def _seqmix_kernel(zp_c, zp_p, zp_n, zc_c, zc_p, zc_n, pw_ref, ps_ref, dw_ref, db_ref,
                   lng_ref, lnb_ref, po_ref, co_ref, xe, ge):
    i = pl.program_id(0)
    has_prev = jnp.where(jnp.logical_and(i != 0, i != NS_LAT), 1.0, 0.0).astype(f32)
    has_next = jnp.where(i < NS_LAT - 1, 1.0, 0.0).astype(f32)
    is_lat = i < NS_LAT
    pos = jnp.where(is_lat, i * TS, 0) + lax.broadcasted_iota(jnp.int32, (TS, 1), 0)
    seq_len = jnp.where(is_lat, S, C)

    xe[0:HALO, :] = zp_p[...].astype(f32) * has_prev
    xe[HALO:HALO + TS, :] = zp_c[...].astype(f32)
    xe[HALO + TS:, :] = zp_n[...].astype(f32) * has_next
    outs = []
    for g, w in enumerate(POOL_WINDOWS):
        cols = slice(g * POOL_GROUP, (g + 1) * POOL_GROUP)
        acc = xe[pl.ds(HALO - w // 2, TS), cols]
        for d in range(-w // 2 + 1, w // 2):
            acc = acc + xe[pl.ds(HALO + d, TS), cols]
        lo = jnp.clip(pos - w // 2, 0, seq_len)
        hi = jnp.clip(pos - w // 2 + w, 0, seq_len)
        diff = acc / (hi - lo).astype(f32) - xe[HALO:HALO + TS, cols]
        outs.append(_dot(diff.astype(bf16), pw_ref[g].astype(bf16)))
    po_ref[...] = (jnp.concatenate(outs, axis=1) * ps_ref[...]).astype(bf16)

    def glu(ref):
        zz = ref[...].astype(f32)
        return zz[:, :CONV_WIDTH] * _sigmoid(zz[:, CONV_WIDTH:])

    ge[0:HALO, :] = glu(zc_p) * has_prev
    ge[HALO:HALO + TS, :] = glu(zc_c)
    ge[HALO + TS:, :] = glu(zc_n) * has_next
    base = HALO - CONV_K // 2
    acc = ge[pl.ds(base, TS), :] * dw_ref[0:1, :]
    for k in range(1, CONV_K):
        acc = acc + ge[pl.ds(base + k, TS), :] * dw_ref[k:k + 1, :]
    y = acc + db_ref[...]
    yc = y - jnp.mean(y, axis=-1, keepdims=True)
    yn = yc * lax.rsqrt(jnp.mean(yc * yc, axis=-1, keepdims=True) + EPS) * lng_ref[...] + lnb_ref[...]
    co_ref[...] = _silu(yn).astype(bf16)


def _seq_mixers(z, pool_w, pool_scale, conv_dw, conv_db, conv_ln_g, conv_ln_b, layer):
    rb = TS // HALO
    last = T // HALO - 1
    pcol = POOL_OFF // POOL_WIDTH
    ccol = CONV_OFF // (2 * CONV_WIDTH)

    def prev(i):
        return jnp.maximum(i * rb - 1, 0)

    def nxt(i):
        return jnp.minimum((i + 1) * rb, last)

    vec = lambda width: pl.BlockSpec((None, 1, width), lambda i: (layer, 0, 0))
    return pl.pallas_call(
        _seqmix_kernel,
        grid=(NS,),
        in_specs=[
            pl.BlockSpec((TS, POOL_WIDTH), lambda i: (i, pcol)),
            pl.BlockSpec((HALO, POOL_WIDTH), lambda i: (prev(i), pcol)),
            pl.BlockSpec((HALO, POOL_WIDTH), lambda i: (nxt(i), pcol)),
            pl.BlockSpec((TS, 2 * CONV_WIDTH), lambda i: (i, ccol)),
            pl.BlockSpec((HALO, 2 * CONV_WIDTH), lambda i: (prev(i), ccol)),
            pl.BlockSpec((HALO, 2 * CONV_WIDTH), lambda i: (nxt(i), ccol)),
            pl.BlockSpec((None, 4, POOL_GROUP, POOL_GROUP), lambda i: (layer, 0, 0, 0)),
            vec(POOL_WIDTH),
            pl.BlockSpec((None, CONV_K, CONV_WIDTH), lambda i: (layer, 0, 0)),
            vec(CONV_WIDTH), vec(CONV_WIDTH), vec(CONV_WIDTH),
        ],
        out_specs=[
            pl.BlockSpec((TS, POOL_WIDTH), lambda i: (i, 0)),
            pl.BlockSpec((TS, CONV_WIDTH), lambda i: (i, 0)),
        ],
        out_shape=[
            jax.ShapeDtypeStruct((T, POOL_WIDTH), bf16),
            jax.ShapeDtypeStruct((T, CONV_WIDTH), bf16),
        ],
        scratch_shapes=[
            pltpu.VMEM((TS + 2 * HALO, POOL_WIDTH), f32),
            pltpu.VMEM((TS + 2 * HALO, CONV_WIDTH), f32),
        ],
        compiler_params=_cparams(("arbitrary",)),
        name="seq_mixers",
    )(z, z, z, z, z, z, pool_w, pool_scale.reshape(DEPTH, 1, POOL_WIDTH), conv_dw,
      conv_db.reshape(DEPTH, 1, CONV_WIDTH), conv_ln_g.reshape(DEPTH, 1, CONV_WIDTH),
      conv_ln_b.reshape(DEPTH, 1, CONV_WIDTH))


def _rope(x, cos_t, sin_t):
    lane = lax.broadcasted_iota(jnp.int32, x.shape, 1) & (2 * ROPE_FREQS - 1)
    partner = jnp.where(lane < ROPE_FREQS, pltpu.roll(x, HEAD_DIM - ROPE_FREQS, 1),
                        pltpu.roll(x, ROPE_FREQS, 1))
    return x * cos_t + partner * sin_t


def _attn_kernel(sink_ref, q_ref, kc_ref, kp_ref, kn_ref, vc_ref, vp_ref, vn_ref, kx_ref, vx_ref,
                 cq_ref, sq_ref, cp_ref, sp_ref, cn_ref, sn_ref, o_ref, *, layer):
    hk = pl.program_id(0)
    i = pl.program_id(1)
    cq = cq_ref[...]
    sq = sq_ref[...]
    qs = []
    for j in range(GQA):
        qj = q_ref[:, j * HEAD_DIM:(j + 1) * HEAD_DIM].astype(f32)
        qs.append(_rope(qj, cq, sq).astype(bf16))
    q = jnp.concatenate(qs, axis=0)
    k = jnp.concatenate([
        _rope(kp_ref[...].astype(f32), cp_ref[...], sp_ref[...]).astype(bf16),
        _rope(kc_ref[...].astype(f32), cq, sq).astype(bf16),
        _rope(kn_ref[...].astype(f32), cn_ref[...], sn_ref[...]).astype(bf16)], axis=0)
    v = jnp.concatenate([vp_ref[...], vc_ref[...], vn_ref[...]], axis=0)
    kw = k.shape[0]

    nt = (((1,), (1,)), ((), ()))
    s_loc = lax.dot_general(q, k, nt, preferred_element_type=f32) * ATTN_SCALE
    s_ctx = lax.dot_general(q, kx_ref[...], nt, preferred_element_type=f32) * ATTN_SCALE

    qrow = lax.broadcasted_iota(jnp.int32, (GQA * TS, 1), 0) & (TS - 1)
    krel = lax.broadcasted_iota(jnp.int32, (1, kw), 1) - WINDOW
    kpos = i * TS + krel
    k_ok = jnp.logical_and(jnp.logical_and(kpos >= 0, kpos < S), i < NS_LAT)
    valid = jnp.logical_and(jnp.abs(krel - qrow) <= WINDOW, k_ok)
    s_loc = jnp.where(valid, s_loc, NEG_INF)

    sink = jnp.concatenate(
        [jnp.full((TS, 1), sink_ref[layer, hk * GQA + j], f32) for j in range(GQA)], axis=0)
    mx = jnp.maximum(jnp.maximum(jnp.max(s_loc, axis=-1, keepdims=True),
                                 jnp.max(s_ctx, axis=-1, keepdims=True)), sink)
    p_loc = jnp.exp(s_loc - mx)
    p_ctx = jnp.exp(s_ctx - mx)
    denom = (jnp.sum(p_loc, axis=-1, keepdims=True) + jnp.sum(p_ctx, axis=-1, keepdims=True)
             + jnp.exp(sink - mx))
    o = (_dot(p_loc.astype(bf16), v) + _dot(p_ctx.astype(bf16), vx_ref[...])) / denom
    for j in range(GQA):
        o_ref[:, j * HEAD_DIM:(j + 1) * HEAD_DIM] = o[j * TS:(j + 1) * TS, :].astype(bf16)


def _attention(z, attn_sink, cos_t, sin_t, layer):
    qcol = Q_OFF // (GQA * HEAD_DIM)
    kcol = K_OFF // HEAD_DIM
    vcol = V_OFF // HEAD_DIM
    hb = TS // WINDOW
    last = T // WINDOW - 1
    ctx_blk = S // TS

    def prev(i):
        return jnp.maximum(i * hb - 1, 0)

    def nxt(i):
        return jnp.minimum((i + 1) * hb, last)

    def kv_specs(col):
        return [
            pl.BlockSpec((TS, HEAD_DIM), lambda h, i: (i, col + h)),
            pl.BlockSpec((WINDOW, HEAD_DIM), lambda h, i: (prev(i), col + h)),
            pl.BlockSpec((WINDOW, HEAD_DIM), lambda h, i: (nxt(i), col + h)),
        ]

    tab_specs = [
        pl.BlockSpec((TS, HEAD_DIM), lambda h, i: (i, 0)),
        pl.BlockSpec((TS, HEAD_DIM), lambda h, i: (i, 0)),
        pl.BlockSpec((WINDOW, HEAD_DIM), lambda h, i: (prev(i), 0)),
        pl.BlockSpec((WINDOW, HEAD_DIM), lambda h, i: (prev(i), 0)),
        pl.BlockSpec((WINDOW, HEAD_DIM), lambda h, i: (nxt(i), 0)),
        pl.BlockSpec((WINDOW, HEAD_DIM), lambda h, i: (nxt(i), 0)),
    ]
    return pl.pallas_call(
        functools.partial(_attn_kernel, layer=layer),
        grid=(N_KV_HEADS, NS),
        in_specs=[pl.BlockSpec(memory_space=pltpu.SMEM),
                  pl.BlockSpec((TS, GQA * HEAD_DIM), lambda h, i: (i, qcol + h))]
                 + kv_specs(kcol) + kv_specs(vcol)
                 + [pl.BlockSpec((TS, HEAD_DIM), lambda h, i: (ctx_blk, kcol + h)),
                    pl.BlockSpec((TS, HEAD_DIM), lambda h, i: (ctx_blk, vcol + h))]
                 + tab_specs,
        out_specs=pl.BlockSpec((TS, GQA * HEAD_DIM), lambda h, i: (i, h)),
        out_shape=jax.ShapeDtypeStruct((T, N_Q_HEADS * HEAD_DIM), bf16),
        compiler_params=_cparams(("arbitrary", "arbitrary")),
        name="attention",
    )(attn_sink, z, z, z, z, z, z, z, z, z, cos_t, sin_t, cos_t, sin_t, cos_t, sin_t)


def _merge_kernel(h_ref, p_ref, a_ref, c_ref, zg0, zg1, zg2, bg0, bg1, bg2, wp, wa, wc, wo,
                  ng_ref, g1_ref, o_ref):
    m = pl.program_id(0)
    n = pl.program_id(1)

    def gate(zg, bg):
        return _sigmoid(zg[...].astype(f32) + bg[...])

    merged = (gate(zg0, bg0) * _dot(p_ref[...], wp[...].astype(bf16))
              + gate(zg1, bg1) * _dot(a_ref[...], wa[...].astype(bf16))
              + gate(zg2, bg2) * _dot(c_ref[...], wc[...].astype(bf16)))
    part = _dot(merged.astype(bf16), wo[...].astype(bf16))

    @pl.when(n == 0)
    def _():
        o_ref[...] = part

    @pl.when(n > 0)
    def _():
        o_ref[...] += part

    @pl.when(n == pl.num_programs(1) - 1)
    def _():
        _residual_tile(h_ref, o_ref, ng_ref[1:2, :], g1_ref, m)


def _merge(h, z, pool_o, attn_o, conv_o, b_gate, w_pool_up, w_attn_up, w_conv_up, w_out,
           norm_g, mod, layer):
    nb = D // TN_MERGE
    bg = b_gate.reshape(DEPTH, 1, GATE_WIDTH)
    zgate = lambda b: pl.BlockSpec((TM, TN_MERGE), lambda m, n: (m, b * nb + n))
    bgate = lambda b: pl.BlockSpec((None, 1, TN_MERGE), lambda m, n: (layer, 0, b * nb + n))
    wup = lambda k: pl.BlockSpec((None, k, TN_MERGE), lambda m, n: (layer, 0, n))
    return pl.pallas_call(
        _merge_kernel,
        grid=(NM, nb),
        in_specs=[
            pl.BlockSpec((TM, D), lambda m, n: (m, 0), pipeline_mode=pl.Buffered(1)),
            pl.BlockSpec((TM, POOL_WIDTH), lambda m, n: (m, 0)),
            pl.BlockSpec((TM, N_Q_HEADS * HEAD_DIM), lambda m, n: (m, 0)),
            pl.BlockSpec((TM, CONV_WIDTH), lambda m, n: (m, 0)),
            zgate(0), zgate(1), zgate(2), bgate(0), bgate(1), bgate(2),
            wup(POOL_WIDTH), wup(N_Q_HEADS * HEAD_DIM), wup(CONV_WIDTH),
            pl.BlockSpec((None, TN_MERGE, D), lambda m, n: (layer, n, 0)),
            pl.BlockSpec((None, 4, D), lambda m, n: (layer, 0, 0)),
            pl.BlockSpec((None, MOD_ROWS, D), lambda m, n: (layer, 0, 2)),
        ],
        out_specs=pl.BlockSpec((TM, D), lambda m, n: (m, 0)),
        out_shape=jax.ShapeDtypeStruct((T, D), f32),
        compiler_params=_cparams(("arbitrary", "arbitrary")),
        name="merge",
    )(h, pool_o, attn_o, conv_o, z, z, z, bg, bg, bg, w_pool_up, w_attn_up, w_conv_up, w_out,
      norm_g, mod)


def _ffn_prologue(h_ref, ng_ref, sh_ref, sc_ref, u_ref, m):
    def store(rows, u):
        u_ref[rows, :] = u.astype(bf16)
    _norm_mod_tile(h_ref, ng_ref[2:3, :], sh_ref, sc_ref, m, store)


def _ffn_epilogue(h_ref, ng_ref, g2_ref, o_ref, m):
    _residual_tile(h_ref, o_ref, ng_ref[3:4, :], g2_ref, m)


def _ffn_kernel(h_ref, ng_ref, sh_ref, sc_ref, g2_ref, w1, w3, w2, o_ref, u_ref):
    m = pl.program_id(0)
    f = pl.program_id(1)

    @pl.when(f == 0)
    def _():
        _ffn_prologue(h_ref, ng_ref, sh_ref, sc_ref, u_ref, m)

    u = u_ref[...]
    act = _silu(_dot(u, w1[...].astype(bf16))) * _dot(u, w3[...].astype(bf16))
    part = _dot(act.astype(bf16), w2[...].astype(bf16))

    @pl.when(f == 0)
    def _():
        o_ref[...] = part

    @pl.when(f > 0)
    def _():
        o_ref[...] += part

    @pl.when(f == pl.num_programs(1) - 1)
    def _():
        _ffn_epilogue(h_ref, ng_ref, g2_ref, o_ref, m)


def _mod_specs(layer, chunks):
    specs = [pl.BlockSpec((TM, D), lambda m, *r: (m, 0), pipeline_mode=pl.Buffered(1)),
             pl.BlockSpec((None, 4, D), lambda m, *r: (layer, 0, 0))]
    for ch in chunks:
        specs.append(pl.BlockSpec((None, MOD_ROWS, D), lambda m, *r, ch=ch: (layer, 0, ch)))
    return specs


def _dense_ffn(h, norm_g, mod, w1, w3, w2, layer):
    j = layer // 2
    return pl.pallas_call(
        _ffn_kernel,
        grid=(NM, FFN // TF),
        in_specs=_mod_specs(layer, (3, 4, 5)) + [
            pl.BlockSpec((None, D, TF), lambda m, f: (j, 0, f)),
            pl.BlockSpec((None, D, TF), lambda m, f: (j, 0, f)),
            pl.BlockSpec((None, TF, D), lambda m, f: (j, f, 0)),
        ],
        out_specs=pl.BlockSpec((TM, D), lambda m, f: (m, 0)),
        out_shape=jax.ShapeDtypeStruct((T, D), f32),
        scratch_shapes=[pltpu.VMEM((TM, D), bf16)],
        compiler_params=_cparams(("arbitrary", "arbitrary")),
        name="dense_ffn",
    )(h, norm_g, mod, mod, mod, w1, w3, w2)


def _split_bf16(x):
    hi = x.astype(bf16)
    lo = (x - hi.astype(f32)).astype(bf16)
    return hi, lo


R_E0, R_E1, R_G0, R_G1, R_R0, R_R1 = range(6)


def _router_kernel(h_ref, ng_ref, sh_ref, sc_ref, rw_ref, route_ref, cnt_ref, u_ref, base_ref):
    m = pl.program_id(0)

    @pl.when(m == 0)
    def _():
        base_ref[...] = jnp.zeros_like(base_ref)

    wh, wl = _split_bf16(rw_ref[...])
    ri = lax.broadcasted_iota(jnp.int32, (ROW_CHUNK, ROW_CHUNK), 0)
    ci = lax.broadcasted_iota(jnp.int32, (ROW_CHUNK, ROW_CHUNK), 1)
    earlier = jnp.where(ri > ci, 1.0, 0.0).astype(bf16)

    def store(rows, u):
        u_ref[rows, :] = u
        uh, ul = _split_bf16(u)
        logits = _dot(uh, wh) + (_dot(uh, wl) + _dot(ul, wh))
        lane = lax.broadcasted_iota(jnp.int32, logits.shape, 1)
        lg = jnp.where(lane < N_EXPERTS, logits, -jnp.inf)
        m1 = jnp.max(lg, axis=-1, keepdims=True)
        i1 = jnp.min(jnp.where(lg == m1, lane, ROUTER_PAD), axis=-1, keepdims=True)
        lg2 = jnp.where(lane == i1, -jnp.inf, lg)
        m2 = jnp.max(lg2, axis=-1, keepdims=True)
        i2 = jnp.min(jnp.where(lg2 == m2, lane, ROUTER_PAD), axis=-1, keepdims=True)
        e2 = jnp.exp(m2 - m1)
        den = 1.0 + e2
        hit1 = lane == i1
        hit2 = lane == i2
        onehot = jnp.where(jnp.logical_or(hit1, hit2), 1.0, 0.0)
        before = _dot(earlier, onehot.astype(bf16)) + base_ref[...]
        r1 = jnp.sum(jnp.where(hit1, before, 0.0), axis=-1, keepdims=True)
        r2 = jnp.sum(jnp.where(hit2, before, 0.0), axis=-1, keepdims=True)
        base_ref[...] = base_ref[...] + jnp.sum(onehot, axis=0, keepdims=True)
        vals = (i1.astype(f32), i2.astype(f32), 1.0 / den, e2 / den, r1, r2)
        route = jnp.zeros(logits.shape, f32)
        for col, val in enumerate(vals):
            route = jnp.where(lane == col, val, route)
        route_ref[rows, :] = route

    _norm_mod_tile(h_ref, ng_ref[2:3, :], sh_ref, sc_ref, m, store)
    cnt_ref[...] = jnp.broadcast_to(base_ref[...], cnt_ref.shape)


def _router(h, norm_g, mod, router_w_pad, layer):
    j = layer // 2
    return pl.pallas_call(
        _router_kernel,
        grid=(NM,),
        in_specs=_mod_specs(layer, (3, 4)) + [
            pl.BlockSpec((None, D, ROUTER_PAD), lambda m: (j, 0, 0)),
        ],
        out_specs=[
            pl.BlockSpec((TM, ROUTER_PAD), lambda m: (m, 0)),
            pl.BlockSpec((8, ROUTER_PAD), lambda m: (0, 0)),
            pl.BlockSpec((TM, D), lambda m: (m, 0)),
        ],
        out_shape=[
            jax.ShapeDtypeStruct((T, ROUTER_PAD), f32),
            jax.ShapeDtypeStruct((8, ROUTER_PAD), f32),
            jax.ShapeDtypeStruct((T, D), f32),
        ],
        scratch_shapes=[pltpu.VMEM((1, ROUTER_PAD), f32)],
        compiler_params=_cparams(("arbitrary",)),
        name="router",
    )(h, norm_g, mod, mod, router_w_pad)


def _dispatch_plan(route, counts):
    expert = route[:, R_E0:R_E1 + 1].astype(jnp.int32)
    rank = route[:, R_R0:R_R1 + 1].astype(jnp.int32)
    cnt = counts[0, :N_EXPERTS].astype(jnp.int32)
    tiles = (cnt + TG - 1) // TG
    tile_end = jnp.cumsum(tiles)
    start = (tile_end - tiles) * TG
    dest = (start[expert] + rank).reshape(2 * T)
    n_used = tile_end[-1]
    tile_id = jnp.minimum(jnp.arange(NT, dtype=jnp.int32), n_used - 1)
    tile_expert = jnp.sum(tile_id[:, None] >= tile_end[None, :], axis=1).astype(jnp.int32)
    tile_used = (jnp.arange(NT, dtype=jnp.int32) < n_used).astype(jnp.int32)
    return dest, tile_expert, tile_used


def _row_copy(src_ref, src_row, dst_ref, dst_row, sem):
    return pltpu.make_async_copy(src_ref.at[pl.ds(src_row, 1)], dst_ref.at[pl.ds(dst_row, 1)], sem)


def _dispatch_kernel(dest_ref, u_ref, xs_in_ref, xs_ref, sem):
    del xs_in_ref
    m = pl.program_id(0)

    def issue(r, carry):
        t = m * TM + r
        _row_copy(u_ref, t, xs_ref, dest_ref[2 * t], sem).start()
        _row_copy(u_ref, t, xs_ref, dest_ref[2 * t + 1], sem).start()
        return carry

    lax.fori_loop(0, TM, issue, 0, unroll=8)
    for _ in range(2):
        pltpu.make_async_copy(u_ref.at[pl.ds(0, TM)], xs_ref.at[pl.ds(0, TM)], sem).wait()


def _dispatch(dest, u):
    xs0 = jnp.zeros((NT * TG, D), f32)
    return pl.pallas_call(
        _dispatch_kernel,
        grid_spec=pltpu.PrefetchScalarGridSpec(
            num_scalar_prefetch=1,
            grid=(NM,),
            in_specs=[pl.BlockSpec(memory_space=pl.ANY), pl.BlockSpec(memory_space=pl.ANY)],
            out_specs=pl.BlockSpec(memory_space=pl.ANY),
            scratch_shapes=[pltpu.SemaphoreType.DMA(())],
        ),
        out_shape=jax.ShapeDtypeStruct((NT * TG, D), f32),
        input_output_aliases={2: 0},
        compiler_params=_cparams(("arbitrary",)),
        name="moe_dispatch",
    )(dest, u, xs0)


def _group_kernel(te_ref, tu_ref, xs_ref, w1, w3, w2, ys_ref, u_ref):
    del te_ref
    j = pl.program_id(0)
    f = pl.program_id(1)
    used = tu_ref[j] == 1

    @pl.when(jnp.logical_and(jnp.logical_not(used), f == 0))
    def _():
        ys_ref[...] = jnp.zeros_like(ys_ref)

    @pl.when(used)
    def _():
        @pl.when(f == 0)
        def _():
            u_ref[...] = xs_ref[...].astype(bf16)

        u = u_ref[...]
        act = _silu(_dot(u, w1[...].astype(bf16))) * _dot(u, w3[...].astype(bf16))
        part = _dot(act.astype(bf16), w2[...].astype(bf16))

        @pl.when(f == 0)
        def _():
            ys_ref[...] = part

        @pl.when(f > 0)
        def _():
            ys_ref[...] += part


def _grouped_ffn(tile_expert, tile_used, xs, w1, w3, w2, layer):
    jl = layer // 2
    nf = FFN // TF

    def fidx(j, f, tu):
        return jnp.where(tu[j] == 1, f, nf - 1)

    return pl.pallas_call(
        _group_kernel,
        grid_spec=pltpu.PrefetchScalarGridSpec(
            num_scalar_prefetch=2,
            grid=(NT, nf),
            in_specs=[
                pl.BlockSpec((TG, D), lambda j, f, te, tu: (j, 0)),
                pl.BlockSpec((None, None, D, TF), lambda j, f, te, tu: (jl, te[j], 0, fidx(j, f, tu))),
                pl.BlockSpec((None, None, D, TF), lambda j, f, te, tu: (jl, te[j], 0, fidx(j, f, tu))),
                pl.BlockSpec((None, None, TF, D), lambda j, f, te, tu: (jl, te[j], fidx(j, f, tu), 0)),
            ],
            out_specs=pl.BlockSpec((TG, D), lambda j, f, te, tu: (j, 0)),
            scratch_shapes=[pltpu.VMEM((TG, D), bf16)],
        ),
        out_shape=jax.ShapeDtypeStruct((NT * TG, D), f32),
        compiler_params=_cparams(("arbitrary", "arbitrary")),
        name="moe_grouped_ffn",
    )(tile_expert, tile_used, xs, w1, w3, w2)


def _combine_kernel(dest_ref, h_ref, route_ref, ng_ref, g2_ref, ys_ref, o_ref, ybuf, sem):
    m = pl.program_id(0)

    def issue(r, carry):
        t = m * TM + r
        _row_copy(ys_ref, dest_ref[2 * t], ybuf.at[0], r, sem).start()
        _row_copy(ys_ref, dest_ref[2 * t + 1], ybuf.at[1], r, sem).start()
        return carry

    lax.fori_loop(0, TM, issue, 0, unroll=8)
    for k in range(2):
        pltpu.make_async_copy(ys_ref.at[pl.ds(0, TM)], ybuf.at[k], sem).wait()

    def chunk(rows, off):
        route = route_ref[rows, :]
        y = route[:, R_G0:R_G0 + 1] * ybuf[0, rows, :] + route[:, R_G1:R_G1 + 1] * ybuf[1, rows, :]
        o_ref[rows, :] = (h_ref[rows, :]
                          + _mod_rows(g2_ref, m * TM + off, ROW_CHUNK) * _rms(y, ng_ref[3:4, :]))

    _for_row_chunks(chunk)


def _combine(dest, h, route, norm_g, mod, ys, layer):
    return pl.pallas_call(
        _combine_kernel,
        grid_spec=pltpu.PrefetchScalarGridSpec(
            num_scalar_prefetch=1,
            grid=(NM,),
            in_specs=[
                pl.BlockSpec((TM, D), lambda m, d: (m, 0)),
                pl.BlockSpec((TM, ROUTER_PAD), lambda m, d: (m, 0)),
                pl.BlockSpec((None, 4, D), lambda m, d: (layer, 0, 0)),
                pl.BlockSpec((None, MOD_ROWS, D), lambda m, d: (layer, 0, 5)),
                pl.BlockSpec(memory_space=pl.ANY),
            ],
            out_specs=pl.BlockSpec((TM, D), lambda m, d: (m, 0)),
            scratch_shapes=[pltpu.VMEM((2, TM, D), f32), pltpu.SemaphoreType.DMA(())],
        ),
        out_shape=jax.ShapeDtypeStruct((T, D), f32),
        compiler_params=_cparams(("arbitrary",)),
        name="moe_combine",
    )(dest, h, route, norm_g, mod, ys)


def _moe_ffn(h, norm_g, mod, router_w_pad, w1, w3, w2, layer):
    route, counts, u = _router(h, norm_g, mod, router_w_pad, layer)
    dest, tile_expert, tile_used = _dispatch_plan(route, counts)
    xs = _dispatch(dest, u)
    ys = _grouped_ffn(tile_expert, tile_used, xs, w1, w3, w2, layer)
    return _combine(dest, h, route, norm_g, mod, ys, layer)


def _rope_tables():
    t = jnp.arange(S)
    row = (t // GRID_W).astype(f32)
    col = (t % GRID_W).astype(f32)
    inv = ROPE_BASE ** (-jnp.arange(ROPE_FREQS, dtype=f32) / ROPE_FREQS)
    a0 = row[:, None] * inv
    a1 = col[:, None] * inv
    cos_t = jnp.concatenate([jnp.cos(a0), jnp.cos(a0), jnp.cos(a1), jnp.cos(a1)], axis=1)
    sin_t = jnp.concatenate([-jnp.sin(a0), jnp.sin(a0), -jnp.sin(a1), jnp.sin(a1)], axis=1)
    cos_t = jnp.concatenate([cos_t, jnp.ones((C, HEAD_DIM), f32)], axis=0)
    sin_t = jnp.concatenate([sin_t, jnp.zeros((C, HEAD_DIM), f32)], axis=0)
    return cos_t, sin_t


def kernel(x, c, ctx, c_ctx, w_mod, b_mod, norm_g, w_in, b_gate, pool_w, pool_scale, w_pool_up,
           attn_sink, w_attn_up, conv_dw, conv_db, conv_ln_g, conv_ln_b, w_conv_up, w_out,
           ffn_w1, ffn_w3, ffn_w2, router_w, moe_w1, moe_w3, moe_w2):
    assert x.shape == (1, S, D) and ctx.shape == (1, C, D)
    cos_t, sin_t = _rope_tables()
    cc = jnp.concatenate([c, c_ctx[None, :], jnp.zeros((MOD_ROWS - 2, D), f32)], axis=0)
    mod = _modulation(cc, w_mod, b_mod)
    router_w_pad = jnp.pad(router_w, ((0, 0), (0, 0), (0, ROUTER_PAD - N_EXPERTS)))
    h = jnp.concatenate([x[0], ctx[0]], axis=0)
    for i in range(DEPTH):
        z = _in_proj(h, norm_g, mod, w_in, i)
        pool_o, conv_o = _seq_mixers(z, pool_w, pool_scale, conv_dw, conv_db, conv_ln_g, conv_ln_b, i)
        attn_o = _attention(z, attn_sink, cos_t, sin_t, i)
        h = _merge(h, z, pool_o, attn_o, conv_o, b_gate, w_pool_up, w_attn_up, w_conv_up, w_out,
                   norm_g, mod, i)
        if i % 2 == 0:
            h = _dense_ffn(h, norm_g, mod, ffn_w1, ffn_w3, ffn_w2, i)
        else:
            h = _moe_ffn(h, norm_g, mod, router_w_pad, moe_w1, moe_w3, moe_w2, i)
    return h[:S][None]
```

```python
import functools

import jax
import jax.numpy as jnp
from jax import lax
from jax.experimental import pallas as pl
from jax.experimental.pallas import tpu as pltpu

f32 = jnp.float32
bf16 = jnp.bfloat16

D = 2048
S = 8192
C = 256
T = S + C
DEPTH = 4
GRID_W = 64
HEAD_DIM = 128
POOL_WIDTH = 512
POOL_WINDOWS = (2, 4, 8, 16)
POOL_GROUP = 128
N_Q_HEADS = 8
N_KV_HEADS = 2
GQA = 4
WINDOW = 128
ATTN_SCALE = HEAD_DIM ** -0.5
ROPE_BASE = 10000.0
ROPE_FREQS = 32
CONV_WIDTH = 512
CONV_K = 31
GATE_WIDTH = 3 * D
POOL_OFF = GATE_WIDTH
Q_OFF = POOL_OFF + POOL_WIDTH
K_OFF = Q_OFF + N_Q_HEADS * HEAD_DIM
V_OFF = K_OFF + N_KV_HEADS * HEAD_DIM
CONV_OFF = V_OFF + N_KV_HEADS * HEAD_DIM
IN_WIDTH = CONV_OFF + 2 * CONV_WIDTH
FFN = 5632
N_EXPERTS = 8
N_MOD = 6
EPS = 1e-6
NEG_INF = -1e30

VMEM_LIMIT_BYTES = 56 * 1024 * 1024

TM = 768
NM = T // TM
TS = 256
NS = T // TS
NS_LAT = S // TS
HALO = 16
TN_IN = 1024
TN_MERGE = 256
TF = 256
TN_MOD = 1024
ROUTER_PAD = 128
ROW_CHUNK = 128
TG = 512
NT = (2 * T + N_EXPERTS * (TG - 1) + TG - 1) // TG
MOD_ROWS = 16


def _cparams(semantics):
    return pltpu.CompilerParams(dimension_semantics=semantics,
                                vmem_limit_bytes=VMEM_LIMIT_BYTES)


def _sigmoid(x):
    return 1.0 / (1.0 + jnp.exp(-x))


def _silu(x):
    return x * _sigmoid(x)


def _mod_rows(mod_ref, row0, nrows):
    rows = row0 + lax.broadcasted_iota(jnp.int32, (nrows, 1), 0)
    return jnp.where(rows >= S, mod_ref[1:2, :], mod_ref[0:1, :])


def _rms(x, g):
    return x * lax.rsqrt(jnp.mean(x * x, axis=-1, keepdims=True) + EPS) * g


def _for_row_chunks(fn):
    def step(r, carry):
        off = pl.multiple_of(r * ROW_CHUNK, ROW_CHUNK)
        fn(pl.ds(off, ROW_CHUNK), off)
        return carry
    lax.fori_loop(0, TM // ROW_CHUNK, step, 0)


def _norm_mod_tile(h_ref, g, sh_ref, sc_ref, m, store):
    def chunk(rows, off):
        row0 = m * TM + off
        u = (_rms(h_ref[rows, :], g) * (1.0 + _mod_rows(sc_ref, row0, ROW_CHUNK))
             + _mod_rows(sh_ref, row0, ROW_CHUNK))
        store(rows, u)
    _for_row_chunks(chunk)


def _residual_tile(h_ref, y_ref, g, gate_ref, m):
    def chunk(rows, off):
        y_ref[rows, :] = h_ref[rows, :] + _mod_rows(gate_ref, m * TM + off, ROW_CHUNK) * _rms(y_ref[rows, :], g)
    _for_row_chunks(chunk)


def _dot(a, b):
    return jnp.dot(a, b, preferred_element_type=f32)


def _mod_kernel(cc_ref, w_ref, b_ref, o_ref):
    s = _silu(cc_ref[...])
    o_ref[...] = _dot(s.astype(bf16), w_ref[...].astype(bf16)) + b_ref[...]


def _modulation(cc, w_mod, b_mod):
    nn = (N_MOD * D) // TN_MOD
    return pl.pallas_call(
        _mod_kernel,
        grid=(DEPTH, nn),
        in_specs=[
            pl.BlockSpec((MOD_ROWS, D), lambda l, n: (0, 0)),
            pl.BlockSpec((None, D, TN_MOD), lambda l, n: (l, 0, n)),
            pl.BlockSpec((None, 1, TN_MOD), lambda l, n: (l, 0, n)),
        ],
        out_specs=pl.BlockSpec((None, MOD_ROWS, TN_MOD), lambda l, n: (l, 0, n)),
        out_shape=jax.ShapeDtypeStruct((DEPTH, MOD_ROWS, N_MOD * D), f32),
        compiler_params=_cparams(("arbitrary", "arbitrary")),
        name="modulation",
    )(cc, w_mod, b_mod.reshape(DEPTH, 1, N_MOD * D))


def _win_kernel(h_ref, ng_ref, sh_ref, sc_ref, w_ref, z_ref, u_ref):
    m = pl.program_id(0)

    @pl.when(pl.program_id(1) == 0)
    def _():
        def store(rows, u):
            u_ref[rows, :] = u.astype(bf16)
        _norm_mod_tile(h_ref, ng_ref[0:1, :], sh_ref, sc_ref, m, store)

    z_ref[...] = _dot(u_ref[...], w_ref[...].astype(bf16)).astype(bf16)


def _in_proj(h, norm_g, mod, w_in, layer):
    return pl.pallas_call(
        _win_kernel,
        grid=(NM, IN_WIDTH // TN_IN),
        in_specs=[
            pl.BlockSpec((TM, D), lambda m, n: (m, 0)),
            pl.BlockSpec((None, 4, D), lambda m, n: (layer, 0, 0)),
            pl.BlockSpec((None, MOD_ROWS, D), lambda m, n: (layer, 0, 0)),
            pl.BlockSpec((None, MOD_ROWS, D), lambda m, n: (layer, 0, 1)),
            pl.BlockSpec((None, D, TN_IN), lambda m, n: (layer, 0, n)),
        ],
        out_specs=pl.BlockSpec((TM, TN_IN), lambda m, n: (m, n)),
        out_shape=jax.ShapeDtypeStruct((T, IN_WIDTH), bf16),
        scratch_shapes=[pltpu.VMEM((TM, D), bf16)],
        compiler_params=_cparams(("arbitrary", "arbitrary")),
        name="in_proj",
    )(h, norm_g, mod, mod, w_in)


def _seqmix_kernel(zp_c, zp_p, zp_n, zc_c, zc_p, zc_n, pw_ref, ps_ref, dw_ref, db_ref,
                   lng_ref, lnb_ref, po_ref, co_ref, xe, ge):
    i = pl.program_id(0)
    has_prev = jnp.where(jnp.logical_and(i != 0, i != NS_LAT), 1.0, 0.0).astype(f32)
    has_next = jnp.where(i < NS_LAT - 1, 1.0, 0.0).astype(f32)
    is_lat = i < NS_LAT
    pos = jnp.where(is_lat, i * TS, 0) + lax.broadcasted_iota(jnp.int32, (TS, 1), 0)
    seq_len = jnp.where(is_lat, S, C)

    xe[0:HALO, :] = zp_p[...].astype(f32) * has_prev
    xe[HALO:HALO + TS, :] = zp_c[...].astype(f32)
    xe[HALO + TS:, :] = zp_n[...].astype(f32) * has_next
    outs = []
    for g, w in enumerate(POOL_WINDOWS):
        cols = slice(g * POOL_GROUP, (g + 1) * POOL_GROUP)
        acc = xe[pl.ds(HALO - w // 2, TS), cols]
        for d in range(-w // 2 + 1, w // 2):
            acc = acc + xe[pl.ds(HALO + d, TS), cols]
        lo = jnp.clip(pos - w // 2, 0, seq_len)
        hi = jnp.clip(pos - w // 2 + w, 0, seq_len)
        diff = acc / (hi - lo).astype(f32) - xe[HALO:HALO + TS, cols]
        outs.append(_dot(diff.astype(bf16), pw_ref[g].astype(bf16)))
    po_ref[...] = (jnp.concatenate(outs, axis=1) * ps_ref[...]).astype(bf16)

    def glu(ref):
        zz = ref[...].astype(f32)
        return zz[:, :CONV_WIDTH] * _sigmoid(zz[:, CONV_WIDTH:])

    ge[0:HALO, :] = glu(zc_p) * has_prev
    ge[HALO:HALO + TS, :] = glu(zc_c)
    ge[HALO + TS:, :] = glu(zc_n) * has_next
    base = HALO - CONV_K // 2
    acc = ge[pl.ds(base, TS), :] * dw_ref[0:1, :]
    for k in range(1, CONV_K):
        acc = acc + ge[pl.ds(base + k, TS), :] * dw_ref[k:k + 1, :]
    y = acc + db_ref[...]
    yc = y - jnp.mean(y, axis=-1, keepdims=True)
    yn = yc * lax.rsqrt(jnp.mean(yc * yc, axis=-1, keepdims=True) + EPS) * lng_ref[...] + lnb_ref[...]
    co_ref[...] = _silu(yn).astype(bf16)


def _seq_mixers(z, pool_w, pool_scale, conv_dw, conv_db, conv_ln_g, conv_ln_b, layer):
    rb = TS // HALO
    last = T // HALO - 1
    pcol = POOL_OFF // POOL_WIDTH
    ccol = CONV_OFF // (2 * CONV_WIDTH)

    def prev(i):
        return jnp.maximum(i * rb - 1, 0)

    def nxt(i):
        return jnp.minimum((i + 1) * rb, last)

    vec = lambda width: pl.BlockSpec((None, 1, width), lambda i: (layer, 0, 0))
    return pl.pallas_call(
        _seqmix_kernel,
        grid=(NS,),
        in_specs=[
            pl.BlockSpec((TS, POOL_WIDTH), lambda i: (i, pcol)),
            pl.BlockSpec((HALO, POOL_WIDTH), lambda i: (prev(i), pcol)),
            pl.BlockSpec((HALO, POOL_WIDTH), lambda i: (nxt(i), pcol)),
            pl.BlockSpec((TS, 2 * CONV_WIDTH), lambda i: (i, ccol)),
            pl.BlockSpec((HALO, 2 * CONV_WIDTH), lambda i: (prev(i), ccol)),
            pl.BlockSpec((HALO, 2 * CONV_WIDTH), lambda i: (nxt(i), ccol)),
            pl.BlockSpec((None, 4, POOL_GROUP, POOL_GROUP), lambda i: (layer, 0, 0, 0)),
            vec(POOL_WIDTH),
            pl.BlockSpec((None, CONV_K, CONV_WIDTH), lambda i: (layer, 0, 0)),
            vec(CONV_WIDTH), vec(CONV_WIDTH), vec(CONV_WIDTH),
        ],
        out_specs=[
            pl.BlockSpec((TS, POOL_WIDTH), lambda i: (i, 0)),
            pl.BlockSpec((TS, CONV_WIDTH), lambda i: (i, 0)),
        ],
        out_shape=[
            jax.ShapeDtypeStruct((T, POOL_WIDTH), bf16),
            jax.ShapeDtypeStruct((T, CONV_WIDTH), bf16),
        ],
        scratch_shapes=[
            pltpu.VMEM((TS + 2 * HALO, POOL_WIDTH), f32),
            pltpu.VMEM((TS + 2 * HALO, CONV_WIDTH), f32),
        ],
        compiler_params=_cparams(("arbitrary",)),
        name="seq_mixers",
    )(z, z, z, z, z, z, pool_w, pool_scale.reshape(DEPTH, 1, POOL_WIDTH), conv_dw,
      conv_db.reshape(DEPTH, 1, CONV_WIDTH), conv_ln_g.reshape(DEPTH, 1, CONV_WIDTH),
      conv_ln_b.reshape(DEPTH, 1, CONV_WIDTH))


def _rope(x, cos_t, sin_t):
    lane = lax.broadcasted_iota(jnp.int32, x.shape, 1) & (2 * ROPE_FREQS - 1)
    partner = jnp.where(lane < ROPE_FREQS, pltpu.roll(x, HEAD_DIM - ROPE_FREQS, 1),
                        pltpu.roll(x, ROPE_FREQS, 1))
    return x * cos_t + partner * sin_t


def _attn_kernel(sink_ref, q_ref, kc_ref, kp_ref, kn_ref, vc_ref, vp_ref, vn_ref, kx_ref, vx_ref,
                 cq_ref, sq_ref, cp_ref, sp_ref, cn_ref, sn_ref, o_ref, *, layer):
    hk = pl.program_id(0)
    i = pl.program_id(1)
    cq = cq_ref[...]
    sq = sq_ref[...]
    qs = []
    for j in range(GQA):
        qj = q_ref[:, j * HEAD_DIM:(j + 1) * HEAD_DIM].astype(f32)
        qs.append(_rope(qj, cq, sq).astype(bf16))
    q = jnp.concatenate(qs, axis=0)
    k = jnp.concatenate([
        _rope(kp_ref[...].astype(f32), cp_ref[...], sp_ref[...]).astype(bf16),
        _rope(kc_ref[...].astype(f32), cq, sq).astype(bf16),
        _rope(kn_ref[...].astype(f32), cn_ref[...], sn_ref[...]).astype(bf16)], axis=0)
    v = jnp.concatenate([vp_ref[...], vc_ref[...], vn_ref[...]], axis=0)
    kw = k.shape[0]

    nt = (((1,), (1,)), ((), ()))
    s_loc = lax.dot_general(q, k, nt, preferred_element_type=f32) * ATTN_SCALE
    s_ctx = lax.dot_general(q, kx_ref[...], nt, preferred_element_type=f32) * ATTN_SCALE

    qrow = lax.broadcasted_iota(jnp.int32, (GQA * TS, 1), 0) & (TS - 1)
    krel = lax.broadcasted_iota(jnp.int32, (1, kw), 1) - WINDOW
    kpos = i * TS + krel
    k_ok = jnp.logical_and(jnp.logical_and(kpos >= 0, kpos < S), i < NS_LAT)
    valid = jnp.logical_and(jnp.abs(krel - qrow) <= WINDOW, k_ok)
    s_loc = jnp.where(valid, s_loc, NEG_INF)

    sink = jnp.concatenate(
        [jnp.full((TS, 1), sink_ref[layer, hk * GQA + j], f32) for j in range(GQA)], axis=0)
    mx = jnp.maximum(jnp.maximum(jnp.max(s_loc, axis=-1, keepdims=True),
                                 jnp.max(s_ctx, axis=-1, keepdims=True)), sink)
    p_loc = jnp.exp(s_loc - mx)
    p_ctx = jnp.exp(s_ctx - mx)
    denom = (jnp.sum(p_loc, axis=-1, keepdims=True) + jnp.sum(p_ctx, axis=-1, keepdims=True)
             + jnp.exp(sink - mx))
    o = (_dot(p_loc.astype(bf16), v) + _dot(p_ctx.astype(bf16), vx_ref[...])) / denom
    for j in range(GQA):
        o_ref[:, j * HEAD_DIM:(j + 1) * HEAD_DIM] = o[j * TS:(j + 1) * TS, :].astype(bf16)


def _attention(z, attn_sink, cos_t, sin_t, layer):
    qcol = Q_OFF // (GQA * HEAD_DIM)
    kcol = K_OFF // HEAD_DIM
    vcol = V_OFF // HEAD_DIM
    hb = TS // WINDOW
    last = T // WINDOW - 1
    ctx_blk = S // TS

    def prev(i):
        return jnp.maximum(i * hb - 1, 0)

    def nxt(i):
        return jnp.minimum((i + 1) * hb, last)

    def kv_specs(col):
        return [
            pl.BlockSpec((TS, HEAD_DIM), lambda h, i: (i, col + h)),
            pl.BlockSpec((WINDOW, HEAD_DIM), lambda h, i: (prev(i), col + h)),
            pl.BlockSpec((WINDOW, HEAD_DIM), lambda h, i: (nxt(i), col + h)),
        ]

    tab_specs = [
        pl.BlockSpec((TS, HEAD_DIM), lambda h, i: (i, 0)),
        pl.BlockSpec((TS, HEAD_DIM), lambda h, i: (i, 0)),
        pl.BlockSpec((WINDOW, HEAD_DIM), lambda h, i: (prev(i), 0)),
        pl.BlockSpec((WINDOW, HEAD_DIM), lambda h, i: (prev(i), 0)),
        pl.BlockSpec((WINDOW, HEAD_DIM), lambda h, i: (nxt(i), 0)),
        pl.BlockSpec((WINDOW, HEAD_DIM), lambda h, i: (nxt(i), 0)),
    ]
    return pl.pallas_call(
        functools.partial(_attn_kernel, layer=layer),
        grid=(N_KV_HEADS, NS),
        in_specs=[pl.BlockSpec(memory_space=pltpu.SMEM),
                  pl.BlockSpec((TS, GQA * HEAD_DIM), lambda h, i: (i, qcol + h))]
                 + kv_specs(kcol) + kv_specs(vcol)
                 + [pl.BlockSpec((TS, HEAD_DIM), lambda h, i: (ctx_blk, kcol + h)),
                    pl.BlockSpec((TS, HEAD_DIM), lambda h, i: (ctx_blk, vcol + h))]
                 + tab_specs,
        out_specs=pl.BlockSpec((TS, GQA * HEAD_DIM), lambda h, i: (i, h)),
        out_shape=jax.ShapeDtypeStruct((T, N_Q_HEADS * HEAD_DIM), bf16),
        compiler_params=_cparams(("arbitrary", "arbitrary")),
        name="attention",
    )(attn_sink, z, z, z, z, z, z, z, z, z, cos_t, sin_t, cos_t, sin_t, cos_t, sin_t)


NB_MERGE = D // TN_MERGE


def _merge_kernel(h_ref, p_ref, a_ref, c_ref, zg0, zg1, zg2, bg0, bg1, bg2, wp, wa, wc, wo,
                  ng_ref, g1_ref, o_ref, act_ref):
    m = pl.program_id(0)
    n = pl.program_id(1)

    def gate(zg, bg):
        return _sigmoid(zg[...].astype(f32) + bg[...])

    def up():
        merged = (gate(zg0, bg0) * _dot(p_ref[...], wp[...].astype(bf16))
                  + gate(zg1, bg1) * _dot(a_ref[...], wa[...].astype(bf16))
                  + gate(zg2, bg2) * _dot(c_ref[...], wc[...].astype(bf16)))
        return merged.astype(bf16)

    def down():
        return _dot(act_ref[...], wo[...].astype(bf16))

    _skewed_pipeline(n, NB_MERGE, up, down, act_ref, o_ref)

    @pl.when(n == NB_MERGE)
    def _():
        _residual_tile(h_ref, o_ref, ng_ref[1:2, :], g1_ref, m)


def _merge(h, z, pool_o, attn_o, conv_o, b_gate, w_pool_up, w_attn_up, w_conv_up, w_out,
           norm_g, mod, layer):
    nb = NB_MERGE
    bg = b_gate.reshape(DEPTH, 1, GATE_WIDTH)
    upc = lambda n: _up_chunk(n, nb)
    zgate = lambda b: pl.BlockSpec((TM, TN_MERGE), lambda m, n: (m, b * nb + upc(n)))
    bgate = lambda b: pl.BlockSpec((None, 1, TN_MERGE), lambda m, n: (layer, 0, b * nb + upc(n)))
    wup = lambda k: pl.BlockSpec((None, k, TN_MERGE), lambda m, n: (layer, 0, upc(n)))
    return pl.pallas_call(
        _merge_kernel,
        grid=(NM, nb + 1),
        in_specs=[
            pl.BlockSpec((TM, D), lambda m, n: (m, 0), pipeline_mode=pl.Buffered(1)),
            pl.BlockSpec((TM, POOL_WIDTH), lambda m, n: (m, 0)),
            pl.BlockSpec((TM, N_Q_HEADS * HEAD_DIM), lambda m, n: (m, 0)),
            pl.BlockSpec((TM, CONV_WIDTH), lambda m, n: (m, 0)),
            zgate(0), zgate(1), zgate(2), bgate(0), bgate(1), bgate(2),
            wup(POOL_WIDTH), wup(N_Q_HEADS * HEAD_DIM), wup(CONV_WIDTH),
            pl.BlockSpec((None, TN_MERGE, D), lambda m, n: (layer, _down_chunk(n, nb), 0)),
            pl.BlockSpec((None, 4, D), lambda m, n: (layer, 0, 0)),
            pl.BlockSpec((None, MOD_ROWS, D), lambda m, n: (layer, 0, 2)),
        ],
        out_specs=pl.BlockSpec((TM, D), lambda m, n: (m, 0)),
        out_shape=jax.ShapeDtypeStruct((T, D), f32),
        scratch_shapes=[pltpu.VMEM((TM, TN_MERGE), bf16)],
        compiler_params=_cparams(("arbitrary", "arbitrary")),
        name="merge",
    )(h, pool_o, attn_o, conv_o, z, z, z, bg, bg, bg, w_pool_up, w_attn_up, w_conv_up, w_out,
      norm_g, mod)


def _ffn_prologue(h_ref, ng_ref, sh_ref, sc_ref, u_ref, m):
    def store(rows, u):
        u_ref[rows, :] = u.astype(bf16)
    _norm_mod_tile(h_ref, ng_ref[2:3, :], sh_ref, sc_ref, m, store)


def _ffn_epilogue(h_ref, ng_ref, g2_ref, o_ref, m):
    _residual_tile(h_ref, o_ref, ng_ref[3:4, :], g2_ref, m)


NF = FFN // TF


def _skewed_pipeline(step, n_chunks, up, down, act_ref, o_ref):
    @pl.when(step == 0)
    def _():
        act_ref[...] = up()

    @pl.when(step == 1)
    def _():
        part = down()
        act = up()
        o_ref[...] = part
        act_ref[...] = act

    @pl.when(jnp.logical_and(step > 1, step < n_chunks))
    def _():
        part = down()
        act = up()
        o_ref[...] += part
        act_ref[...] = act

    @pl.when(step == n_chunks)
    def _():
        o_ref[...] += down()


def _up_chunk(step, n_chunks):
    return jnp.minimum(step, n_chunks - 1)


def _down_chunk(step, n_chunks):
    del n_chunks
    return jnp.maximum(step - 1, 0)


def _swiglu_pipeline(f, u_ref, w1, w3, w2, act_ref, o_ref):
    def up():
        u = u_ref[...]
        act = _silu(_dot(u, w1[...].astype(bf16))) * _dot(u, w3[...].astype(bf16))
        return act.astype(bf16)

    def down():
        return _dot(act_ref[...], w2[...].astype(bf16))

    _skewed_pipeline(f, NF, up, down, act_ref, o_ref)


def _ffn_kernel(h_ref, ng_ref, sh_ref, sc_ref, g2_ref, w1, w3, w2, o_ref, u_ref, act_ref):
    m = pl.program_id(0)
    f = pl.program_id(1)

    @pl.when(f == 0)
    def _():
        _ffn_prologue(h_ref, ng_ref, sh_ref, sc_ref, u_ref, m)

    _swiglu_pipeline(f, u_ref, w1, w3, w2, act_ref, o_ref)

    @pl.when(f == NF)
    def _():
        _ffn_epilogue(h_ref, ng_ref, g2_ref, o_ref, m)


def _mod_specs(layer, chunks):
    specs = [pl.BlockSpec((TM, D), lambda m, *r: (m, 0), pipeline_mode=pl.Buffered(1)),
             pl.BlockSpec((None, 4, D), lambda m, *r: (layer, 0, 0))]
    for ch in chunks:
        specs.append(pl.BlockSpec((None, MOD_ROWS, D), lambda m, *r, ch=ch: (layer, 0, ch)))
    return specs


def _dense_ffn(h, norm_g, mod, w1, w3, w2, layer):
    j = layer // 2
    return pl.pallas_call(
        _ffn_kernel,
        grid=(NM, NF + 1),
        in_specs=_mod_specs(layer, (3, 4, 5)) + [
            pl.BlockSpec((None, D, TF), lambda m, f: (j, 0, _up_chunk(f, NF))),
            pl.BlockSpec((None, D, TF), lambda m, f: (j, 0, _up_chunk(f, NF))),
            pl.BlockSpec((None, TF, D), lambda m, f: (j, _down_chunk(f, NF), 0)),
        ],
        out_specs=pl.BlockSpec((TM, D), lambda m, f: (m, 0)),
        out_shape=jax.ShapeDtypeStruct((T, D), f32),
        scratch_shapes=[pltpu.VMEM((TM, D), bf16), pltpu.VMEM((TM, TF), bf16)],
        compiler_params=_cparams(("arbitrary", "arbitrary")),
        name="dense_ffn",
    )(h, norm_g, mod, mod, mod, w1, w3, w2)


def _split_bf16(x):
    hi = x.astype(bf16)
    lo = (x - hi.astype(f32)).astype(bf16)
    return hi, lo


R_E0, R_E1, R_G0, R_G1, R_R0, R_R1 = range(6)


def _router_kernel(h_ref, ng_ref, sh_ref, sc_ref, rw_ref, route_ref, cnt_ref, u_ref, base_ref):
    m = pl.program_id(0)

    @pl.when(m == 0)
    def _():
        base_ref[...] = jnp.zeros_like(base_ref)

    wh, wl = _split_bf16(rw_ref[...])
    ri = lax.broadcasted_iota(jnp.int32, (ROW_CHUNK, ROW_CHUNK), 0)
    ci = lax.broadcasted_iota(jnp.int32, (ROW_CHUNK, ROW_CHUNK), 1)
    earlier = jnp.where(ri > ci, 1.0, 0.0).astype(bf16)

    def store(rows, u):
        u_ref[rows, :] = u
        uh, ul = _split_bf16(u)
        logits = _dot(uh, wh) + (_dot(uh, wl) + _dot(ul, wh))
        lane = lax.broadcasted_iota(jnp.int32, logits.shape, 1)
        lg = jnp.where(lane < N_EXPERTS, logits, -jnp.inf)
        m1 = jnp.max(lg, axis=-1, keepdims=True)
        i1 = jnp.min(jnp.where(lg == m1, lane, ROUTER_PAD), axis=-1, keepdims=True)
        lg2 = jnp.where(lane == i1, -jnp.inf, lg)
        m2 = jnp.max(lg2, axis=-1, keepdims=True)
        i2 = jnp.min(jnp.where(lg2 == m2, lane, ROUTER_PAD), axis=-1, keepdims=True)
        e2 = jnp.exp(m2 - m1)
        den = 1.0 + e2
        hit1 = lane == i1
        hit2 = lane == i2
        onehot = jnp.where(jnp.logical_or(hit1, hit2), 1.0, 0.0)
        before = _dot(earlier, onehot.astype(bf16)) + base_ref[...]
        r1 = jnp.sum(jnp.where(hit1, before, 0.0), axis=-1, keepdims=True)
        r2 = jnp.sum(jnp.where(hit2, before, 0.0), axis=-1, keepdims=True)
        base_ref[...] = base_ref[...] + jnp.sum(onehot, axis=0, keepdims=True)
        vals = (i1.astype(f32), i2.astype(f32), 1.0 / den, e2 / den, r1, r2)
        route = jnp.zeros(logits.shape, f32)
        for col, val in enumerate(vals):
            route = jnp.where(lane == col, val, route)
        route_ref[rows, :] = route

    _norm_mod_tile(h_ref, ng_ref[2:3, :], sh_ref, sc_ref, m, store)
    cnt_ref[...] = jnp.broadcast_to(base_ref[...], cnt_ref.shape)


def _router(h, norm_g, mod, router_w_pad, layer):
    j = layer // 2
    return pl.pallas_call(
        _router_kernel,
        grid=(NM,),
        in_specs=_mod_specs(layer, (3, 4)) + [
            pl.BlockSpec((None, D, ROUTER_PAD), lambda m: (j, 0, 0)),
        ],
        out_specs=[
            pl.BlockSpec((TM, ROUTER_PAD), lambda m: (m, 0)),
            pl.BlockSpec((8, ROUTER_PAD), lambda m: (0, 0)),
            pl.BlockSpec((TM, D), lambda m: (m, 0)),
        ],
        out_shape=[
            jax.ShapeDtypeStruct((T, ROUTER_PAD), f32),
            jax.ShapeDtypeStruct((8, ROUTER_PAD), f32),
            jax.ShapeDtypeStruct((T, D), f32),
        ],
        scratch_shapes=[pltpu.VMEM((1, ROUTER_PAD), f32)],
        compiler_params=_cparams(("arbitrary",)),
        name="router",
    )(h, norm_g, mod, mod, router_w_pad)


def _dispatch_plan(route, counts):
    expert = route[:, R_E0:R_E1 + 1].astype(jnp.int32)
    rank = route[:, R_R0:R_R1 + 1].astype(jnp.int32)
    cnt = counts[0, :N_EXPERTS].astype(jnp.int32)
    tiles = (cnt + TG - 1) // TG
    tile_end = jnp.cumsum(tiles)
    start = (tile_end - tiles) * TG
    dest = (start[expert] + rank).reshape(2 * T)
    n_used = tile_end[-1]
    tile_id = jnp.minimum(jnp.arange(NT, dtype=jnp.int32), n_used - 1)
    tile_expert = jnp.sum(tile_id[:, None] >= tile_end[None, :], axis=1).astype(jnp.int32)
    tile_used = (jnp.arange(NT, dtype=jnp.int32) < n_used).astype(jnp.int32)
    return dest, tile_expert, tile_used


def _row_copy(src_ref, src_row, dst_ref, dst_row, sem):
    return pltpu.make_async_copy(src_ref.at[pl.ds(src_row, 1)], dst_ref.at[pl.ds(dst_row, 1)], sem)


def _dispatch_kernel(dest_ref, u_ref, xs_in_ref, xs_ref, sem):
    del xs_in_ref
    m = pl.program_id(0)

    def issue(r, carry):
        t = m * TM + r
        _row_copy(u_ref, r, xs_ref, dest_ref[2 * t], sem).start()
        _row_copy(u_ref, r, xs_ref, dest_ref[2 * t + 1], sem).start()
        return carry

    lax.fori_loop(0, TM, issue, 0, unroll=8)
    for _ in range(2):
        pltpu.make_async_copy(u_ref, xs_ref.at[pl.ds(0, TM)], sem).wait()


def _dispatch(dest, u):
    xs0 = jnp.zeros((NT * TG, D), f32)
    return pl.pallas_call(
        _dispatch_kernel,
        grid_spec=pltpu.PrefetchScalarGridSpec(
            num_scalar_prefetch=1,
            grid=(NM,),
            in_specs=[pl.BlockSpec((TM, D), lambda m, d: (m, 0)), pl.BlockSpec(memory_space=pl.ANY)],
            out_specs=pl.BlockSpec(memory_space=pl.ANY),
            scratch_shapes=[pltpu.SemaphoreType.DMA(())],
        ),
        out_shape=jax.ShapeDtypeStruct((NT * TG, D), f32),
        input_output_aliases={2: 0},
        compiler_params=_cparams(("arbitrary",)),
        name="moe_dispatch",
    )(dest, u, xs0)


def _group_kernel(te_ref, tu_ref, xs_ref, w1, w3, w2, ys_ref, u_ref, act_ref):
    del te_ref
    j = pl.program_id(0)
    f = pl.program_id(1)
    used = tu_ref[j] == 1

    @pl.when(jnp.logical_and(jnp.logical_not(used), f == 0))
    def _():
        ys_ref[...] = jnp.zeros_like(ys_ref)

    @pl.when(used)
    def _():
        @pl.when(f == 0)
        def _():
            u_ref[...] = xs_ref[...].astype(bf16)

        _swiglu_pipeline(f, u_ref, w1, w3, w2, act_ref, ys_ref)


def _grouped_ffn(tile_expert, tile_used, xs, w1, w3, w2, layer):
    jl = layer // 2

    def chunk(j, f, tu, which):
        return jnp.where(tu[j] == 1, which(f, NF), NF - 1)

    return pl.pallas_call(
        _group_kernel,
        grid_spec=pltpu.PrefetchScalarGridSpec(
            num_scalar_prefetch=2,
            grid=(NT, NF + 1),
            in_specs=[
                pl.BlockSpec((TG, D), lambda j, f, te, tu: (j, 0)),
                pl.BlockSpec((None, None, D, TF),
                             lambda j, f, te, tu: (jl, te[j], 0, chunk(j, f, tu, _up_chunk))),
                pl.BlockSpec((None, None, D, TF),
                             lambda j, f, te, tu: (jl, te[j], 0, chunk(j, f, tu, _up_chunk))),
                pl.BlockSpec((None, None, TF, D),
                             lambda j, f, te, tu: (jl, te[j], chunk(j, f, tu, _down_chunk), 0)),
            ],
            out_specs=pl.BlockSpec((TG, D), lambda j, f, te, tu: (j, 0)),
            scratch_shapes=[pltpu.VMEM((TG, D), bf16), pltpu.VMEM((TG, TF), bf16)],
        ),
        out_shape=jax.ShapeDtypeStruct((NT * TG, D), f32),
        compiler_params=_cparams(("arbitrary", "arbitrary")),
        name="moe_grouped_ffn",
    )(tile_expert, tile_used, xs, w1, w3, w2)


def _combine_kernel(dest_ref, h_ref, route_ref, ng_ref, g2_ref, ys_ref, o_ref, ybuf, sem):
    m = pl.program_id(0)

    def issue(r, carry):
        t = m * TM + r
        _row_copy(ys_ref, dest_ref[2 * t], ybuf.at[0], r, sem).start()
        _row_copy(ys_ref, dest_ref[2 * t + 1], ybuf.at[1], r, sem).start()
        return carry

    lax.fori_loop(0, TM, issue, 0, unroll=8)
    for k in range(2):
        pltpu.make_async_copy(ys_ref.at[pl.ds(0, TM)], ybuf.at[k], sem).wait()

    def chunk(rows, off):
        route = route_ref[rows, :]
        y = route[:, R_G0:R_G0 + 1] * ybuf[0, rows, :] + route[:, R_G1:R_G1 + 1] * ybuf[1, rows, :]
        o_ref[rows, :] = (h_ref[rows, :]
                          + _mod_rows(g2_ref, m * TM + off, ROW_CHUNK) * _rms(y, ng_ref[3:4, :]))

    _for_row_chunks(chunk)


def _combine(dest, h, route, norm_g, mod, ys, layer):
    return pl.pallas_call(
        _combine_kernel,
        grid_spec=pltpu.PrefetchScalarGridSpec(
            num_scalar_prefetch=1,
            grid=(NM,),
            in_specs=[
                pl.BlockSpec((TM, D), lambda m, d: (m, 0)),
                pl.BlockSpec((TM, ROUTER_PAD), lambda m, d: (m, 0)),
                pl.BlockSpec((None, 4, D), lambda m, d: (layer, 0, 0)),
                pl.BlockSpec((None, MOD_ROWS, D), lambda m, d: (layer, 0, 5)),
                pl.BlockSpec(memory_space=pl.ANY),
            ],
            out_specs=pl.BlockSpec((TM, D), lambda m, d: (m, 0)),
            scratch_shapes=[pltpu.VMEM((2, TM, D), f32), pltpu.SemaphoreType.DMA(())],
        ),
        out_shape=jax.ShapeDtypeStruct((T, D), f32),
        compiler_params=_cparams(("arbitrary",)),
        name="moe_combine",
    )(dest, h, route, norm_g, mod, ys)


def _moe_ffn(h, norm_g, mod, router_w_pad, w1, w3, w2, layer):
    route, counts, u = _router(h, norm_g, mod, router_w_pad, layer)
    dest, tile_expert, tile_used = _dispatch_plan(route, counts)
    xs = _dispatch(dest, u)
    ys = _grouped_ffn(tile_expert, tile_used, xs, w1, w3, w2, layer)
    return _combine(dest, h, route, norm_g, mod, ys, layer)


def _rope_tables():
    t = jnp.arange(S)
    row = (t // GRID_W).astype(f32)
    col = (t % GRID_W).astype(f32)
    inv = ROPE_BASE ** (-jnp.arange(ROPE_FREQS, dtype=f32) / ROPE_FREQS)
    a0 = row[:, None] * inv
    a1 = col[:, None] * inv
    cos_t = jnp.concatenate([jnp.cos(a0), jnp.cos(a0), jnp.cos(a1), jnp.cos(a1)], axis=1)
    sin_t = jnp.concatenate([-jnp.sin(a0), jnp.sin(a0), -jnp.sin(a1), jnp.sin(a1)], axis=1)
    cos_t = jnp.concatenate([cos_t, jnp.ones((C, HEAD_DIM), f32)], axis=0)
    sin_t = jnp.concatenate([sin_t, jnp.zeros((C, HEAD_DIM), f32)], axis=0)
    return cos_t, sin_t


def kernel(x, c, ctx, c_ctx, w_mod, b_mod, norm_g, w_in, b_gate, pool_w, pool_scale, w_pool_up,
           attn_sink, w_attn_up, conv_dw, conv_db, conv_ln_g, conv_ln_b, w_conv_up, w_out,
           ffn_w1, ffn_w3, ffn_w2, router_w, moe_w1, moe_w3, moe_w2):
    assert x.shape == (1, S, D) and ctx.shape == (1, C, D)
    cos_t, sin_t = _rope_tables()
    cc = jnp.concatenate([c, c_ctx[None, :], jnp.zeros((MOD_ROWS - 2, D), f32)], axis=0)
    mod = _modulation(cc, w_mod, b_mod)
    router_w_pad = jnp.pad(router_w, ((0, 0), (0, 0), (0, ROUTER_PAD - N_EXPERTS)))
    h = jnp.concatenate([x[0], ctx[0]], axis=0)
    for i in range(DEPTH):
        z = _in_proj(h, norm_g, mod, w_in, i)
        pool_o, conv_o = _seq_mixers(z, pool_w, pool_scale, conv_dw, conv_db, conv_ln_g, conv_ln_b, i)
        attn_o = _attention(z, attn_sink, cos_t, sin_t, i)
        h = _merge(h, z, pool_o, attn_o, conv_o, b_gate, w_pool_up, w_attn_up, w_conv_up, w_out,
                   norm_g, mod, i)
        if i % 2 == 0:
            h = _dense_ffn(h, norm_g, mod, ffn_w1, ffn_w3, ffn_w2, i)
        else:
            h = _moe_ffn(h, norm_g, mod, router_w_pad, moe_w1, moe_w3, moe_w2, i)
    return h[:S][None]
```

```python
import functools

import jax
import jax.numpy as jnp
from jax import lax
from jax.experimental import pallas as pl
from jax.experimental.pallas import tpu as pltpu

f32 = jnp.float32
bf16 = jnp.bfloat16

D = 2048
S = 8192
C = 256
T = S + C
DEPTH = 4
GRID_W = 64
HEAD_DIM = 128
POOL_WIDTH = 512
POOL_WINDOWS = (2, 4, 8, 16)
POOL_GROUP = 128
N_Q_HEADS = 8
N_KV_HEADS = 2
GQA = 4
WINDOW = 128
ATTN_SCALE = HEAD_DIM ** -0.5
ROPE_BASE = 10000.0
ROPE_FREQS = 32
CONV_WIDTH = 512
CONV_K = 31
GATE_WIDTH = 3 * D
POOL_OFF = GATE_WIDTH
Q_OFF = POOL_OFF + POOL_WIDTH
K_OFF = Q_OFF + N_Q_HEADS * HEAD_DIM
V_OFF = K_OFF + N_KV_HEADS * HEAD_DIM
CONV_OFF = V_OFF + N_KV_HEADS * HEAD_DIM
IN_WIDTH = CONV_OFF + 2 * CONV_WIDTH
FFN = 5632
N_EXPERTS = 8
N_MOD = 6
EPS = 1e-6
NEG_INF = -1e30

VMEM_LIMIT_BYTES = 56 * 1024 * 1024

TM = 1056
NM = T // TM
TS = 256
NS = T // TS
NS_LAT = S // TS
HALO = 16
TN_IN = 1024
TN_MERGE = 256
TF = 256
TN_MOD = 1024
ROUTER_PAD = 128
ROW_CHUNK = 96
TG = 1024
NT = (2 * T + N_EXPERTS * (TG - 1) + TG - 1) // TG
MOD_ROWS = 16


def _cparams(semantics):
    return pltpu.CompilerParams(dimension_semantics=semantics,
                                vmem_limit_bytes=VMEM_LIMIT_BYTES)


def _sigmoid(x):
    return 1.0 / (1.0 + jnp.exp(-x))


def _silu(x):
    return x * _sigmoid(x)


def _mod_rows(mod_ref, row0, nrows):
    rows = row0 + lax.broadcasted_iota(jnp.int32, (nrows, 1), 0)
    return jnp.where(rows >= S, mod_ref[1:2, :], mod_ref[0:1, :])


def _rms(x, g):
    return x * lax.rsqrt(jnp.mean(x * x, axis=-1, keepdims=True) + EPS) * g


def _for_row_chunks(fn):
    def step(r, carry):
        off = pl.multiple_of(r * ROW_CHUNK, ROW_CHUNK)
        fn(pl.ds(off, ROW_CHUNK), off)
        return carry
    lax.fori_loop(0, TM // ROW_CHUNK, step, 0)


def _norm_mod_tile(h_ref, g, sh_ref, sc_ref, m, store):
    def chunk(rows, off):
        row0 = m * TM + off
        u = (_rms(h_ref[rows, :], g) * (1.0 + _mod_rows(sc_ref, row0, ROW_CHUNK))
             + _mod_rows(sh_ref, row0, ROW_CHUNK))
        store(rows, u)
    _for_row_chunks(chunk)


def _residual_tile(h_ref, y_ref, g, gate_ref, m):
    def chunk(rows, off):
        y_ref[rows, :] = h_ref[rows, :] + _mod_rows(gate_ref, m * TM + off, ROW_CHUNK) * _rms(y_ref[rows, :], g)
    _for_row_chunks(chunk)


def _dot(a, b):
    return jnp.dot(a, b, preferred_element_type=f32)


def _mod_kernel(cc_ref, w_ref, b_ref, o_ref):
    s = _silu(cc_ref[...])
    o_ref[...] = _dot(s.astype(bf16), w_ref[...].astype(bf16)) + b_ref[...]


def _modulation(cc, w_mod, b_mod):
    nn = (N_MOD * D) // TN_MOD
    return pl.pallas_call(
        _mod_kernel,
        grid=(DEPTH, nn),
        in_specs=[
            pl.BlockSpec((MOD_ROWS, D), lambda l, n: (0, 0)),
            pl.BlockSpec((None, D, TN_MOD), lambda l, n: (l, 0, n)),
            pl.BlockSpec((None, 1, TN_MOD), lambda l, n: (l, 0, n)),
        ],
        out_specs=pl.BlockSpec((None, MOD_ROWS, TN_MOD), lambda l, n: (l, 0, n)),
        out_shape=jax.ShapeDtypeStruct((DEPTH, MOD_ROWS, N_MOD * D), f32),
        compiler_params=_cparams(("arbitrary", "arbitrary")),
        name="modulation",
    )(cc, w_mod, b_mod.reshape(DEPTH, 1, N_MOD * D))


def _win_kernel(h_ref, ng_ref, sh_ref, sc_ref, w_ref, z_ref, u_ref):
    m = pl.program_id(0)

    @pl.when(pl.program_id(1) == 0)
    def _():
        def store(rows, u):
            u_ref[rows, :] = u.astype(bf16)
        _norm_mod_tile(h_ref, ng_ref[0:1, :], sh_ref, sc_ref, m, store)

    z_ref[...] = _dot(u_ref[...], w_ref[...].astype(bf16)).astype(bf16)


def _in_proj(h, norm_g, mod, w_in, layer):
    return pl.pallas_call(
        _win_kernel,
        grid=(NM, IN_WIDTH // TN_IN),
        in_specs=[
            pl.BlockSpec((TM, D), lambda m, n: (m, 0)),
            pl.BlockSpec((None, 4, D), lambda m, n: (layer, 0, 0)),
            pl.BlockSpec((None, MOD_ROWS, D), lambda m, n: (layer, 0, 0)),
            pl.BlockSpec((None, MOD_ROWS, D), lambda m, n: (layer, 0, 1)),
            pl.BlockSpec((None, D, TN_IN), lambda m, n: (layer, 0, n)),
        ],
        out_specs=pl.BlockSpec((TM, TN_IN), lambda m, n: (m, n)),
        out_shape=jax.ShapeDtypeStruct((T, IN_WIDTH), bf16),
        scratch_shapes=[pltpu.VMEM((TM, D), bf16)],
        compiler_params=_cparams(("arbitrary", "arbitrary")),
        name="in_proj",
    )(h, norm_g, mod, mod, w_in)


def _seqmix_kernel(zp_c, zp_p, zp_n, zc_c, zc_p, zc_n, pw_ref, ps_ref, dw_ref, db_ref,
                   lng_ref, lnb_ref, po_ref, co_ref, xe, ge):
    i = pl.program_id(0)
    has_prev = jnp.where(jnp.logical_and(i != 0, i != NS_LAT), 1.0, 0.0).astype(f32)
    has_next = jnp.where(i < NS_LAT - 1, 1.0, 0.0).astype(f32)
    is_lat = i < NS_LAT
    pos = jnp.where(is_lat, i * TS, 0) + lax.broadcasted_iota(jnp.int32, (TS, 1), 0)
    seq_len = jnp.where(is_lat, S, C)

    xe[0:HALO, :] = zp_p[...].astype(f32) * has_prev
    xe[HALO:HALO + TS, :] = zp_c[...].astype(f32)
    xe[HALO + TS:, :] = zp_n[...].astype(f32) * has_next
    outs = []
    for g, w in enumerate(POOL_WINDOWS):
        cols = slice(g * POOL_GROUP, (g + 1) * POOL_GROUP)
        acc = xe[pl.ds(HALO - w // 2, TS), cols]
        for d in range(-w // 2 + 1, w // 2):
            acc = acc + xe[pl.ds(HALO + d, TS), cols]
        lo = jnp.clip(pos - w // 2, 0, seq_len)
        hi = jnp.clip(pos - w // 2 + w, 0, seq_len)
        diff = acc / (hi - lo).astype(f32) - xe[HALO:HALO + TS, cols]
        outs.append(_dot(diff.astype(bf16), pw_ref[g].astype(bf16)))
    po_ref[...] = (jnp.concatenate(outs, axis=1) * ps_ref[...]).astype(bf16)

    def glu(ref):
        zz = ref[...].astype(f32)
        return zz[:, :CONV_WIDTH] * _sigmoid(zz[:, CONV_WIDTH:])

    ge[0:HALO, :] = glu(zc_p) * has_prev
    ge[HALO:HALO + TS, :] = glu(zc_c)
    ge[HALO + TS:, :] = glu(zc_n) * has_next
    base = HALO - CONV_K // 2
    acc = ge[pl.ds(base, TS), :] * dw_ref[0:1, :]
    for k in range(1, CONV_K):
        acc = acc + ge[pl.ds(base + k, TS), :] * dw_ref[k:k + 1, :]
    y = acc + db_ref[...]
    yc = y - jnp.mean(y, axis=-1, keepdims=True)
    yn = yc * lax.rsqrt(jnp.mean(yc * yc, axis=-1, keepdims=True) + EPS) * lng_ref[...] + lnb_ref[...]
    co_ref[...] = _silu(yn).astype(bf16)


def _seq_mixers(z, pool_w, pool_scale, conv_dw, conv_db, conv_ln_g, conv_ln_b, layer):
    rb = TS // HALO
    last = T // HALO - 1
    pcol = POOL_OFF // POOL_WIDTH
    ccol = CONV_OFF // (2 * CONV_WIDTH)

    def prev(i):
        return jnp.maximum(i * rb - 1, 0)

    def nxt(i):
        return jnp.minimum((i + 1) * rb, last)

    vec = lambda width: pl.BlockSpec((None, 1, width), lambda i: (layer, 0, 0))
    return pl.pallas_call(
        _seqmix_kernel,
        grid=(NS,),
        in_specs=[
            pl.BlockSpec((TS, POOL_WIDTH), lambda i: (i, pcol)),
            pl.BlockSpec((HALO, POOL_WIDTH), lambda i: (prev(i), pcol)),
            pl.BlockSpec((HALO, POOL_WIDTH), lambda i: (nxt(i), pcol)),
            pl.BlockSpec((TS, 2 * CONV_WIDTH), lambda i: (i, ccol)),
            pl.BlockSpec((HALO, 2 * CONV_WIDTH), lambda i: (prev(i), ccol)),
            pl.BlockSpec((HALO, 2 * CONV_WIDTH), lambda i: (nxt(i), ccol)),
            pl.BlockSpec((None, 4, POOL_GROUP, POOL_GROUP), lambda i: (layer, 0, 0, 0)),
            vec(POOL_WIDTH),
            pl.BlockSpec((None, CONV_K, CONV_WIDTH), lambda i: (layer, 0, 0)),
            vec(CONV_WIDTH), vec(CONV_WIDTH), vec(CONV_WIDTH),
        ],
        out_specs=[
            pl.BlockSpec((TS, POOL_WIDTH), lambda i: (i, 0)),
            pl.BlockSpec((TS, CONV_WIDTH), lambda i: (i, 0)),
        ],
        out_shape=[
            jax.ShapeDtypeStruct((T, POOL_WIDTH), bf16),
            jax.ShapeDtypeStruct((T, CONV_WIDTH), bf16),
        ],
        scratch_shapes=[
            pltpu.VMEM((TS + 2 * HALO, POOL_WIDTH), f32),
            pltpu.VMEM((TS + 2 * HALO, CONV_WIDTH), f32),
        ],
        compiler_params=_cparams(("arbitrary",)),
        name="seq_mixers",
    )(z, z, z, z, z, z, pool_w, pool_scale.reshape(DEPTH, 1, POOL_WIDTH), conv_dw,
      conv_db.reshape(DEPTH, 1, CONV_WIDTH), conv_ln_g.reshape(DEPTH, 1, CONV_WIDTH),
      conv_ln_b.reshape(DEPTH, 1, CONV_WIDTH))


def _rope(x, cos_t, sin_t):
    lane = lax.broadcasted_iota(jnp.int32, x.shape, 1) & (2 * ROPE_FREQS - 1)
    partner = jnp.where(lane < ROPE_FREQS, pltpu.roll(x, HEAD_DIM - ROPE_FREQS, 1),
                        pltpu.roll(x, ROPE_FREQS, 1))
    return x * cos_t + partner * sin_t


def _attn_kernel(sink_ref, q_ref, kc_ref, kp_ref, kn_ref, vc_ref, vp_ref, vn_ref, kx_ref, vx_ref,
                 cq_ref, sq_ref, cp_ref, sp_ref, cn_ref, sn_ref, o_ref, *, layer):
    hk = pl.program_id(0)
    i = pl.program_id(1)
    cq = cq_ref[...]
    sq = sq_ref[...]
    qs = []
    for j in range(GQA):
        qj = q_ref[:, j * HEAD_DIM:(j + 1) * HEAD_DIM].astype(f32)
        qs.append(_rope(qj, cq, sq).astype(bf16))
    q = jnp.concatenate(qs, axis=0)
    k = jnp.concatenate([
        _rope(kp_ref[...].astype(f32), cp_ref[...], sp_ref[...]).astype(bf16),
        _rope(kc_ref[...].astype(f32), cq, sq).astype(bf16),
        _rope(kn_ref[...].astype(f32), cn_ref[...], sn_ref[...]).astype(bf16)], axis=0)
    v = jnp.concatenate([vp_ref[...], vc_ref[...], vn_ref[...]], axis=0)
    kw = k.shape[0]

    nt = (((1,), (1,)), ((), ()))
    s_loc = lax.dot_general(q, k, nt, preferred_element_type=f32) * ATTN_SCALE
    s_ctx = lax.dot_general(q, kx_ref[...], nt, preferred_element_type=f32) * ATTN_SCALE

    qrow = lax.broadcasted_iota(jnp.int32, (GQA * TS, 1), 0) & (TS - 1)
    krel = lax.broadcasted_iota(jnp.int32, (1, kw), 1) - WINDOW
    kpos = i * TS + krel
    k_ok = jnp.logical_and(jnp.logical_and(kpos >= 0, kpos < S), i < NS_LAT)
    valid = jnp.logical_and(jnp.abs(krel - qrow) <= WINDOW, k_ok)
    s_loc = jnp.where(valid, s_loc, NEG_INF)

    sink = jnp.concatenate(
        [jnp.full((TS, 1), sink_ref[layer, hk * GQA + j], f32) for j in range(GQA)], axis=0)
    mx = jnp.maximum(jnp.maximum(jnp.max(s_loc, axis=-1, keepdims=True),
                                 jnp.max(s_ctx, axis=-1, keepdims=True)), sink)
    p_loc = jnp.exp(s_loc - mx)
    p_ctx = jnp.exp(s_ctx - mx)
    denom = (jnp.sum(p_loc, axis=-1, keepdims=True) + jnp.sum(p_ctx, axis=-1, keepdims=True)
             + jnp.exp(sink - mx))
    o = (_dot(p_loc.astype(bf16), v) + _dot(p_ctx.astype(bf16), vx_ref[...])) / denom
    for j in range(GQA):
        o_ref[:, j * HEAD_DIM:(j + 1) * HEAD_DIM] = o[j * TS:(j + 1) * TS, :].astype(bf16)


def _attention(z, attn_sink, cos_t, sin_t, layer):
    qcol = Q_OFF // (GQA * HEAD_DIM)
    kcol = K_OFF // HEAD_DIM
    vcol = V_OFF // HEAD_DIM
    hb = TS // WINDOW
    last = T // WINDOW - 1
    ctx_blk = S // TS

    def prev(i):
        return jnp.maximum(i * hb - 1, 0)

    def nxt(i):
        return jnp.minimum((i + 1) * hb, last)

    def kv_specs(col):
        return [
            pl.BlockSpec((TS, HEAD_DIM), lambda h, i: (i, col + h)),
            pl.BlockSpec((WINDOW, HEAD_DIM), lambda h, i: (prev(i), col + h)),
            pl.BlockSpec((WINDOW, HEAD_DIM), lambda h, i: (nxt(i), col + h)),
        ]

    tab_specs = [
        pl.BlockSpec((TS, HEAD_DIM), lambda h, i: (i, 0)),
        pl.BlockSpec((TS, HEAD_DIM), lambda h, i: (i, 0)),
        pl.BlockSpec((WINDOW, HEAD_DIM), lambda h, i: (prev(i), 0)),
        pl.BlockSpec((WINDOW, HEAD_DIM), lambda h, i: (prev(i), 0)),
        pl.BlockSpec((WINDOW, HEAD_DIM), lambda h, i: (nxt(i), 0)),
        pl.BlockSpec((WINDOW, HEAD_DIM), lambda h, i: (nxt(i), 0)),
    ]
    return pl.pallas_call(
        functools.partial(_attn_kernel, layer=layer),
        grid=(N_KV_HEADS, NS),
        in_specs=[pl.BlockSpec(memory_space=pltpu.SMEM),
                  pl.BlockSpec((TS, GQA * HEAD_DIM), lambda h, i: (i, qcol + h))]
                 + kv_specs(kcol) + kv_specs(vcol)
                 + [pl.BlockSpec((TS, HEAD_DIM), lambda h, i: (ctx_blk, kcol + h)),
                    pl.BlockSpec((TS, HEAD_DIM), lambda h, i: (ctx_blk, vcol + h))]
                 + tab_specs,
        out_specs=pl.BlockSpec((TS, GQA * HEAD_DIM), lambda h, i: (i, h)),
        out_shape=jax.ShapeDtypeStruct((T, N_Q_HEADS * HEAD_DIM), bf16),
        compiler_params=_cparams(("arbitrary", "arbitrary")),
        name="attention",
    )(attn_sink, z, z, z, z, z, z, z, z, z, cos_t, sin_t, cos_t, sin_t, cos_t, sin_t)


NB_MERGE = D // TN_MERGE


def _merge_kernel(h_ref, p_ref, a_ref, c_ref, zg0, zg1, zg2, bg0, bg1, bg2, wp, wa, wc, wo,
                  ng_ref, g1_ref, o_ref, act_ref):
    m = pl.program_id(0)
    n = pl.program_id(1)

    def gate(zg, bg):
        return _sigmoid(zg[...].astype(f32) + bg[...])

    def up():
        merged = (gate(zg0, bg0) * _dot(p_ref[...], wp[...].astype(bf16))
                  + gate(zg1, bg1) * _dot(a_ref[...], wa[...].astype(bf16))
                  + gate(zg2, bg2) * _dot(c_ref[...], wc[...].astype(bf16)))
        return merged.astype(bf16)

    def down():
        return _dot(act_ref[...], wo[...].astype(bf16))

    _skewed_pipeline(n, NB_MERGE, up, down, act_ref, o_ref)

    @pl.when(n == NB_MERGE)
    def _():
        _residual_tile(h_ref, o_ref, ng_ref[1:2, :], g1_ref, m)


def _merge(h, z, pool_o, attn_o, conv_o, b_gate, w_pool_up, w_attn_up, w_conv_up, w_out,
           norm_g, mod, layer):
    nb = NB_MERGE
    bg = b_gate.reshape(DEPTH, 1, GATE_WIDTH)
    upc = lambda n: _up_chunk(n, nb)
    zgate = lambda b: pl.BlockSpec((TM, TN_MERGE), lambda m, n: (m, b * nb + upc(n)))
    bgate = lambda b: pl.BlockSpec((None, 1, TN_MERGE), lambda m, n: (layer, 0, b * nb + upc(n)))
    wup = lambda k: pl.BlockSpec((None, k, TN_MERGE), lambda m, n: (layer, 0, upc(n)))
    return pl.pallas_call(
        _merge_kernel,
        grid=(NM, nb + 1),
        in_specs=[
            pl.BlockSpec((TM, D), lambda m, n: (m, 0), pipeline_mode=pl.Buffered(1)),
            pl.BlockSpec((TM, POOL_WIDTH), lambda m, n: (m, 0)),
            pl.BlockSpec((TM, N_Q_HEADS * HEAD_DIM), lambda m, n: (m, 0)),
            pl.BlockSpec((TM, CONV_WIDTH), lambda m, n: (m, 0)),
            zgate(0), zgate(1), zgate(2), bgate(0), bgate(1), bgate(2),
            wup(POOL_WIDTH), wup(N_Q_HEADS * HEAD_DIM), wup(CONV_WIDTH),
            pl.BlockSpec((None, TN_MERGE, D), lambda m, n: (layer, _down_chunk(n, nb), 0)),
            pl.BlockSpec((None, 4, D), lambda m, n: (layer, 0, 0)),
            pl.BlockSpec((None, MOD_ROWS, D), lambda m, n: (layer, 0, 2)),
        ],
        out_specs=pl.BlockSpec((TM, D), lambda m, n: (m, 0)),
        out_shape=jax.ShapeDtypeStruct((T, D), f32),
        scratch_shapes=[pltpu.VMEM((TM, TN_MERGE), bf16)],
        compiler_params=_cparams(("arbitrary", "arbitrary")),
        name="merge",
    )(h, pool_o, attn_o, conv_o, z, z, z, bg, bg, bg, w_pool_up, w_attn_up, w_conv_up, w_out,
      norm_g, mod)


def _ffn_prologue(h_ref, ng_ref, sh_ref, sc_ref, u_ref, m):
    def store(rows, u):
        u_ref[rows, :] = u.astype(bf16)
    _norm_mod_tile(h_ref, ng_ref[2:3, :], sh_ref, sc_ref, m, store)


def _ffn_epilogue(h_ref, ng_ref, g2_ref, o_ref, m):
    _residual_tile(h_ref, o_ref, ng_ref[3:4, :], g2_ref, m)


NF = FFN // TF


def _skewed_pipeline(step, n_chunks, up, down, act_ref, o_ref):
    @pl.when(step == 0)
    def _():
        act_ref[...] = up()

    @pl.when(step == 1)
    def _():
        part = down()
        act = up()
        o_ref[...] = part
        act_ref[...] = act

    @pl.when(jnp.logical_and(step > 1, step < n_chunks))
    def _():
        part = down()
        act = up()
        o_ref[...] += part
        act_ref[...] = act

    @pl.when(step == n_chunks)
    def _():
        o_ref[...] += down()


def _up_chunk(step, n_chunks):
    return jnp.minimum(step, n_chunks - 1)


def _down_chunk(step, n_chunks):
    del n_chunks
    return jnp.maximum(step - 1, 0)


def _swiglu_pipeline(f, u_ref, w1, w3, w2, act_ref, o_ref):
    def up():
        u = u_ref[...]
        act = _silu(_dot(u, w1[...].astype(bf16))) * _dot(u, w3[...].astype(bf16))
        return act.astype(bf16)

    def down():
        return _dot(act_ref[...], w2[...].astype(bf16))

    _skewed_pipeline(f, NF, up, down, act_ref, o_ref)


def _ffn_kernel(h_ref, ng_ref, sh_ref, sc_ref, g2_ref, w1, w3, w2, o_ref, u_ref, act_ref):
    m = pl.program_id(0)
    f = pl.program_id(1)

    @pl.when(f == 0)
    def _():
        _ffn_prologue(h_ref, ng_ref, sh_ref, sc_ref, u_ref, m)

    _swiglu_pipeline(f, u_ref, w1, w3, w2, act_ref, o_ref)

    @pl.when(f == NF)
    def _():
        _ffn_epilogue(h_ref, ng_ref, g2_ref, o_ref, m)


def _mod_specs(layer, chunks):
    specs = [pl.BlockSpec((TM, D), lambda m, *r: (m, 0), pipeline_mode=pl.Buffered(1)),
             pl.BlockSpec((None, 4, D), lambda m, *r: (layer, 0, 0))]
    for ch in chunks:
        specs.append(pl.BlockSpec((None, MOD_ROWS, D), lambda m, *r, ch=ch: (layer, 0, ch)))
    return specs


def _dense_ffn(h, norm_g, mod, w1, w3, w2, layer):
    j = layer // 2
    return pl.pallas_call(
        _ffn_kernel,
        grid=(NM, NF + 1),
        in_specs=_mod_specs(layer, (3, 4, 5)) + [
            pl.BlockSpec((None, D, TF), lambda m, f: (j, 0, _up_chunk(f, NF))),
            pl.BlockSpec((None, D, TF), lambda m, f: (j, 0, _up_chunk(f, NF))),
            pl.BlockSpec((None, TF, D), lambda m, f: (j, _down_chunk(f, NF), 0)),
        ],
        out_specs=pl.BlockSpec((TM, D), lambda m, f: (m, 0)),
        out_shape=jax.ShapeDtypeStruct((T, D), f32),
        scratch_shapes=[pltpu.VMEM((TM, D), bf16), pltpu.VMEM((TM, TF), bf16)],
        compiler_params=_cparams(("arbitrary", "arbitrary")),
        name="dense_ffn",
    )(h, norm_g, mod, mod, mod, w1, w3, w2)


def _split_bf16(x):
    hi = x.astype(bf16)
    lo = (x - hi.astype(f32)).astype(bf16)
    return hi, lo


R_E0, R_E1, R_G0, R_G1, R_R0, R_R1 = range(6)


def _router_kernel(h_ref, ng_ref, sh_ref, sc_ref, rw_ref, route_ref, cnt_ref, u_ref, base_ref):
    m = pl.program_id(0)

    @pl.when(m == 0)
    def _():
        base_ref[...] = jnp.zeros_like(base_ref)

    wh, wl = _split_bf16(rw_ref[...])
    ri = lax.broadcasted_iota(jnp.int32, (ROW_CHUNK, ROW_CHUNK), 0)
    ci = lax.broadcasted_iota(jnp.int32, (ROW_CHUNK, ROW_CHUNK), 1)
    earlier = jnp.where(ri > ci, 1.0, 0.0).astype(bf16)

    def store(rows, u):
        u_ref[rows, :] = u
        uh, ul = _split_bf16(u)
        logits = _dot(uh, wh) + (_dot(uh, wl) + _dot(ul, wh))
        lane = lax.broadcasted_iota(jnp.int32, logits.shape, 1)
        lg = jnp.where(lane < N_EXPERTS, logits, -jnp.inf)
        m1 = jnp.max(lg, axis=-1, keepdims=True)
        i1 = jnp.min(jnp.where(lg == m1, lane, ROUTER_PAD), axis=-1, keepdims=True)
        lg2 = jnp.where(lane == i1, -jnp.inf, lg)
        m2 = jnp.max(lg2, axis=-1, keepdims=True)
        i2 = jnp.min(jnp.where(lg2 == m2, lane, ROUTER_PAD), axis=-1, keepdims=True)
        e2 = jnp.exp(m2 - m1)
        den = 1.0 + e2
        hit1 = lane == i1
        hit2 = lane == i2
        onehot = jnp.where(jnp.logical_or(hit1, hit2), 1.0, 0.0)
        before = _dot(earlier, onehot.astype(bf16)) + base_ref[...]
        r1 = jnp.sum(jnp.where(hit1, before, 0.0), axis=-1, keepdims=True)
        r2 = jnp.sum(jnp.where(hit2, before, 0.0), axis=-1, keepdims=True)
        base_ref[...] = base_ref[...] + jnp.sum(onehot, axis=0, keepdims=True)
        vals = (i1.astype(f32), i2.astype(f32), 1.0 / den, e2 / den, r1, r2)
        route = jnp.zeros(logits.shape, f32)
        for col, val in enumerate(vals):
            route = jnp.where(lane == col, val, route)
        route_ref[rows, :] = route

    _norm_mod_tile(h_ref, ng_ref[2:3, :], sh_ref, sc_ref, m, store)
    cnt_ref[...] = jnp.broadcast_to(base_ref[...], cnt_ref.shape)


def _router(h, norm_g, mod, router_w_pad, layer):
    j = layer // 2
    return pl.pallas_call(
        _router_kernel,
        grid=(NM,),
        in_specs=_mod_specs(layer, (3, 4)) + [
            pl.BlockSpec((None, D, ROUTER_PAD), lambda m: (j, 0, 0)),
        ],
        out_specs=[
            pl.BlockSpec((TM, ROUTER_PAD), lambda m: (m, 0)),
            pl.BlockSpec((8, ROUTER_PAD), lambda m: (0, 0)),
            pl.BlockSpec((TM, D), lambda m: (m, 0)),
        ],
        out_shape=[
            jax.ShapeDtypeStruct((T, ROUTER_PAD), f32),
            jax.ShapeDtypeStruct((8, ROUTER_PAD), f32),
            jax.ShapeDtypeStruct((T, D), f32),
        ],
        scratch_shapes=[pltpu.VMEM((1, ROUTER_PAD), f32)],
        compiler_params=_cparams(("arbitrary",)),
        name="router",
    )(h, norm_g, mod, mod, router_w_pad)


def _dispatch_plan(route, counts):
    expert = route[:, R_E0:R_E1 + 1].astype(jnp.int32)
    rank = route[:, R_R0:R_R1 + 1].astype(jnp.int32)
    cnt = counts[0, :N_EXPERTS].astype(jnp.int32)
    tiles = (cnt + TG - 1) // TG
    tile_end = jnp.cumsum(tiles)
    start = (tile_end - tiles) * TG
    dest = (start[expert] + rank).reshape(2 * T)
    n_used = tile_end[-1]
    tile_id = jnp.minimum(jnp.arange(NT, dtype=jnp.int32), n_used - 1)
    tile_expert = jnp.sum(tile_id[:, None] >= tile_end[None, :], axis=1).astype(jnp.int32)
    rows_used = cnt[tile_expert] - (tile_id - (tile_end - tiles)[tile_expert]) * TG
    tile_mode = jnp.where(jnp.arange(NT, dtype=jnp.int32) >= n_used, TILE_UNUSED,
                          jnp.where(rows_used <= TG // 2, TILE_HALF, TILE_FULL)).astype(jnp.int32)
    return dest, tile_expert, tile_mode


def _row_copy(src_ref, src_row, dst_ref, dst_row, sem):
    return pltpu.make_async_copy(src_ref.at[pl.ds(src_row, 1)], dst_ref.at[pl.ds(dst_row, 1)], sem)


def _dispatch_kernel(dest_ref, u_ref, xs_in_ref, xs_ref, sem):
    del xs_in_ref
    m = pl.program_id(0)

    def issue(r, carry):
        t = m * TM + r
        _row_copy(u_ref, r, xs_ref, dest_ref[2 * t], sem).start()
        _row_copy(u_ref, r, xs_ref, dest_ref[2 * t + 1], sem).start()
        return carry

    lax.fori_loop(0, TM, issue, 0, unroll=8)
    for _ in range(2):
        pltpu.make_async_copy(u_ref, xs_ref.at[pl.ds(0, TM)], sem).wait()


def _dispatch(dest, u):
    xs0 = jnp.zeros((NT * TG, D), f32)
    return pl.pallas_call(
        _dispatch_kernel,
        grid_spec=pltpu.PrefetchScalarGridSpec(
            num_scalar_prefetch=1,
            grid=(NM,),
            in_specs=[pl.BlockSpec((TM, D), lambda m, d: (m, 0)), pl.BlockSpec(memory_space=pl.ANY)],
            out_specs=pl.BlockSpec(memory_space=pl.ANY),
            scratch_shapes=[pltpu.SemaphoreType.DMA(())],
        ),
        out_shape=jax.ShapeDtypeStruct((NT * TG, D), f32),
        input_output_aliases={2: 0},
        compiler_params=_cparams(("arbitrary",)),
        name="moe_dispatch",
    )(dest, u, xs0)


TILE_UNUSED, TILE_HALF, TILE_FULL = 0, 1, 2


def _group_kernel(te_ref, tm_ref, xs_ref, w1, w3, w2, ys_ref, u_ref, act_ref):
    del te_ref
    j = pl.program_id(0)
    f = pl.program_id(1)
    mode = tm_ref[j]

    @pl.when(jnp.logical_and(mode != TILE_FULL, f == 0))
    def _():
        ys_ref[...] = jnp.zeros_like(ys_ref)

    def run(rows):
        sl = pl.ds(0, rows)

        @pl.when(f == 0)
        def _():
            u_ref[sl, :] = xs_ref[sl, :].astype(bf16)

        _swiglu_pipeline(f, u_ref.at[sl], w1, w3, w2, act_ref.at[sl], ys_ref.at[sl])

    @pl.when(mode == TILE_FULL)
    def _():
        run(TG)

    @pl.when(mode == TILE_HALF)
    def _():
        run(TG // 2)


def _grouped_ffn(tile_expert, tile_mode, xs, w1, w3, w2, layer):
    jl = layer // 2

    def chunk(j, f, tmode, which):
        return jnp.where(tmode[j] != TILE_UNUSED, which(f, NF), NF - 1)

    return pl.pallas_call(
        _group_kernel,
        grid_spec=pltpu.PrefetchScalarGridSpec(
            num_scalar_prefetch=2,
            grid=(NT, NF + 1),
            in_specs=[
                pl.BlockSpec((TG, D), lambda j, f, te, tu: (j, 0)),
                pl.BlockSpec((None, None, D, TF),
                             lambda j, f, te, tu: (jl, te[j], 0, chunk(j, f, tu, _up_chunk))),
                pl.BlockSpec((None, None, D, TF),
                             lambda j, f, te, tu: (jl, te[j], 0, chunk(j, f, tu, _up_chunk))),
                pl.BlockSpec((None, None, TF, D),
                             lambda j, f, te, tu: (jl, te[j], chunk(j, f, tu, _down_chunk), 0)),
            ],
            out_specs=pl.BlockSpec((TG, D), lambda j, f, te, tu: (j, 0)),
            scratch_shapes=[pltpu.VMEM((TG, D), bf16), pltpu.VMEM((TG, TF), bf16)],
        ),
        out_shape=jax.ShapeDtypeStruct((NT * TG, D), f32),
        compiler_params=_cparams(("arbitrary", "arbitrary")),
        name="moe_grouped_ffn",
    )(tile_expert, tile_mode, xs, w1, w3, w2)


def _combine_kernel(dest_ref, h_ref, route_ref, ng_ref, g2_ref, ys_ref, o_ref, ybuf, sem):
    m = pl.program_id(0)

    def issue(r, carry):
        t = m * TM + r
        _row_copy(ys_ref, dest_ref[2 * t], ybuf.at[0], r, sem).start()
        _row_copy(ys_ref, dest_ref[2 * t + 1], ybuf.at[1], r, sem).start()
        return carry

    lax.fori_loop(0, TM, issue, 0, unroll=8)
    for k in range(2):
        pltpu.make_async_copy(ys_ref.at[pl.ds(0, TM)], ybuf.at[k], sem).wait()

    def chunk(rows, off):
        route = route_ref[rows, :]
        y = route[:, R_G0:R_G0 + 1] * ybuf[0, rows, :] + route[:, R_G1:R_G1 + 1] * ybuf[1, rows, :]
        o_ref[rows, :] = (h_ref[rows, :]
                          + _mod_rows(g2_ref, m * TM + off, ROW_CHUNK) * _rms(y, ng_ref[3:4, :]))

    _for_row_chunks(chunk)


def _combine(dest, h, route, norm_g, mod, ys, layer):
    return pl.pallas_call(
        _combine_kernel,
        grid_spec=pltpu.PrefetchScalarGridSpec(
            num_scalar_prefetch=1,
            grid=(NM,),
            in_specs=[
                pl.BlockSpec((TM, D), lambda m, d: (m, 0), pipeline_mode=pl.Buffered(1)),
                pl.BlockSpec((TM, ROUTER_PAD), lambda m, d: (m, 0)),
                pl.BlockSpec((None, 4, D), lambda m, d: (layer, 0, 0)),
                pl.BlockSpec((None, MOD_ROWS, D), lambda m, d: (layer, 0, 5)),
                pl.BlockSpec(memory_space=pl.ANY),
            ],
            out_specs=pl.BlockSpec((TM, D), lambda m, d: (m, 0)),
            scratch_shapes=[pltpu.VMEM((2, TM, D), f32), pltpu.SemaphoreType.DMA(())],
        ),
        out_shape=jax.ShapeDtypeStruct((T, D), f32),
        compiler_params=_cparams(("arbitrary",)),
        name="moe_combine",
    )(dest, h, route, norm_g, mod, ys)


def _moe_ffn(h, norm_g, mod, router_w_pad, w1, w3, w2, layer):
    route, counts, u = _router(h, norm_g, mod, router_w_pad, layer)
    dest, tile_expert, tile_mode = _dispatch_plan(route, counts)
    xs = _dispatch(dest, u)
    ys = _grouped_ffn(tile_expert, tile_mode, xs, w1, w3, w2, layer)
    return _combine(dest, h, route, norm_g, mod, ys, layer)


def _rope_tables():
    t = jnp.arange(S)
    row = (t // GRID_W).astype(f32)
    col = (t % GRID_W).astype(f32)
    inv = ROPE_BASE ** (-jnp.arange(ROPE_FREQS, dtype=f32) / ROPE_FREQS)
    a0 = row[:, None] * inv
    a1 = col[:, None] * inv
    cos_t = jnp.concatenate([jnp.cos(a0), jnp.cos(a0), jnp.cos(a1), jnp.cos(a1)], axis=1)
    sin_t = jnp.concatenate([-jnp.sin(a0), jnp.sin(a0), -jnp.sin(a1), jnp.sin(a1)], axis=1)
    cos_t = jnp.concatenate([cos_t, jnp.ones((C, HEAD_DIM), f32)], axis=0)
    sin_t = jnp.concatenate([sin_t, jnp.zeros((C, HEAD_DIM), f32)], axis=0)
    return cos_t, sin_t


def kernel(x, c, ctx, c_ctx, w_mod, b_mod, norm_g, w_in, b_gate, pool_w, pool_scale, w_pool_up,
           attn_sink, w_attn_up, conv_dw, conv_db, conv_ln_g, conv_ln_b, w_conv_up, w_out,
           ffn_w1, ffn_w3, ffn_w2, router_w, moe_w1, moe_w3, moe_w2):
    assert x.shape == (1, S, D) and ctx.shape == (1, C, D)
    cos_t, sin_t = _rope_tables()
    cc = jnp.concatenate([c, c_ctx[None, :], jnp.zeros((MOD_ROWS - 2, D), f32)], axis=0)
    mod = _modulation(cc, w_mod, b_mod)
    router_w_pad = jnp.pad(router_w, ((0, 0), (0, 0), (0, ROUTER_PAD - N_EXPERTS)))
    h = jnp.concatenate([x[0], ctx[0]], axis=0)
    for i in range(DEPTH):
        z = _in_proj(h, norm_g, mod, w_in, i)
        pool_o, conv_o = _seq_mixers(z, pool_w, pool_scale, conv_dw, conv_db, conv_ln_g, conv_ln_b, i)
        attn_o = _attention(z, attn_sink, cos_t, sin_t, i)
        h = _merge(h, z, pool_o, attn_o, conv_o, b_gate, w_pool_up, w_attn_up, w_conv_up, w_out,
                   norm_g, mod, i)
        if i % 2 == 0:
            h = _dense_ffn(h, norm_g, mod, ffn_w1, ffn_w3, ffn_w2, i)
        else:
            h = _moe_ffn(h, norm_g, mod, router_w_pad, moe_w1, moe_w3, moe_w2, i)
    return h[:S][None]
```

```python
import functools

import jax
import jax.numpy as jnp
from jax import lax
from jax.experimental import pallas as pl
from jax.experimental.pallas import tpu as pltpu

f32 = jnp.float32
bf16 = jnp.bfloat16

D = 2048
S = 8192
C = 256
T = S + C
DEPTH = 4
GRID_W = 64
HEAD_DIM = 128
POOL_WIDTH = 512
POOL_WINDOWS = (2, 4, 8, 16)
POOL_GROUP = 128
N_Q_HEADS = 8
N_KV_HEADS = 2
GQA = 4
WINDOW = 128
ATTN_SCALE = HEAD_DIM ** -0.5
ROPE_BASE = 10000.0
ROPE_FREQS = 32
CONV_WIDTH = 512
CONV_K = 31
GATE_WIDTH = 3 * D
POOL_OFF = GATE_WIDTH
Q_OFF = POOL_OFF + POOL_WIDTH
K_OFF = Q_OFF + N_Q_HEADS * HEAD_DIM
V_OFF = K_OFF + N_KV_HEADS * HEAD_DIM
CONV_OFF = V_OFF + N_KV_HEADS * HEAD_DIM
IN_WIDTH = CONV_OFF + 2 * CONV_WIDTH
FFN = 5632
N_EXPERTS = 8
N_MOD = 6
EPS = 1e-6
NEG_INF = -1e30

VMEM_LIMIT_BYTES = 56 * 1024 * 1024

TM = 1056
NM = T // TM
TS = 256
NS = T // TS
NS_LAT = S // TS
HALO = 16
SUBLANES = 8
TN_IN = 1024
TN_MERGE = 256
TF = 256
TN_MOD = 1024
ROUTER_PAD = 128
ROW_CHUNK = 96
NORM_CHUNK = 32
NORM_UNROLL = 3
assert S % NORM_CHUNK == 0 and TM % (NORM_CHUNK * NORM_UNROLL) == 0
TG = 1024
NT = (2 * T + N_EXPERTS * (TG - 1) + TG - 1) // TG
MOD_ROWS = 16


def _cparams(semantics):
    return pltpu.CompilerParams(dimension_semantics=semantics,
                                vmem_limit_bytes=VMEM_LIMIT_BYTES)


def _sigmoid(x):
    return 1.0 / (1.0 + jnp.exp(-x))


def _silu(x):
    return x * _sigmoid(x)


def _mod_row(mod_ref, row0):
    return mod_ref[pl.ds(jnp.where(row0 >= S, 1, 0), 1), :]


def _rms(x, g):
    return x * lax.rsqrt(jnp.mean(x * x, axis=-1, keepdims=True) + EPS) * g


def _for_row_chunks(fn):
    def step(r, carry):
        off = pl.multiple_of(r * ROW_CHUNK, ROW_CHUNK)
        fn(pl.ds(off, ROW_CHUNK), off)
        return carry
    lax.fori_loop(0, TM // ROW_CHUNK, step, 0, unroll=True)


def _for_norm_chunks(fn):
    def step(r, carry):
        off = pl.multiple_of(r * NORM_CHUNK, NORM_CHUNK)
        fn(pl.ds(off, NORM_CHUNK), off)
        return carry
    lax.fori_loop(0, TM // NORM_CHUNK, step, 0, unroll=NORM_UNROLL)


def _norm_mod_tile(h_ref, g, sh_ref, sc_ref, m, store):
    def chunk(rows, off):
        row0 = m * TM + off
        u = _rms(h_ref[rows, :], g) * (1.0 + _mod_row(sc_ref, row0)) + _mod_row(sh_ref, row0)
        store(rows, u)
    _for_norm_chunks(chunk)


def _residual_tile(h_ref, y_ref, g, gate_ref, m):
    def chunk(rows, off):
        y_ref[rows, :] = h_ref[rows, :] + _mod_row(gate_ref, m * TM + off) * _rms(y_ref[rows, :], g)
    _for_norm_chunks(chunk)


def _dot(a, b):
    return jnp.dot(a, b, preferred_element_type=f32)


def _mod_kernel(cc_ref, w_ref, b_ref, o_ref):
    s = _silu(cc_ref[...])
    o_ref[...] = _dot(s.astype(bf16), w_ref[...].astype(bf16)) + b_ref[...]


def _modulation(cc, w_mod, b_mod):
    nn = (N_MOD * D) // TN_MOD
    return pl.pallas_call(
        _mod_kernel,
        grid=(DEPTH, nn),
        in_specs=[
            pl.BlockSpec((MOD_ROWS, D), lambda l, n: (0, 0)),
            pl.BlockSpec((None, D, TN_MOD), lambda l, n: (l, 0, n)),
            pl.BlockSpec((None, 1, TN_MOD), lambda l, n: (l, 0, n)),
        ],
        out_specs=pl.BlockSpec((None, MOD_ROWS, TN_MOD), lambda l, n: (l, 0, n)),
        out_shape=jax.ShapeDtypeStruct((DEPTH, MOD_ROWS, N_MOD * D), f32),
        compiler_params=_cparams(("arbitrary", "arbitrary")),
        name="modulation",
    )(cc, w_mod, b_mod.reshape(DEPTH, 1, N_MOD * D))


def _win_kernel(h_ref, ng_ref, sh_ref, sc_ref, w_ref, z_ref, u_ref):
    m = pl.program_id(0)

    @pl.when(pl.program_id(1) == 0)
    def _():
        def store(rows, u):
            u_ref[rows, :] = u.astype(bf16)
        _norm_mod_tile(h_ref, ng_ref[0:1, :], sh_ref, sc_ref, m, store)

    z_ref[...] = _dot(u_ref[...], w_ref[...].astype(bf16)).astype(bf16)


def _in_proj(h, norm_g, mod, w_in, layer):
    return pl.pallas_call(
        _win_kernel,
        grid=(NM, IN_WIDTH // TN_IN),
        in_specs=[
            pl.BlockSpec((TM, D), lambda m, n: (m, 0)),
            pl.BlockSpec((None, 4, D), lambda m, n: (layer, 0, 0)),
            pl.BlockSpec((None, MOD_ROWS, D), lambda m, n: (layer, 0, 0)),
            pl.BlockSpec((None, MOD_ROWS, D), lambda m, n: (layer, 0, 1)),
            pl.BlockSpec((None, D, TN_IN), lambda m, n: (layer, 0, n)),
        ],
        out_specs=pl.BlockSpec((TM, TN_IN), lambda m, n: (m, n)),
        out_shape=jax.ShapeDtypeStruct((T, IN_WIDTH), bf16),
        scratch_shapes=[pltpu.VMEM((TM, D), bf16)],
        compiler_params=_cparams(("arbitrary", "arbitrary")),
        name="in_proj",
    )(h, norm_g, mod, mod, w_in)


def _seqmix_kernel(zp_c, zp_p, zp_n, zc_c, zc_p, zc_n, pw_ref, ps_ref, dw_ref, db_ref,
                   lng_ref, lnb_ref, po_ref, co_ref, xe, ge, gs):
    i = pl.program_id(0)
    has_prev = jnp.where(jnp.logical_and(i != 0, i != NS_LAT), 1.0, 0.0).astype(f32)
    has_next = jnp.where(i < NS_LAT - 1, 1.0, 0.0).astype(f32)
    is_lat = i < NS_LAT
    pos = jnp.where(is_lat, i * TS, 0) + lax.broadcasted_iota(jnp.int32, (TS, 1), 0)
    seq_len = jnp.where(is_lat, S, C)

    xe[0:HALO, :] = zp_p[...].astype(f32) * has_prev
    xe[HALO:HALO + TS, :] = zp_c[...].astype(f32)
    xe[HALO + TS:, :] = zp_n[...].astype(f32) * has_next
    outs = []
    for g, w in enumerate(POOL_WINDOWS):
        cols = slice(g * POOL_GROUP, (g + 1) * POOL_GROUP)
        acc = xe[pl.ds(HALO - w // 2, TS), cols]
        for d in range(-w // 2 + 1, w // 2):
            acc = acc + xe[pl.ds(HALO + d, TS), cols]
        lo = jnp.clip(pos - w // 2, 0, seq_len)
        hi = jnp.clip(pos - w // 2 + w, 0, seq_len)
        diff = acc / (hi - lo).astype(f32) - xe[HALO:HALO + TS, cols]
        outs.append(_dot(diff.astype(bf16), pw_ref[g].astype(bf16)))
    po_ref[...] = (jnp.concatenate(outs, axis=1) * ps_ref[...]).astype(bf16)

    def glu(ref):
        zz = ref[...].astype(f32)
        return zz[:, :CONV_WIDTH] * _sigmoid(zz[:, CONV_WIDTH:])

    ge[0:HALO, :] = glu(zc_p) * has_prev
    ge[HALO:HALO + TS, :] = glu(zc_c)
    ge[HALO + TS:, :] = glu(zc_n) * has_next
    span = TS + 2 * HALO - SUBLANES
    for s in range(SUBLANES):
        gs[s] = ge[pl.ds(s, span), :]
    base = HALO - CONV_K // 2
    acc = None
    for k in range(CONV_K):
        shift, start = (base + k) % SUBLANES, (base + k) // SUBLANES * SUBLANES
        term = gs[shift, pl.ds(start, TS), :] * dw_ref[k:k + 1, :]
        acc = term if acc is None else acc + term
    y = acc + db_ref[...]
    yc = y - jnp.mean(y, axis=-1, keepdims=True)
    yn = yc * lax.rsqrt(jnp.mean(yc * yc, axis=-1, keepdims=True) + EPS) * lng_ref[...] + lnb_ref[...]
    co_ref[...] = _silu(yn).astype(bf16)


def _seq_mixers(z, pool_w, pool_scale, conv_dw, conv_db, conv_ln_g, conv_ln_b, layer):
    rb = TS // HALO
    last = T // HALO - 1
    pcol = POOL_OFF // POOL_WIDTH
    ccol = CONV_OFF // (2 * CONV_WIDTH)

    def prev(i):
        return jnp.maximum(i * rb - 1, 0)

    def nxt(i):
        return jnp.minimum((i + 1) * rb, last)

    vec = lambda width: pl.BlockSpec((None, 1, width), lambda i: (layer, 0, 0))
    return pl.pallas_call(
        _seqmix_kernel,
        grid=(NS,),
        in_specs=[
            pl.BlockSpec((TS, POOL_WIDTH), lambda i: (i, pcol)),
            pl.BlockSpec((HALO, POOL_WIDTH), lambda i: (prev(i), pcol)),
            pl.BlockSpec((HALO, POOL_WIDTH), lambda i: (nxt(i), pcol)),
            pl.BlockSpec((TS, 2 * CONV_WIDTH), lambda i: (i, ccol)),
            pl.BlockSpec((HALO, 2 * CONV_WIDTH), lambda i: (prev(i), ccol)),
            pl.BlockSpec((HALO, 2 * CONV_WIDTH), lambda i: (nxt(i), ccol)),
            pl.BlockSpec((None, 4, POOL_GROUP, POOL_GROUP), lambda i: (layer, 0, 0, 0)),
            vec(POOL_WIDTH),
            pl.BlockSpec((None, CONV_K, CONV_WIDTH), lambda i: (layer, 0, 0)),
            vec(CONV_WIDTH), vec(CONV_WIDTH), vec(CONV_WIDTH),
        ],
        out_specs=[
            pl.BlockSpec((TS, POOL_WIDTH), lambda i: (i, 0)),
            pl.BlockSpec((TS, CONV_WIDTH), lambda i: (i, 0)),
        ],
        out_shape=[
            jax.ShapeDtypeStruct((T, POOL_WIDTH), bf16),
            jax.ShapeDtypeStruct((T, CONV_WIDTH), bf16),
        ],
        scratch_shapes=[
            pltpu.VMEM((TS + 2 * HALO, POOL_WIDTH), f32),
            pltpu.VMEM((TS + 2 * HALO, CONV_WIDTH), f32),
            pltpu.VMEM((SUBLANES, TS + 2 * HALO - SUBLANES, CONV_WIDTH), f32),
        ],
        compiler_params=_cparams(("arbitrary",)),
        name="seq_mixers",
    )(z, z, z, z, z, z, pool_w, pool_scale.reshape(DEPTH, 1, POOL_WIDTH), conv_dw,
      conv_db.reshape(DEPTH, 1, CONV_WIDTH), conv_ln_g.reshape(DEPTH, 1, CONV_WIDTH),
      conv_ln_b.reshape(DEPTH, 1, CONV_WIDTH))


def _rope(x, cos_t, sin_t):
    lane = lax.broadcasted_iota(jnp.int32, x.shape, 1) & (2 * ROPE_FREQS - 1)
    partner = jnp.where(lane < ROPE_FREQS, pltpu.roll(x, HEAD_DIM - ROPE_FREQS, 1),
                        pltpu.roll(x, ROPE_FREQS, 1))
    return x * cos_t + partner * sin_t


def _attn_kernel(sink_ref, q_ref, kc_ref, kp_ref, kn_ref, vc_ref, vp_ref, vn_ref, kx_ref, vx_ref,
                 cq_ref, sq_ref, cp_ref, sp_ref, cn_ref, sn_ref, o_ref, *, layer):
    hk = pl.program_id(0)
    i = pl.program_id(1)
    cq = cq_ref[...]
    sq = sq_ref[...]
    qs = []
    for j in range(GQA):
        qj = q_ref[:, j * HEAD_DIM:(j + 1) * HEAD_DIM].astype(f32)
        qs.append(_rope(qj, cq, sq).astype(bf16))
    q = jnp.concatenate(qs, axis=0)
    k = jnp.concatenate([
        _rope(kp_ref[...].astype(f32), cp_ref[...], sp_ref[...]).astype(bf16),
        _rope(kc_ref[...].astype(f32), cq, sq).astype(bf16),
        _rope(kn_ref[...].astype(f32), cn_ref[...], sn_ref[...]).astype(bf16)], axis=0)
    v = jnp.concatenate([vp_ref[...], vc_ref[...], vn_ref[...]], axis=0)
    kw = k.shape[0]

    nt = (((1,), (1,)), ((), ()))
    s_loc = lax.dot_general(q, k, nt, preferred_element_type=f32) * ATTN_SCALE
    s_ctx = lax.dot_general(q, kx_ref[...], nt, preferred_element_type=f32) * ATTN_SCALE

    qrow = lax.broadcasted_iota(jnp.int32, (GQA * TS, 1), 0) & (TS - 1)
    krel = lax.broadcasted_iota(jnp.int32, (1, kw), 1) - WINDOW
    kpos = i * TS + krel
    k_ok = jnp.logical_and(jnp.logical_and(kpos >= 0, kpos < S), i < NS_LAT)
    valid = jnp.logical_and(jnp.abs(krel - qrow) <= WINDOW, k_ok)
    s_loc = jnp.where(valid, s_loc, NEG_INF)

    sink = jnp.concatenate(
        [jnp.full((TS, 1), sink_ref[layer, hk * GQA + j], f32) for j in range(GQA)], axis=0)
    mx = jnp.maximum(jnp.maximum(jnp.max(s_loc, axis=-1, keepdims=True),
                                 jnp.max(s_ctx, axis=-1, keepdims=True)), sink)
    p_loc = jnp.exp(s_loc - mx)
    p_ctx = jnp.exp(s_ctx - mx)
    denom = (jnp.sum(p_loc, axis=-1, keepdims=True) + jnp.sum(p_ctx, axis=-1, keepdims=True)
             + jnp.exp(sink - mx))
    o = (_dot(p_loc.astype(bf16), v) + _dot(p_ctx.astype(bf16), vx_ref[...])) / denom
    for j in range(GQA):
        o_ref[:, j * HEAD_DIM:(j + 1) * HEAD_DIM] = o[j * TS:(j + 1) * TS, :].astype(bf16)


def _attention(z, attn_sink, cos_t, sin_t, layer):
    qcol = Q_OFF // (GQA * HEAD_DIM)
    kcol = K_OFF // HEAD_DIM
    vcol = V_OFF // HEAD_DIM
    hb = TS // WINDOW
    last = T // WINDOW - 1
    ctx_blk = S // TS

    def prev(i):
        return jnp.maximum(i * hb - 1, 0)

    def nxt(i):
        return jnp.minimum((i + 1) * hb, last)

    def kv_specs(col):
        return [
            pl.BlockSpec((TS, HEAD_DIM), lambda h, i: (i, col + h)),
            pl.BlockSpec((WINDOW, HEAD_DIM), lambda h, i: (prev(i), col + h)),
            pl.BlockSpec((WINDOW, HEAD_DIM), lambda h, i: (nxt(i), col + h)),
        ]

    tab_specs = [
        pl.BlockSpec((TS, HEAD_DIM), lambda h, i: (i, 0)),
        pl.BlockSpec((TS, HEAD_DIM), lambda h, i: (i, 0)),
        pl.BlockSpec((WINDOW, HEAD_DIM), lambda h, i: (prev(i), 0)),
        pl.BlockSpec((WINDOW, HEAD_DIM), lambda h, i: (prev(i), 0)),
        pl.BlockSpec((WINDOW, HEAD_DIM), lambda h, i: (nxt(i), 0)),
        pl.BlockSpec((WINDOW, HEAD_DIM), lambda h, i: (nxt(i), 0)),
    ]
    return pl.pallas_call(
        functools.partial(_attn_kernel, layer=layer),
        grid=(N_KV_HEADS, NS),
        in_specs=[pl.BlockSpec(memory_space=pltpu.SMEM),
                  pl.BlockSpec((TS, GQA * HEAD_DIM), lambda h, i: (i, qcol + h))]
                 + kv_specs(kcol) + kv_specs(vcol)
                 + [pl.BlockSpec((TS, HEAD_DIM), lambda h, i: (ctx_blk, kcol + h)),
                    pl.BlockSpec((TS, HEAD_DIM), lambda h, i: (ctx_blk, vcol + h))]
                 + tab_specs,
        out_specs=pl.BlockSpec((TS, GQA * HEAD_DIM), lambda h, i: (i, h)),
        out_shape=jax.ShapeDtypeStruct((T, N_Q_HEADS * HEAD_DIM), bf16),
        compiler_params=_cparams(("arbitrary", "arbitrary")),
        name="attention",
    )(attn_sink, z, z, z, z, z, z, z, z, z, cos_t, sin_t, cos_t, sin_t, cos_t, sin_t)


NB_MERGE = D // TN_MERGE


def _merge_kernel(h_ref, p_ref, a_ref, c_ref, zg0, zg1, zg2, bg0, bg1, bg2, wp, wa, wc, wo,
                  ng_ref, g1_ref, o_ref, act_ref):
    m = pl.program_id(0)
    n = pl.program_id(1)

    def gate(zg, bg):
        return _sigmoid(zg[...].astype(f32) + bg[...])

    def up():
        merged = (gate(zg0, bg0) * _dot(p_ref[...], wp[...].astype(bf16))
                  + gate(zg1, bg1) * _dot(a_ref[...], wa[...].astype(bf16))
                  + gate(zg2, bg2) * _dot(c_ref[...], wc[...].astype(bf16)))
        return merged.astype(bf16)

    def down():
        return _dot(act_ref[...], wo[...].astype(bf16))

    _skewed_pipeline(n, NB_MERGE, up, down, act_ref, o_ref)

    @pl.when(n == NB_MERGE)
    def _():
        _residual_tile(h_ref, o_ref, ng_ref[1:2, :], g1_ref, m)


def _merge(h, z, pool_o, attn_o, conv_o, b_gate, w_pool_up, w_attn_up, w_conv_up, w_out,
           norm_g, mod, layer):
    nb = NB_MERGE
    bg = b_gate.reshape(DEPTH, 1, GATE_WIDTH)
    upc = lambda n: _up_chunk(n, nb)
    zgate = lambda b: pl.BlockSpec((TM, TN_MERGE), lambda m, n: (m, b * nb + upc(n)))
    bgate = lambda b: pl.BlockSpec((None, 1, TN_MERGE), lambda m, n: (layer, 0, b * nb + upc(n)))
    wup = lambda k: pl.BlockSpec((None, k, TN_MERGE), lambda m, n: (layer, 0, upc(n)))
    return pl.pallas_call(
        _merge_kernel,
        grid=(NM, nb + 1),
        in_specs=[
            pl.BlockSpec((TM, D), lambda m, n: (m, 0), pipeline_mode=pl.Buffered(1)),
            pl.BlockSpec((TM, POOL_WIDTH), lambda m, n: (m, 0)),
            pl.BlockSpec((TM, N_Q_HEADS * HEAD_DIM), lambda m, n: (m, 0)),
            pl.BlockSpec((TM, CONV_WIDTH), lambda m, n: (m, 0)),
            zgate(0), zgate(1), zgate(2), bgate(0), bgate(1), bgate(2),
            wup(POOL_WIDTH), wup(N_Q_HEADS * HEAD_DIM), wup(CONV_WIDTH),
            pl.BlockSpec((None, TN_MERGE, D), lambda m, n: (layer, _down_chunk(n, nb), 0)),
            pl.BlockSpec((None, 4, D), lambda m, n: (layer, 0, 0)),
            pl.BlockSpec((None, MOD_ROWS, D), lambda m, n: (layer, 0, 2)),
        ],
        out_specs=pl.BlockSpec((TM, D), lambda m, n: (m, 0)),
        out_shape=jax.ShapeDtypeStruct((T, D), f32),
        scratch_shapes=[pltpu.VMEM((TM, TN_MERGE), bf16)],
        compiler_params=_cparams(("arbitrary", "arbitrary")),
        name="merge",
    )(h, pool_o, attn_o, conv_o, z, z, z, bg, bg, bg, w_pool_up, w_attn_up, w_conv_up, w_out,
      norm_g, mod)


def _ffn_prologue(h_ref, ng_ref, sh_ref, sc_ref, u_ref, m):
    def store(rows, u):
        u_ref[rows, :] = u.astype(bf16)
    _norm_mod_tile(h_ref, ng_ref[2:3, :], sh_ref, sc_ref, m, store)


def _ffn_epilogue(h_ref, ng_ref, g2_ref, o_ref, m):
    _residual_tile(h_ref, o_ref, ng_ref[3:4, :], g2_ref, m)


NF = FFN // TF


def _skewed_pipeline(step, n_chunks, up, down, act_ref, o_ref):
    @pl.when(step == 0)
    def _():
        act_ref[...] = up()

    @pl.when(step == 1)
    def _():
        part = down()
        act = up()
        o_ref[...] = part
        act_ref[...] = act

    @pl.when(jnp.logical_and(step > 1, step < n_chunks))
    def _():
        part = down()
        act = up()
        o_ref[...] += part
        act_ref[...] = act

    @pl.when(step == n_chunks)
    def _():
        o_ref[...] += down()


def _up_chunk(step, n_chunks):
    return jnp.minimum(step, n_chunks - 1)


def _down_chunk(step, n_chunks):
    del n_chunks
    return jnp.maximum(step - 1, 0)


def _swiglu_pipeline(f, u_ref, w1, w3, w2, act_ref, o_ref):
    def up():
        u = u_ref[...]
        act = _silu(_dot(u, w1[...].astype(bf16))) * _dot(u, w3[...].astype(bf16))
        return act.astype(bf16)

    def down():
        return _dot(act_ref[...], w2[...].astype(bf16))

    _skewed_pipeline(f, NF, up, down, act_ref, o_ref)


def _ffn_kernel(h_ref, ng_ref, sh_ref, sc_ref, g2_ref, w1, w3, w2, o_ref, u_ref, act_ref):
    m = pl.program_id(0)
    f = pl.program_id(1)

    @pl.when(f == 0)
    def _():
        _ffn_prologue(h_ref, ng_ref, sh_ref, sc_ref, u_ref, m)

    _swiglu_pipeline(f, u_ref, w1, w3, w2, act_ref, o_ref)

    @pl.when(f == NF)
    def _():
        _ffn_epilogue(h_ref, ng_ref, g2_ref, o_ref, m)


def _mod_specs(layer, chunks):
    specs = [pl.BlockSpec((TM, D), lambda m, *r: (m, 0), pipeline_mode=pl.Buffered(1)),
             pl.BlockSpec((None, 4, D), lambda m, *r: (layer, 0, 0))]
    for ch in chunks:
        specs.append(pl.BlockSpec((None, MOD_ROWS, D), lambda m, *r, ch=ch: (layer, 0, ch)))
    return specs


def _dense_ffn(h, norm_g, mod, w1, w3, w2, layer):
    j = layer // 2
    return pl.pallas_call(
        _ffn_kernel,
        grid=(NM, NF + 1),
        in_specs=_mod_specs(layer, (3, 4, 5)) + [
            pl.BlockSpec((None, D, TF), lambda m, f: (j, 0, _up_chunk(f, NF))),
            pl.BlockSpec((None, D, TF), lambda m, f: (j, 0, _up_chunk(f, NF))),
            pl.BlockSpec((None, TF, D), lambda m, f: (j, _down_chunk(f, NF), 0)),
        ],
        out_specs=pl.BlockSpec((TM, D), lambda m, f: (m, 0)),
        out_shape=jax.ShapeDtypeStruct((T, D), f32),
        scratch_shapes=[pltpu.VMEM((TM, D), bf16), pltpu.VMEM((TM, TF), bf16)],
        compiler_params=_cparams(("arbitrary", "arbitrary")),
        name="dense_ffn",
    )(h, norm_g, mod, mod, mod, w1, w3, w2)


def _split_bf16(x):
    hi = x.astype(bf16)
    lo = (x - hi.astype(f32)).astype(bf16)
    return hi, lo


R_E0, R_E1, R_G0, R_G1, R_R0, R_R1 = range(6)


def _router_kernel(h_ref, ng_ref, sh_ref, sc_ref, rw_ref, route_ref, cnt_ref, u_ref, base_ref, lg_ref):
    m = pl.program_id(0)

    @pl.when(m == 0)
    def _():
        base_ref[...] = jnp.zeros_like(base_ref)

    def store(rows, u):
        u_ref[rows, :] = u

    _norm_mod_tile(h_ref, ng_ref[2:3, :], sh_ref, sc_ref, m, store)

    wh, wl = _split_bf16(rw_ref[...])
    for half in range(2):
        sl = pl.ds(half * (TM // 2), TM // 2)
        uh, ul = _split_bf16(u_ref[sl, :])
        lg_ref[sl, :] = _dot(uh, wh) + (_dot(uh, wl) + _dot(ul, wh))

    ri = lax.broadcasted_iota(jnp.int32, (ROW_CHUNK, ROW_CHUNK), 0)
    ci = lax.broadcasted_iota(jnp.int32, (ROW_CHUNK, ROW_CHUNK), 1)
    earlier = jnp.where(ri > ci, 1.0, 0.0).astype(bf16)

    def chunk(rows, off):
        del off
        logits = lg_ref[rows, :]
        lane = lax.broadcasted_iota(jnp.int32, logits.shape, 1)
        lg = jnp.where(lane < N_EXPERTS, logits, -jnp.inf)
        m1 = jnp.max(lg, axis=-1, keepdims=True)
        i1 = jnp.min(jnp.where(lg == m1, lane, ROUTER_PAD), axis=-1, keepdims=True)
        lg2 = jnp.where(lane == i1, -jnp.inf, lg)
        m2 = jnp.max(lg2, axis=-1, keepdims=True)
        i2 = jnp.min(jnp.where(lg2 == m2, lane, ROUTER_PAD), axis=-1, keepdims=True)
        e2 = jnp.exp(m2 - m1)
        den = 1.0 + e2
        hit1 = lane == i1
        hit2 = lane == i2
        onehot = jnp.where(jnp.logical_or(hit1, hit2), 1.0, 0.0)
        before = _dot(earlier, onehot.astype(bf16)) + base_ref[...]
        r1 = jnp.sum(jnp.where(hit1, before, 0.0), axis=-1, keepdims=True)
        r2 = jnp.sum(jnp.where(hit2, before, 0.0), axis=-1, keepdims=True)
        base_ref[...] = base_ref[...] + jnp.sum(onehot, axis=0, keepdims=True)
        vals = (i1.astype(f32), i2.astype(f32), 1.0 / den, e2 / den, r1, r2)
        route = jnp.zeros(logits.shape, f32)
        for col, val in enumerate(vals):
            route = jnp.where(lane == col, val, route)
        route_ref[rows, :] = route

    _for_row_chunks(chunk)
    cnt_ref[...] = jnp.broadcast_to(base_ref[...], cnt_ref.shape)


def _router(h, norm_g, mod, router_w_pad, layer):
    j = layer // 2
    return pl.pallas_call(
        _router_kernel,
        grid=(NM,),
        in_specs=_mod_specs(layer, (3, 4)) + [
            pl.BlockSpec((None, D, ROUTER_PAD), lambda m: (j, 0, 0)),
        ],
        out_specs=[
            pl.BlockSpec((TM, ROUTER_PAD), lambda m: (m, 0)),
            pl.BlockSpec((8, ROUTER_PAD), lambda m: (0, 0)),
            pl.BlockSpec((TM, D), lambda m: (m, 0)),
        ],
        out_shape=[
            jax.ShapeDtypeStruct((T, ROUTER_PAD), f32),
            jax.ShapeDtypeStruct((8, ROUTER_PAD), f32),
            jax.ShapeDtypeStruct((T, D), f32),
        ],
        scratch_shapes=[pltpu.VMEM((1, ROUTER_PAD), f32), pltpu.VMEM((TM, ROUTER_PAD), f32)],
        compiler_params=_cparams(("arbitrary",)),
        name="router",
    )(h, norm_g, mod, mod, router_w_pad)


def _dispatch_plan(route, counts):
    expert = route[:, R_E0:R_E1 + 1].astype(jnp.int32)
    rank = route[:, R_R0:R_R1 + 1].astype(jnp.int32)
    cnt = counts[0, :N_EXPERTS].astype(jnp.int32)
    tiles = (cnt + TG - 1) // TG
    tile_end = jnp.cumsum(tiles)
    start = (tile_end - tiles) * TG
    dest = (start[expert] + rank).reshape(2 * T)
    n_used = tile_end[-1]
    tile_id = jnp.minimum(jnp.arange(NT, dtype=jnp.int32), n_used - 1)
    tile_expert = jnp.sum(tile_id[:, None] >= tile_end[None, :], axis=1).astype(jnp.int32)
    rows_used = cnt[tile_expert] - (tile_id - (tile_end - tiles)[tile_expert]) * TG
    tile_mode = jnp.where(jnp.arange(NT, dtype=jnp.int32) >= n_used, TILE_UNUSED,
                          jnp.where(rows_used <= TG // 2, TILE_HALF, TILE_FULL)).astype(jnp.int32)
    return dest, tile_expert, tile_mode


def _row_copy(src_ref, src_row, dst_ref, dst_row, sem):
    return pltpu.make_async_copy(src_ref.at[pl.ds(src_row, 1)], dst_ref.at[pl.ds(dst_row, 1)], sem)


def _dispatch_kernel(dest_ref, u_ref, xs_in_ref, xs_ref, sem):
    del xs_in_ref
    m = pl.program_id(0)

    def issue(r, carry):
        t = m * TM + r
        _row_copy(u_ref, r, xs_ref, dest_ref[2 * t], sem).start()
        _row_copy(u_ref, r, xs_ref, dest_ref[2 * t + 1], sem).start()
        return carry

    lax.fori_loop(0, TM, issue, 0, unroll=8)
    for _ in range(2):
        pltpu.make_async_copy(u_ref, xs_ref.at[pl.ds(0, TM)], sem).wait()


def _dispatch(dest, u):
    xs0 = jnp.zeros((NT * TG, D), f32)
    return pl.pallas_call(
        _dispatch_kernel,
        grid_spec=pltpu.PrefetchScalarGridSpec(
            num_scalar_prefetch=1,
            grid=(NM,),
            in_specs=[pl.BlockSpec((TM, D), lambda m, d: (m, 0)), pl.BlockSpec(memory_space=pl.ANY)],
            out_specs=pl.BlockSpec(memory_space=pl.ANY),
            scratch_shapes=[pltpu.SemaphoreType.DMA(())],
        ),
        out_shape=jax.ShapeDtypeStruct((NT * TG, D), f32),
        input_output_aliases={2: 0},
        compiler_params=_cparams(("arbitrary",)),
        name="moe_dispatch",
    )(dest, u, xs0)


TILE_UNUSED, TILE_HALF, TILE_FULL = 0, 1, 2


def _group_kernel(te_ref, tm_ref, xs_ref, w1, w3, w2, ys_ref, u_ref, act_ref):
    del te_ref
    j = pl.program_id(0)
    f = pl.program_id(1)
    mode = tm_ref[j]

    @pl.when(jnp.logical_and(mode != TILE_FULL, f == 0))
    def _():
        ys_ref[...] = jnp.zeros_like(ys_ref)

    def run(rows):
        sl = pl.ds(0, rows)

        @pl.when(f == 0)
        def _():
            u_ref[sl, :] = xs_ref[sl, :].astype(bf16)

        _swiglu_pipeline(f, u_ref.at[sl], w1, w3, w2, act_ref.at[sl], ys_ref.at[sl])

    @pl.when(mode == TILE_FULL)
    def _():
        run(TG)

    @pl.when(mode == TILE_HALF)
    def _():
        run(TG // 2)


def _grouped_ffn(tile_expert, tile_mode, xs, w1, w3, w2, layer):
    jl = layer // 2

    def chunk(j, f, tmode, which):
        return jnp.where(tmode[j] != TILE_UNUSED, which(f, NF), NF - 1)

    return pl.pallas_call(
        _group_kernel,
        grid_spec=pltpu.PrefetchScalarGridSpec(
            num_scalar_prefetch=2,
            grid=(NT, NF + 1),
            in_specs=[
                pl.BlockSpec((TG, D), lambda j, f, te, tu: (j, 0)),
                pl.BlockSpec((None, None, D, TF),
                             lambda j, f, te, tu: (jl, te[j], 0, chunk(j, f, tu, _up_chunk))),
                pl.BlockSpec((None, None, D, TF),
                             lambda j, f, te, tu: (jl, te[j], 0, chunk(j, f, tu, _up_chunk))),
                pl.BlockSpec((None, None, TF, D),
                             lambda j, f, te, tu: (jl, te[j], chunk(j, f, tu, _down_chunk), 0)),
            ],
            out_specs=pl.BlockSpec((TG, D), lambda j, f, te, tu: (j, 0)),
            scratch_shapes=[pltpu.VMEM((TG, D), bf16), pltpu.VMEM((TG, TF), bf16)],
        ),
        out_shape=jax.ShapeDtypeStruct((NT * TG, D), f32),
        compiler_params=_cparams(("arbitrary", "arbitrary")),
        name="moe_grouped_ffn",
    )(tile_expert, tile_mode, xs, w1, w3, w2)


def _combine_kernel(dest_ref, h_ref, route_ref, ng_ref, g2_ref, ys_ref, o_ref, ybuf, sem):
    m = pl.program_id(0)

    def issue(r, carry):
        t = m * TM + r
        _row_copy(ys_ref, dest_ref[2 * t], ybuf.at[0], r, sem).start()
        _row_copy(ys_ref, dest_ref[2 * t + 1], ybuf.at[1], r, sem).start()
        return carry

    lax.fori_loop(0, TM, issue, 0, unroll=8)
    for k in range(2):
        pltpu.make_async_copy(ys_ref.at[pl.ds(0, TM)], ybuf.at[k], sem).wait()

    def chunk(rows, off):
        route = route_ref[rows, :]
        y = route[:, R_G0:R_G0 + 1] * ybuf[0, rows, :] + route[:, R_G1:R_G1 + 1] * ybuf[1, rows, :]
        o_ref[rows, :] = h_ref[rows, :] + _mod_row(g2_ref, m * TM + off) * _rms(y, ng_ref[3:4, :])

    _for_norm_chunks(chunk)


def _combine(dest, h, route, norm_g, mod, ys, layer):
    return pl.pallas_call(
        _combine_kernel,
        grid_spec=pltpu.PrefetchScalarGridSpec(
            num_scalar_prefetch=1,
            grid=(NM,),
            in_specs=[
                pl.BlockSpec((TM, D), lambda m, d: (m, 0), pipeline_mode=pl.Buffered(1)),
                pl.BlockSpec((TM, ROUTER_PAD), lambda m, d: (m, 0)),
                pl.BlockSpec((None, 4, D), lambda m, d: (layer, 0, 0)),
                pl.BlockSpec((None, MOD_ROWS, D), lambda m, d: (layer, 0, 5)),
                pl.BlockSpec(memory_space=pl.ANY),
            ],
            out_specs=pl.BlockSpec((TM, D), lambda m, d: (m, 0)),
            scratch_shapes=[pltpu.VMEM((2, TM, D), f32), pltpu.SemaphoreType.DMA(())],
        ),
        out_shape=jax.ShapeDtypeStruct((T, D), f32),
        compiler_params=_cparams(("arbitrary",)),
        name="moe_combine",
    )(dest, h, route, norm_g, mod, ys)


def _moe_ffn(h, norm_g, mod, router_w_pad, w1, w3, w2, layer):
    route, counts, u = _router(h, norm_g, mod, router_w_pad, layer)
    dest, tile_expert, tile_mode = _dispatch_plan(route, counts)
    xs = _dispatch(dest, u)
    ys = _grouped_ffn(tile_expert, tile_mode, xs, w1, w3, w2, layer)
    return _combine(dest, h, route, norm_g, mod, ys, layer)


def _rope_tables():
    t = jnp.arange(S)
    row = (t // GRID_W).astype(f32)
    col = (t % GRID_W).astype(f32)
    inv = ROPE_BASE ** (-jnp.arange(ROPE_FREQS, dtype=f32) / ROPE_FREQS)
    a0 = row[:, None] * inv
    a1 = col[:, None] * inv
    cos_t = jnp.concatenate([jnp.cos(a0), jnp.cos(a0), jnp.cos(a1), jnp.cos(a1)], axis=1)
    sin_t = jnp.concatenate([-jnp.sin(a0), jnp.sin(a0), -jnp.sin(a1), jnp.sin(a1)], axis=1)
    cos_t = jnp.concatenate([cos_t, jnp.ones((C, HEAD_DIM), f32)], axis=0)
    sin_t = jnp.concatenate([sin_t, jnp.zeros((C, HEAD_DIM), f32)], axis=0)
    return cos_t, sin_t


def kernel(x, c, ctx, c_ctx, w_mod, b_mod, norm_g, w_in, b_gate, pool_w, pool_scale, w_pool_up,
           attn_sink, w_attn_up, conv_dw, conv_db, conv_ln_g, conv_ln_b, w_conv_up, w_out,
           ffn_w1, ffn_w3, ffn_w2, router_w, moe_w1, moe_w3, moe_w2):
    assert x.shape == (1, S, D) and ctx.shape == (1, C, D)
    cos_t, sin_t = _rope_tables()
    cc = jnp.concatenate([c, c_ctx[None, :], jnp.zeros((MOD_ROWS - 2, D), f32)], axis=0)
    mod = _modulation(cc, w_mod, b_mod)
    router_w_pad = jnp.pad(router_w, ((0, 0), (0, 0), (0, ROUTER_PAD - N_EXPERTS)))
    h = jnp.concatenate([x[0], ctx[0]], axis=0)
    for i in range(DEPTH):
        z = _in_proj(h, norm_g, mod, w_in, i)
        pool_o, conv_o = _seq_mixers(z, pool_w, pool_scale, conv_dw, conv_db, conv_ln_g, conv_ln_b, i)
        attn_o = _attention(z, attn_sink, cos_t, sin_t, i)
        h = _merge(h, z, pool_o, attn_o, conv_o, b_gate, w_pool_up, w_attn_up, w_conv_up, w_out,
                   norm_g, mod, i)
        if i % 2 == 0:
            h = _dense_ffn(h, norm_g, mod, ffn_w1, ffn_w3, ffn_w2, i)
        else:
            h = _moe_ffn(h, norm_g, mod, router_w_pad, moe_w1, moe_w3, moe_w2, i)
    return h[:S][None]
```

```python
import functools

import jax
import jax.numpy as jnp
from jax import lax
from jax.experimental import pallas as pl
from jax.experimental.pallas import tpu as pltpu

f32 = jnp.float32
bf16 = jnp.bfloat16

D = 2048
S = 8192
C = 256
T = S + C
DEPTH = 4
GRID_W = 64
HEAD_DIM = 128
POOL_WIDTH = 512
POOL_WINDOWS = (2, 4, 8, 16)
POOL_GROUP = 128
N_Q_HEADS = 8
N_KV_HEADS = 2
GQA = 4
WINDOW = 128
ATTN_SCALE = HEAD_DIM ** -0.5
ROPE_BASE = 10000.0
ROPE_FREQS = 32
CONV_WIDTH = 512
CONV_K = 31
GATE_WIDTH = 3 * D
POOL_OFF = GATE_WIDTH
Q_OFF = POOL_OFF + POOL_WIDTH
K_OFF = Q_OFF + N_Q_HEADS * HEAD_DIM
V_OFF = K_OFF + N_KV_HEADS * HEAD_DIM
CONV_OFF = V_OFF + N_KV_HEADS * HEAD_DIM
IN_WIDTH = CONV_OFF + 2 * CONV_WIDTH
FFN = 5632
N_EXPERTS = 8
N_MOD = 6
EPS = 1e-6
NEG_INF = -1e30

VMEM_LIMIT_BYTES = 56 * 1024 * 1024

TM = 1056
NM = T // TM
TS = 256
NS = T // TS
NS_LAT = S // TS
HALO = 16
SUBLANES = 8
TN_IN = 1024
TN_MERGE = 256
TF = 256
TN_MOD = 1024
ROUTER_PAD = 128
ROW_CHUNK = 96
NORM_CHUNK = 32
NORM_UNROLL = 3
assert S % NORM_CHUNK == 0 and TM % (NORM_CHUNK * NORM_UNROLL) == 0
TG = 1024
NT = (2 * T + N_EXPERTS * (TG - 1) + TG - 1) // TG
MOD_ROWS = 16


def _cparams(semantics):
    return pltpu.CompilerParams(dimension_semantics=semantics,
                                vmem_limit_bytes=VMEM_LIMIT_BYTES)


def _sigmoid(x):
    return 1.0 / (1.0 + jnp.exp(-x))


def _silu(x):
    return x * _sigmoid(x)


def _mod_row(mod_ref, row0):
    return mod_ref[pl.ds(jnp.where(row0 >= S, 1, 0), 1), :]


def _rms(x, g):
    return x * lax.rsqrt(jnp.mean(x * x, axis=-1, keepdims=True) + EPS) * g


def _for_row_chunks(fn):
    def step(r, carry):
        off = pl.multiple_of(r * ROW_CHUNK, ROW_CHUNK)
        fn(pl.ds(off, ROW_CHUNK), off)
        return carry
    lax.fori_loop(0, TM // ROW_CHUNK, step, 0, unroll=True)


def _for_norm_chunks(fn):
    def step(r, carry):
        off = pl.multiple_of(r * NORM_CHUNK, NORM_CHUNK)
        fn(pl.ds(off, NORM_CHUNK), off)
        return carry
    lax.fori_loop(0, TM // NORM_CHUNK, step, 0, unroll=NORM_UNROLL)


def _norm_mod_tile(h_ref, g, sh_ref, sc_ref, m, store):
    def chunk(rows, off):
        row0 = m * TM + off
        u = _rms(h_ref[rows, :], g) * (1.0 + _mod_row(sc_ref, row0)) + _mod_row(sh_ref, row0)
        store(rows, u)
    _for_norm_chunks(chunk)


def _residual_tile(h_ref, y_ref, g, gate_ref, m):
    def chunk(rows, off):
        y_ref[rows, :] = h_ref[rows, :] + _mod_row(gate_ref, m * TM + off) * _rms(y_ref[rows, :], g)
    _for_norm_chunks(chunk)


def _dot(a, b):
    return jnp.dot(a, b, preferred_element_type=f32)


def _mod_kernel(cc_ref, w_ref, b_ref, o_ref):
    s = _silu(cc_ref[...])
    o_ref[...] = _dot(s.astype(bf16), w_ref[...].astype(bf16)) + b_ref[...]


def _modulation(cc, w_mod, b_mod):
    nn = (N_MOD * D) // TN_MOD
    return pl.pallas_call(
        _mod_kernel,
        grid=(DEPTH, nn),
        in_specs=[
            pl.BlockSpec((MOD_ROWS, D), lambda l, n: (0, 0)),
            pl.BlockSpec((None, D, TN_MOD), lambda l, n: (l, 0, n)),
            pl.BlockSpec((None, 1, TN_MOD), lambda l, n: (l, 0, n)),
        ],
        out_specs=pl.BlockSpec((None, MOD_ROWS, TN_MOD), lambda l, n: (l, 0, n)),
        out_shape=jax.ShapeDtypeStruct((DEPTH, MOD_ROWS, N_MOD * D), f32),
        compiler_params=_cparams(("arbitrary", "arbitrary")),
        name="modulation",
    )(cc, w_mod, b_mod.reshape(DEPTH, 1, N_MOD * D))


def _win_kernel(h_ref, ng_ref, sh_ref, sc_ref, w_ref, z_ref, u_ref):
    m = pl.program_id(0)

    @pl.when(pl.program_id(1) == 0)
    def _():
        def store(rows, u):
            u_ref[rows, :] = u.astype(bf16)
        _norm_mod_tile(h_ref, ng_ref[0:1, :], sh_ref, sc_ref, m, store)

    z_ref[...] = _dot(u_ref[...], w_ref[...].astype(bf16)).astype(bf16)


def _in_proj(h, norm_g, mod, w_in, layer):
    return pl.pallas_call(
        _win_kernel,
        grid=(NM, IN_WIDTH // TN_IN),
        in_specs=[
            pl.BlockSpec((TM, D), lambda m, n: (m, 0)),
            pl.BlockSpec((None, 4, D), lambda m, n: (layer, 0, 0)),
            pl.BlockSpec((None, MOD_ROWS, D), lambda m, n: (layer, 0, 0)),
            pl.BlockSpec((None, MOD_ROWS, D), lambda m, n: (layer, 0, 1)),
            pl.BlockSpec((None, D, TN_IN), lambda m, n: (layer, 0, n)),
        ],
        out_specs=pl.BlockSpec((TM, TN_IN), lambda m, n: (m, n)),
        out_shape=jax.ShapeDtypeStruct((T, IN_WIDTH), bf16),
        scratch_shapes=[pltpu.VMEM((TM, D), bf16)],
        compiler_params=_cparams(("arbitrary", "arbitrary")),
        name="in_proj",
    )(h, norm_g, mod, mod, w_in)


def _seqmix_kernel(zp_c, zp_p, zp_n, zc_c, zc_p, zc_n, pw_ref, ps_ref, dw_ref, db_ref,
                   lng_ref, lnb_ref, po_ref, co_ref, xe, ge, gs):
    i = pl.program_id(0)
    has_prev = jnp.where(jnp.logical_and(i != 0, i != NS_LAT), 1.0, 0.0).astype(f32)
    has_next = jnp.where(i < NS_LAT - 1, 1.0, 0.0).astype(f32)
    is_lat = i < NS_LAT
    pos = jnp.where(is_lat, i * TS, 0) + lax.broadcasted_iota(jnp.int32, (TS, 1), 0)
    seq_len = jnp.where(is_lat, S, C)

    xe[0:HALO, :] = zp_p[...].astype(f32) * has_prev
    xe[HALO:HALO + TS, :] = zp_c[...].astype(f32)
    xe[HALO + TS:, :] = zp_n[...].astype(f32) * has_next
    outs = []
    for g, w in enumerate(POOL_WINDOWS):
        cols = slice(g * POOL_GROUP, (g + 1) * POOL_GROUP)
        acc = xe[pl.ds(HALO - w // 2, TS), cols]
        for d in range(-w // 2 + 1, w // 2):
            acc = acc + xe[pl.ds(HALO + d, TS), cols]
        lo = jnp.clip(pos - w // 2, 0, seq_len)
        hi = jnp.clip(pos - w // 2 + w, 0, seq_len)
        diff = acc / (hi - lo).astype(f32) - xe[HALO:HALO + TS, cols]
        outs.append(_dot(diff.astype(bf16), pw_ref[g].astype(bf16)))
    po_ref[...] = (jnp.concatenate(outs, axis=1) * ps_ref[...]).astype(bf16)

    def glu(ref):
        zz = ref[...].astype(f32)
        return zz[:, :CONV_WIDTH] * _sigmoid(zz[:, CONV_WIDTH:])

    ge[0:HALO, :] = glu(zc_p) * has_prev
    ge[HALO:HALO + TS, :] = glu(zc_c)
    ge[HALO + TS:, :] = glu(zc_n) * has_next
    span = TS + 2 * HALO - SUBLANES
    for s in range(SUBLANES):
        gs[s] = ge[pl.ds(s, span), :]
    base = HALO - CONV_K // 2
    acc = None
    for k in range(CONV_K):
        shift, start = (base + k) % SUBLANES, (base + k) // SUBLANES * SUBLANES
        term = gs[shift, pl.ds(start, TS), :] * dw_ref[k:k + 1, :]
        acc = term if acc is None else acc + term
    y = acc + db_ref[...]
    yc = y - jnp.mean(y, axis=-1, keepdims=True)
    yn = yc * lax.rsqrt(jnp.mean(yc * yc, axis=-1, keepdims=True) + EPS) * lng_ref[...] + lnb_ref[...]
    co_ref[...] = _silu(yn).astype(bf16)


def _seq_mixers(z, pool_w, pool_scale, conv_dw, conv_db, conv_ln_g, conv_ln_b, layer):
    rb = TS // HALO
    last = T // HALO - 1
    pcol = POOL_OFF // POOL_WIDTH
    ccol = CONV_OFF // (2 * CONV_WIDTH)

    def prev(i):
        return jnp.maximum(i * rb - 1, 0)

    def nxt(i):
        return jnp.minimum((i + 1) * rb, last)

    vec = lambda width: pl.BlockSpec((None, 1, width), lambda i: (layer, 0, 0))
    return pl.pallas_call(
        _seqmix_kernel,
        grid=(NS,),
        in_specs=[
            pl.BlockSpec((TS, POOL_WIDTH), lambda i: (i, pcol)),
            pl.BlockSpec((HALO, POOL_WIDTH), lambda i: (prev(i), pcol)),
            pl.BlockSpec((HALO, POOL_WIDTH), lambda i: (nxt(i), pcol)),
            pl.BlockSpec((TS, 2 * CONV_WIDTH), lambda i: (i, ccol)),
            pl.BlockSpec((HALO, 2 * CONV_WIDTH), lambda i: (prev(i), ccol)),
            pl.BlockSpec((HALO, 2 * CONV_WIDTH), lambda i: (nxt(i), ccol)),
            pl.BlockSpec((None, 4, POOL_GROUP, POOL_GROUP), lambda i: (layer, 0, 0, 0)),
            vec(POOL_WIDTH),
            pl.BlockSpec((None, CONV_K, CONV_WIDTH), lambda i: (layer, 0, 0)),
            vec(CONV_WIDTH), vec(CONV_WIDTH), vec(CONV_WIDTH),
        ],
        out_specs=[
            pl.BlockSpec((TS, POOL_WIDTH), lambda i: (i, 0)),
            pl.BlockSpec((TS, CONV_WIDTH), lambda i: (i, 0)),
        ],
        out_shape=[
            jax.ShapeDtypeStruct((T, POOL_WIDTH), bf16),
            jax.ShapeDtypeStruct((T, CONV_WIDTH), bf16),
        ],
        scratch_shapes=[
            pltpu.VMEM((TS + 2 * HALO, POOL_WIDTH), f32),
            pltpu.VMEM((TS + 2 * HALO, CONV_WIDTH), f32),
            pltpu.VMEM((SUBLANES, TS + 2 * HALO - SUBLANES, CONV_WIDTH), f32),
        ],
        compiler_params=_cparams(("arbitrary",)),
        name="seq_mixers",
    )(z, z, z, z, z, z, pool_w, pool_scale.reshape(DEPTH, 1, POOL_WIDTH), conv_dw,
      conv_db.reshape(DEPTH, 1, CONV_WIDTH), conv_ln_g.reshape(DEPTH, 1, CONV_WIDTH),
      conv_ln_b.reshape(DEPTH, 1, CONV_WIDTH))


def _rope(x, cos_t, sin_t):
    lane = lax.broadcasted_iota(jnp.int32, x.shape, 1) & (2 * ROPE_FREQS - 1)
    partner = jnp.where(lane < ROPE_FREQS, pltpu.roll(x, HEAD_DIM - ROPE_FREQS, 1),
                        pltpu.roll(x, ROPE_FREQS, 1))
    return x * cos_t + partner * sin_t


def _attn_kernel(sink_ref, q_ref, kc_ref, kp_ref, kn_ref, vc_ref, vp_ref, vn_ref, kx_ref, vx_ref,
                 cq_ref, sq_ref, cp_ref, sp_ref, cn_ref, sn_ref, o_ref, *, layer):
    hk = pl.program_id(0)
    i = pl.program_id(1)
    cq = cq_ref[...]
    sq = sq_ref[...]
    qs = []
    for j in range(GQA):
        qj = q_ref[:, j * HEAD_DIM:(j + 1) * HEAD_DIM].astype(f32)
        qs.append(_rope(qj, cq, sq).astype(bf16))
    q = jnp.concatenate(qs, axis=0)
    k = jnp.concatenate([
        _rope(kp_ref[...].astype(f32), cp_ref[...], sp_ref[...]).astype(bf16),
        _rope(kc_ref[...].astype(f32), cq, sq).astype(bf16),
        _rope(kn_ref[...].astype(f32), cn_ref[...], sn_ref[...]).astype(bf16)], axis=0)
    v = jnp.concatenate([vp_ref[...], vc_ref[...], vn_ref[...]], axis=0)
    kw = k.shape[0]

    nt = (((1,), (1,)), ((), ()))
    s_loc = lax.dot_general(q, k, nt, preferred_element_type=f32) * ATTN_SCALE
    s_ctx = lax.dot_general(q, kx_ref[...], nt, preferred_element_type=f32) * ATTN_SCALE

    qrow = lax.broadcasted_iota(jnp.int32, (GQA * TS, 1), 0) & (TS - 1)
    krel = lax.broadcasted_iota(jnp.int32, (1, kw), 1) - WINDOW
    kpos = i * TS + krel
    k_ok = jnp.logical_and(jnp.logical_and(kpos >= 0, kpos < S), i < NS_LAT)
    valid = jnp.logical_and(jnp.abs(krel - qrow) <= WINDOW, k_ok)
    s_loc = jnp.where(valid, s_loc, NEG_INF)

    sink = jnp.concatenate(
        [jnp.full((TS, 1), sink_ref[layer, hk * GQA + j], f32) for j in range(GQA)], axis=0)
    mx = jnp.maximum(jnp.maximum(jnp.max(s_loc, axis=-1, keepdims=True),
                                 jnp.max(s_ctx, axis=-1, keepdims=True)), sink)
    p_loc = jnp.exp(s_loc - mx)
    p_ctx = jnp.exp(s_ctx - mx)
    denom = (jnp.sum(p_loc, axis=-1, keepdims=True) + jnp.sum(p_ctx, axis=-1, keepdims=True)
             + jnp.exp(sink - mx))
    o = (_dot(p_loc.astype(bf16), v) + _dot(p_ctx.astype(bf16), vx_ref[...])) / denom
    for j in range(GQA):
        o_ref[:, j * HEAD_DIM:(j + 1) * HEAD_DIM] = o[j * TS:(j + 1) * TS, :].astype(bf16)


def _attention(z, attn_sink, cos_t, sin_t, layer):
    qcol = Q_OFF // (GQA * HEAD_DIM)
    kcol = K_OFF // HEAD_DIM
    vcol = V_OFF // HEAD_DIM
    hb = TS // WINDOW
    last = T // WINDOW - 1
    ctx_blk = S // TS

    def prev(i):
        return jnp.maximum(i * hb - 1, 0)

    def nxt(i):
        return jnp.minimum((i + 1) * hb, last)

    def kv_specs(col):
        return [
            pl.BlockSpec((TS, HEAD_DIM), lambda h, i: (i, col + h)),
            pl.BlockSpec((WINDOW, HEAD_DIM), lambda h, i: (prev(i), col + h)),
            pl.BlockSpec((WINDOW, HEAD_DIM), lambda h, i: (nxt(i), col + h)),
        ]

    tab_specs = [
        pl.BlockSpec((TS, HEAD_DIM), lambda h, i: (i, 0)),
        pl.BlockSpec((TS, HEAD_DIM), lambda h, i: (i, 0)),
        pl.BlockSpec((WINDOW, HEAD_DIM), lambda h, i: (prev(i), 0)),
        pl.BlockSpec((WINDOW, HEAD_DIM), lambda h, i: (prev(i), 0)),
        pl.BlockSpec((WINDOW, HEAD_DIM), lambda h, i: (nxt(i), 0)),
        pl.BlockSpec((WINDOW, HEAD_DIM), lambda h, i: (nxt(i), 0)),
    ]
    return pl.pallas_call(
        functools.partial(_attn_kernel, layer=layer),
        grid=(N_KV_HEADS, NS),
        in_specs=[pl.BlockSpec(memory_space=pltpu.SMEM),
                  pl.BlockSpec((TS, GQA * HEAD_DIM), lambda h, i: (i, qcol + h))]
                 + kv_specs(kcol) + kv_specs(vcol)
                 + [pl.BlockSpec((TS, HEAD_DIM), lambda h, i: (ctx_blk, kcol + h)),
                    pl.BlockSpec((TS, HEAD_DIM), lambda h, i: (ctx_blk, vcol + h))]
                 + tab_specs,
        out_specs=pl.BlockSpec((TS, GQA * HEAD_DIM), lambda h, i: (i, h)),
        out_shape=jax.ShapeDtypeStruct((T, N_Q_HEADS * HEAD_DIM), bf16),
        compiler_params=_cparams(("arbitrary", "arbitrary")),
        name="attention",
    )(attn_sink, z, z, z, z, z, z, z, z, z, cos_t, sin_t, cos_t, sin_t, cos_t, sin_t)


NB_MERGE = D // TN_MERGE


def _merge_kernel(h_ref, p_ref, a_ref, c_ref, zg0, zg1, zg2, bg0, bg1, bg2, wp, wa, wc, wo,
                  ng_ref, g1_ref, o_ref, merged_ref):
    m = pl.program_id(0)
    n = pl.program_id(1)

    def gate(zg, bg):
        return _sigmoid(zg[...].astype(f32) + bg[...])

    merged = (gate(zg0, bg0) * _dot(p_ref[...], wp[...])
              + gate(zg1, bg1) * _dot(a_ref[...], wa[...])
              + gate(zg2, bg2) * _dot(c_ref[...], wc[...]))
    merged_ref[n] = merged.astype(bf16)

    @pl.when(n == NB_MERGE - 1)
    def _():
        lhs = jnp.concatenate([merged_ref[k] for k in range(NB_MERGE)], axis=1)
        o_ref[...] = _dot(lhs, wo[...])
        _residual_tile(h_ref, o_ref, ng_ref[1:2, :], g1_ref, m)


def _merge(h, z, pool_o, attn_o, conv_o, b_gate, w_pool_up, w_attn_up, w_conv_up, w_out,
           norm_g, mod, layer):
    nb = NB_MERGE
    bg = b_gate.reshape(DEPTH, 1, GATE_WIDTH)
    zgate = lambda b: pl.BlockSpec((TM, TN_MERGE), lambda m, n: (m, b * nb + n))
    bgate = lambda b: pl.BlockSpec((None, 1, TN_MERGE), lambda m, n: (layer, 0, b * nb + n))
    wup = lambda k: pl.BlockSpec((None, k, TN_MERGE), lambda m, n: (layer, 0, n))
    return pl.pallas_call(
        _merge_kernel,
        grid=(NM, nb),
        in_specs=[
            pl.BlockSpec((TM, D), lambda m, n: (m, 0), pipeline_mode=pl.Buffered(1)),
            pl.BlockSpec((TM, POOL_WIDTH), lambda m, n: (m, 0)),
            pl.BlockSpec((TM, N_Q_HEADS * HEAD_DIM), lambda m, n: (m, 0)),
            pl.BlockSpec((TM, CONV_WIDTH), lambda m, n: (m, 0)),
            zgate(0), zgate(1), zgate(2), bgate(0), bgate(1), bgate(2),
            wup(POOL_WIDTH), wup(N_Q_HEADS * HEAD_DIM), wup(CONV_WIDTH),
            pl.BlockSpec((None, D, D), lambda m, n: (layer, 0, 0), pipeline_mode=pl.Buffered(1)),
            pl.BlockSpec((None, 4, D), lambda m, n: (layer, 0, 0)),
            pl.BlockSpec((None, MOD_ROWS, D), lambda m, n: (layer, 0, 2)),
        ],
        out_specs=pl.BlockSpec((TM, D), lambda m, n: (m, 0)),
        out_shape=jax.ShapeDtypeStruct((T, D), f32),
        scratch_shapes=[pltpu.VMEM((nb, TM, TN_MERGE), bf16)],
        compiler_params=_cparams(("arbitrary", "arbitrary")),
        name="merge",
    )(h, pool_o, attn_o, conv_o, z, z, z, bg, bg, bg, w_pool_up, w_attn_up, w_conv_up, w_out,
      norm_g, mod)


def _ffn_prologue(h_ref, ng_ref, sh_ref, sc_ref, u_ref, m):
    def store(rows, u):
        u_ref[rows, :] = u.astype(bf16)
    _norm_mod_tile(h_ref, ng_ref[2:3, :], sh_ref, sc_ref, m, store)


def _ffn_epilogue(h_ref, ng_ref, g2_ref, o_ref, m):
    _residual_tile(h_ref, o_ref, ng_ref[3:4, :], g2_ref, m)


NF = FFN // TF


def _skewed_pipeline(step, n_chunks, up, down, act_ref, o_ref):
    @pl.when(step == 0)
    def _():
        act_ref[...] = up()

    @pl.when(step == 1)
    def _():
        part = down()
        act = up()
        o_ref[...] = part
        act_ref[...] = act

    @pl.when(jnp.logical_and(step > 1, step < n_chunks))
    def _():
        part = down()
        act = up()
        o_ref[...] += part
        act_ref[...] = act

    @pl.when(step == n_chunks)
    def _():
        o_ref[...] += down()


def _up_chunk(step, n_chunks):
    return jnp.minimum(step, n_chunks - 1)


def _down_chunk(step, n_chunks):
    del n_chunks
    return jnp.maximum(step - 1, 0)


def _swiglu_pipeline(f, u_ref, w1, w3, w2, act_ref, o_ref):
    def up():
        u = u_ref[...]
        act = _silu(_dot(u, w1[...].astype(bf16))) * _dot(u, w3[...].astype(bf16))
        return act.astype(bf16)

    def down():
        return _dot(act_ref[...], w2[...].astype(bf16))

    _skewed_pipeline(f, NF, up, down, act_ref, o_ref)


def _ffn_kernel(h_ref, ng_ref, sh_ref, sc_ref, g2_ref, w1, w3, w2, o_ref, u_ref, act_ref):
    m = pl.program_id(0)
    f = pl.program_id(1)

    @pl.when(f == 0)
    def _():
        _ffn_prologue(h_ref, ng_ref, sh_ref, sc_ref, u_ref, m)

    _swiglu_pipeline(f, u_ref, w1, w3, w2, act_ref, o_ref)

    @pl.when(f == NF)
    def _():
        _ffn_epilogue(h_ref, ng_ref, g2_ref, o_ref, m)


def _mod_specs(layer, chunks):
    specs = [pl.BlockSpec((TM, D), lambda m, *r: (m, 0), pipeline_mode=pl.Buffered(1)),
             pl.BlockSpec((None, 4, D), lambda m, *r: (layer, 0, 0))]
    for ch in chunks:
        specs.append(pl.BlockSpec((None, MOD_ROWS, D), lambda m, *r, ch=ch: (layer, 0, ch)))
    return specs


def _dense_ffn(h, norm_g, mod, w1, w3, w2, layer):
    j = layer // 2
    return pl.pallas_call(
        _ffn_kernel,
        grid=(NM, NF + 1),
        in_specs=_mod_specs(layer, (3, 4, 5)) + [
            pl.BlockSpec((None, D, TF), lambda m, f: (j, 0, _up_chunk(f, NF))),
            pl.BlockSpec((None, D, TF), lambda m, f: (j, 0, _up_chunk(f, NF))),
            pl.BlockSpec((None, TF, D), lambda m, f: (j, _down_chunk(f, NF), 0)),
        ],
        out_specs=pl.BlockSpec((TM, D), lambda m, f: (m, 0)),
        out_shape=jax.ShapeDtypeStruct((T, D), f32),
        scratch_shapes=[pltpu.VMEM((TM, D), bf16), pltpu.VMEM((TM, TF), bf16)],
        compiler_params=_cparams(("arbitrary", "arbitrary")),
        name="dense_ffn",
    )(h, norm_g, mod, mod, mod, w1, w3, w2)


def _split_bf16(x):
    hi = x.astype(bf16)
    lo = (x - hi.astype(f32)).astype(bf16)
    return hi, lo


R_E0, R_E1, R_G0, R_G1, R_R0, R_R1 = range(6)


def _router_kernel(h_ref, ng_ref, sh_ref, sc_ref, rw_ref, route_ref, cnt_ref, u_ref, base_ref, lg_ref):
    m = pl.program_id(0)

    @pl.when(m == 0)
    def _():
        base_ref[...] = jnp.zeros_like(base_ref)

    def store(rows, u):
        u_ref[rows, :] = u

    _norm_mod_tile(h_ref, ng_ref[2:3, :], sh_ref, sc_ref, m, store)

    wh, wl = _split_bf16(rw_ref[...])
    for half in range(2):
        sl = pl.ds(half * (TM // 2), TM // 2)
        uh, ul = _split_bf16(u_ref[sl, :])
        lg_ref[sl, :] = _dot(uh, wh) + (_dot(uh, wl) + _dot(ul, wh))

    ri = lax.broadcasted_iota(jnp.int32, (ROW_CHUNK, ROW_CHUNK), 0)
    ci = lax.broadcasted_iota(jnp.int32, (ROW_CHUNK, ROW_CHUNK), 1)
    earlier = jnp.where(ri > ci, 1.0, 0.0).astype(bf16)

    def chunk(rows, off):
        del off
        logits = lg_ref[rows, :]
        lane = lax.broadcasted_iota(jnp.int32, logits.shape, 1)
        lg = jnp.where(lane < N_EXPERTS, logits, -jnp.inf)
        m1 = jnp.max(lg, axis=-1, keepdims=True)
        i1 = jnp.min(jnp.where(lg == m1, lane, ROUTER_PAD), axis=-1, keepdims=True)
        lg2 = jnp.where(lane == i1, -jnp.inf, lg)
        m2 = jnp.max(lg2, axis=-1, keepdims=True)
        i2 = jnp.min(jnp.where(lg2 == m2, lane, ROUTER_PAD), axis=-1, keepdims=True)
        e2 = jnp.exp(m2 - m1)
        den = 1.0 + e2
        hit1 = lane == i1
        hit2 = lane == i2
        onehot = jnp.where(jnp.logical_or(hit1, hit2), 1.0, 0.0)
        before = _dot(earlier, onehot.astype(bf16)) + base_ref[...]
        r1 = jnp.sum(jnp.where(hit1, before, 0.0), axis=-1, keepdims=True)
        r2 = jnp.sum(jnp.where(hit2, before, 0.0), axis=-1, keepdims=True)
        base_ref[...] = base_ref[...] + jnp.sum(onehot, axis=0, keepdims=True)
        vals = (i1.astype(f32), i2.astype(f32), 1.0 / den, e2 / den, r1, r2)
        route = jnp.zeros(logits.shape, f32)
        for col, val in enumerate(vals):
            route = jnp.where(lane == col, val, route)
        route_ref[rows, :] = route

    _for_row_chunks(chunk)
    cnt_ref[...] = jnp.broadcast_to(base_ref[...], cnt_ref.shape)


def _router(h, norm_g, mod, router_w_pad, layer):
    j = layer // 2
    return pl.pallas_call(
        _router_kernel,
        grid=(NM,),
        in_specs=_mod_specs(layer, (3, 4)) + [
            pl.BlockSpec((None, D, ROUTER_PAD), lambda m: (j, 0, 0)),
        ],
        out_specs=[
            pl.BlockSpec((TM, ROUTER_PAD), lambda m: (m, 0)),
            pl.BlockSpec((8, ROUTER_PAD), lambda m: (0, 0)),
            pl.BlockSpec((TM, D), lambda m: (m, 0)),
        ],
        out_shape=[
            jax.ShapeDtypeStruct((T, ROUTER_PAD), f32),
            jax.ShapeDtypeStruct((8, ROUTER_PAD), f32),
            jax.ShapeDtypeStruct((T, D), f32),
        ],
        scratch_shapes=[pltpu.VMEM((1, ROUTER_PAD), f32), pltpu.VMEM((TM, ROUTER_PAD), f32)],
        compiler_params=_cparams(("arbitrary",)),
        name="router",
    )(h, norm_g, mod, mod, router_w_pad)


def _dispatch_plan(route, counts):
    expert = route[:, R_E0:R_E1 + 1].astype(jnp.int32)
    rank = route[:, R_R0:R_R1 + 1].astype(jnp.int32)
    cnt = counts[0, :N_EXPERTS].astype(jnp.int32)
    tiles = (cnt + TG - 1) // TG
    tile_end = jnp.cumsum(tiles)
    start = (tile_end - tiles) * TG
    dest = (start[expert] + rank).reshape(2 * T)
    n_used = tile_end[-1]
    tile_id = jnp.minimum(jnp.arange(NT, dtype=jnp.int32), n_used - 1)
    tile_expert = jnp.sum(tile_id[:, None] >= tile_end[None, :], axis=1).astype(jnp.int32)
    rows_used = cnt[tile_expert] - (tile_id - (tile_end - tiles)[tile_expert]) * TG
    tile_mode = jnp.where(jnp.arange(NT, dtype=jnp.int32) >= n_used, TILE_UNUSED,
                          jnp.where(rows_used <= TG // 2, TILE_HALF, TILE_FULL)).astype(jnp.int32)
    return dest, tile_expert, tile_mode


def _row_copy(src_ref, src_row, dst_ref, dst_row, sem):
    return pltpu.make_async_copy(src_ref.at[pl.ds(src_row, 1)], dst_ref.at[pl.ds(dst_row, 1)], sem)


def _dispatch_kernel(dest_ref, u_ref, xs_in_ref, xs_ref, sem):
    del xs_in_ref
    m = pl.program_id(0)

    def issue(r, carry):
        t = m * TM + r
        _row_copy(u_ref, r, xs_ref, dest_ref[2 * t], sem).start()
        _row_copy(u_ref, r, xs_ref, dest_ref[2 * t + 1], sem).start()
        return carry

    lax.fori_loop(0, TM, issue, 0, unroll=8)
    for _ in range(2):
        pltpu.make_async_copy(u_ref, xs_ref.at[pl.ds(0, TM)], sem).wait()


def _dispatch(dest, u):
    xs0 = jnp.zeros((NT * TG, D), f32)
    return pl.pallas_call(
        _dispatch_kernel,
        grid_spec=pltpu.PrefetchScalarGridSpec(
            num_scalar_prefetch=1,
            grid=(NM,),
            in_specs=[pl.BlockSpec((TM, D), lambda m, d: (m, 0)), pl.BlockSpec(memory_space=pl.ANY)],
            out_specs=pl.BlockSpec(memory_space=pl.ANY),
            scratch_shapes=[pltpu.SemaphoreType.DMA(())],
        ),
        out_shape=jax.ShapeDtypeStruct((NT * TG, D), f32),
        input_output_aliases={2: 0},
        compiler_params=_cparams(("arbitrary",)),
        name="moe_dispatch",
    )(dest, u, xs0)


TILE_UNUSED, TILE_HALF, TILE_FULL = 0, 1, 2


def _group_kernel(te_ref, tm_ref, xs_ref, w1, w3, w2, ys_ref, u_ref, act_ref):
    del te_ref
    j = pl.program_id(0)
    f = pl.program_id(1)
    mode = tm_ref[j]

    @pl.when(jnp.logical_and(mode != TILE_FULL, f == 0))
    def _():
        ys_ref[...] = jnp.zeros_like(ys_ref)

    def run(rows):
        sl = pl.ds(0, rows)

        @pl.when(f == 0)
        def _():
            u_ref[sl, :] = xs_ref[sl, :].astype(bf16)

        _swiglu_pipeline(f, u_ref.at[sl], w1, w3, w2, act_ref.at[sl], ys_ref.at[sl])

    @pl.when(mode == TILE_FULL)
    def _():
        run(TG)

    @pl.when(mode == TILE_HALF)
    def _():
        run(TG // 2)


def _grouped_ffn(tile_expert, tile_mode, xs, w1, w3, w2, layer):
    jl = layer // 2

    def chunk(j, f, tmode, which):
        return jnp.where(tmode[j] != TILE_UNUSED, which(f, NF), NF - 1)

    return pl.pallas_call(
        _group_kernel,
        grid_spec=pltpu.PrefetchScalarGridSpec(
            num_scalar_prefetch=2,
            grid=(NT, NF + 1),
            in_specs=[
                pl.BlockSpec((TG, D), lambda j, f, te, tu: (j, 0)),
                pl.BlockSpec((None, None, D, TF),
                             lambda j, f, te, tu: (jl, te[j], 0, chunk(j, f, tu, _up_chunk))),
                pl.BlockSpec((None, None, D, TF),
                             lambda j, f, te, tu: (jl, te[j], 0, chunk(j, f, tu, _up_chunk))),
                pl.BlockSpec((None, None, TF, D),
                             lambda j, f, te, tu: (jl, te[j], chunk(j, f, tu, _down_chunk), 0)),
            ],
            out_specs=pl.BlockSpec((TG, D), lambda j, f, te, tu: (j, 0)),
            scratch_shapes=[pltpu.VMEM((TG, D), bf16), pltpu.VMEM((TG, TF), bf16)],
        ),
        out_shape=jax.ShapeDtypeStruct((NT * TG, D), f32),
        compiler_params=_cparams(("arbitrary", "arbitrary")),
        name="moe_grouped_ffn",
    )(tile_expert, tile_mode, xs, w1, w3, w2)


def _combine_kernel(dest_ref, h_ref, route_ref, ng_ref, g2_ref, ys_ref, o_ref, ybuf, sem):
    m = pl.program_id(0)

    def issue(r, carry):
        t = m * TM + r
        _row_copy(ys_ref, dest_ref[2 * t], ybuf.at[0], r, sem).start()
        _row_copy(ys_ref, dest_ref[2 * t + 1], ybuf.at[1], r, sem).start()
        return carry

    lax.fori_loop(0, TM, issue, 0, unroll=8)
    for k in range(2):
        pltpu.make_async_copy(ys_ref.at[pl.ds(0, TM)], ybuf.at[k], sem).wait()

    def chunk(rows, off):
        route = route_ref[rows, :]
        y = route[:, R_G0:R_G0 + 1] * ybuf[0, rows, :] + route[:, R_G1:R_G1 + 1] * ybuf[1, rows, :]
        o_ref[rows, :] = h_ref[rows, :] + _mod_row(g2_ref, m * TM + off) * _rms(y, ng_ref[3:4, :])

    _for_norm_chunks(chunk)


def _combine(dest, h, route, norm_g, mod, ys, layer):
    return pl.pallas_call(
        _combine_kernel,
        grid_spec=pltpu.PrefetchScalarGridSpec(
            num_scalar_prefetch=1,
            grid=(NM,),
            in_specs=[
                pl.BlockSpec((TM, D), lambda m, d: (m, 0), pipeline_mode=pl.Buffered(1)),
                pl.BlockSpec((TM, ROUTER_PAD), lambda m, d: (m, 0)),
                pl.BlockSpec((None, 4, D), lambda m, d: (layer, 0, 0)),
                pl.BlockSpec((None, MOD_ROWS, D), lambda m, d: (layer, 0, 5)),
                pl.BlockSpec(memory_space=pl.ANY),
            ],
            out_specs=pl.BlockSpec((TM, D), lambda m, d: (m, 0)),
            scratch_shapes=[pltpu.VMEM((2, TM, D), f32), pltpu.SemaphoreType.DMA(())],
        ),
        out_shape=jax.ShapeDtypeStruct((T, D), f32),
        compiler_params=_cparams(("arbitrary",)),
        name="moe_combine",
    )(dest, h, route, norm_g, mod, ys)


def _moe_ffn(h, norm_g, mod, router_w_pad, w1, w3, w2, layer):
    route, counts, u = _router(h, norm_g, mod, router_w_pad, layer)
    dest, tile_expert, tile_mode = _dispatch_plan(route, counts)
    xs = _dispatch(dest, u)
    ys = _grouped_ffn(tile_expert, tile_mode, xs, w1, w3, w2, layer)
    return _combine(dest, h, route, norm_g, mod, ys, layer)


def _rope_tables():
    t = jnp.arange(S)
    row = (t // GRID_W).astype(f32)
    col = (t % GRID_W).astype(f32)
    inv = ROPE_BASE ** (-jnp.arange(ROPE_FREQS, dtype=f32) / ROPE_FREQS)
    a0 = row[:, None] * inv
    a1 = col[:, None] * inv
    cos_t = jnp.concatenate([jnp.cos(a0), jnp.cos(a0), jnp.cos(a1), jnp.cos(a1)], axis=1)
    sin_t = jnp.concatenate([-jnp.sin(a0), jnp.sin(a0), -jnp.sin(a1), jnp.sin(a1)], axis=1)
    cos_t = jnp.concatenate([cos_t, jnp.ones((C, HEAD_DIM), f32)], axis=0)
    sin_t = jnp.concatenate([sin_t, jnp.zeros((C, HEAD_DIM), f32)], axis=0)
    return cos_t, sin_t


def kernel(x, c, ctx, c_ctx, w_mod, b_mod, norm_g, w_in, b_gate, pool_w, pool_scale, w_pool_up,
           attn_sink, w_attn_up, conv_dw, conv_db, conv_ln_g, conv_ln_b, w_conv_up, w_out,
           ffn_w1, ffn_w3, ffn_w2, router_w, moe_w1, moe_w3, moe_w2):
    assert x.shape == (1, S, D) and ctx.shape == (1, C, D)
    cos_t, sin_t = _rope_tables()
    cc = jnp.concatenate([c, c_ctx[None, :], jnp.zeros((MOD_ROWS - 2, D), f32)], axis=0)
    mod = _modulation(cc, w_mod, b_mod)
    router_w_pad = jnp.pad(router_w, ((0, 0), (0, 0), (0, ROUTER_PAD - N_EXPERTS)))
    w_pool_up, w_attn_up, w_conv_up, w_out = (
        w.astype(bf16) for w in (w_pool_up, w_attn_up, w_conv_up, w_out))
    h = jnp.concatenate([x[0], ctx[0]], axis=0)
    for i in range(DEPTH):
        z = _in_proj(h, norm_g, mod, w_in, i)
        pool_o, conv_o = _seq_mixers(z, pool_w, pool_scale, conv_dw, conv_db, conv_ln_g, conv_ln_b, i)
        attn_o = _attention(z, attn_sink, cos_t, sin_t, i)
        h = _merge(h, z, pool_o, attn_o, conv_o, b_gate, w_pool_up, w_attn_up, w_conv_up, w_out,
                   norm_g, mod, i)
        if i % 2 == 0:
            h = _dense_ffn(h, norm_g, mod, ffn_w1, ffn_w3, ffn_w2, i)
        else:
            h = _moe_ffn(h, norm_g, mod, router_w_pad, moe_w1, moe_w3, moe_w2, i)
    return h[:S][None]
```

```python
import functools

import jax
import jax.numpy as jnp
from jax import lax
from jax.experimental import pallas as pl
from jax.experimental.pallas import tpu as pltpu

f32 = jnp.float32
bf16 = jnp.bfloat16

D = 2048
S = 8192
C = 256
T = S + C
DEPTH = 4
GRID_W = 64
HEAD_DIM = 128
POOL_WIDTH = 512
POOL_WINDOWS = (2, 4, 8, 16)
POOL_GROUP = 128
N_Q_HEADS = 8
N_KV_HEADS = 2
GQA = 4
WINDOW = 128
ATTN_SCALE = HEAD_DIM ** -0.5
ROPE_BASE = 10000.0
ROPE_FREQS = 32
CONV_WIDTH = 512
CONV_K = 31
GATE_WIDTH = 3 * D
POOL_OFF = GATE_WIDTH
Q_OFF = POOL_OFF + POOL_WIDTH
K_OFF = Q_OFF + N_Q_HEADS * HEAD_DIM
V_OFF = K_OFF + N_KV_HEADS * HEAD_DIM
CONV_OFF = V_OFF + N_KV_HEADS * HEAD_DIM
IN_WIDTH = CONV_OFF + 2 * CONV_WIDTH
FFN = 5632
N_EXPERTS = 8
N_MOD = 6
EPS = 1e-6
NEG_INF = -1e30

VMEM_LIMIT_BYTES = 56 * 1024 * 1024

TM = 1056
NM = T // TM
TS = 256
NS = T // TS
NS_LAT = S // TS
HALO = 16
SUBLANES = 8
TN_IN = 1024
TN_MERGE = 512
TF = 256
TN_MOD = 1024
ROUTER_PAD = 128
ROW_CHUNK = 96
NORM_CHUNK = 32
NORM_UNROLL = 3
assert S % NORM_CHUNK == 0 and TM % (NORM_CHUNK * NORM_UNROLL) == 0
TG = 1024
NT = (2 * T + N_EXPERTS * (TG - 1) + TG - 1) // TG
MOD_ROWS = 16


def _cparams(semantics):
    return pltpu.CompilerParams(dimension_semantics=semantics,
                                vmem_limit_bytes=VMEM_LIMIT_BYTES)


def _sigmoid(x):
    return 1.0 / (1.0 + jnp.exp(-x))


def _silu(x):
    return x * _sigmoid(x)


def _mod_row(mod_ref, row0):
    return mod_ref[pl.ds(jnp.where(row0 >= S, 1, 0), 1), :]


def _rms(x, g):
    return x * lax.rsqrt(jnp.mean(x * x, axis=-1, keepdims=True) + EPS) * g


def _for_row_chunks(fn):
    def step(r, carry):
        off = pl.multiple_of(r * ROW_CHUNK, ROW_CHUNK)
        fn(pl.ds(off, ROW_CHUNK), off)
        return carry
    lax.fori_loop(0, TM // ROW_CHUNK, step, 0, unroll=True)


def _for_norm_chunks(fn):
    def step(r, carry):
        off = pl.multiple_of(r * NORM_CHUNK, NORM_CHUNK)
        fn(pl.ds(off, NORM_CHUNK), off)
        return carry
    lax.fori_loop(0, TM // NORM_CHUNK, step, 0, unroll=NORM_UNROLL)


def _norm_mod_tile(h_ref, g, sh_ref, sc_ref, m, store):
    def chunk(rows, off):
        row0 = m * TM + off
        u = _rms(h_ref[rows, :], g) * (1.0 + _mod_row(sc_ref, row0)) + _mod_row(sh_ref, row0)
        store(rows, u)
    _for_norm_chunks(chunk)


def _residual_tile(h_ref, y_ref, g, gate_ref, m):
    def chunk(rows, off):
        y_ref[rows, :] = h_ref[rows, :] + _mod_row(gate_ref, m * TM + off) * _rms(y_ref[rows, :], g)
    _for_norm_chunks(chunk)


def _dot(a, b):
    return jnp.dot(a, b, preferred_element_type=f32)


def _mod_kernel(cc_ref, w_ref, b_ref, o_ref):
    s = _silu(cc_ref[...])
    o_ref[...] = _dot(s.astype(bf16), w_ref[...].astype(bf16)) + b_ref[...]


def _modulation(cc, w_mod, b_mod):
    nn = (N_MOD * D) // TN_MOD
    return pl.pallas_call(
        _mod_kernel,
        grid=(DEPTH, nn),
        in_specs=[
            pl.BlockSpec((MOD_ROWS, D), lambda l, n: (0, 0)),
            pl.BlockSpec((None, D, TN_MOD), lambda l, n: (l, 0, n)),
            pl.BlockSpec((None, 1, TN_MOD), lambda l, n: (l, 0, n)),
        ],
        out_specs=pl.BlockSpec((None, MOD_ROWS, TN_MOD), lambda l, n: (l, 0, n)),
        out_shape=jax.ShapeDtypeStruct((DEPTH, MOD_ROWS, N_MOD * D), f32),
        compiler_params=_cparams(("arbitrary", "arbitrary")),
        name="modulation",
    )(cc, w_mod, b_mod.reshape(DEPTH, 1, N_MOD * D))


def _win_kernel(h_ref, ng_ref, sh_ref, sc_ref, w_ref, z_ref, u_ref):
    m = pl.program_id(0)

    @pl.when(pl.program_id(1) == 0)
    def _():
        def store(rows, u):
            u_ref[rows, :] = u.astype(bf16)
        _norm_mod_tile(h_ref, ng_ref[0:1, :], sh_ref, sc_ref, m, store)

    z_ref[...] = _dot(u_ref[...], w_ref[...].astype(bf16)).astype(bf16)


def _in_proj(h, norm_g, mod, w_in, layer):
    return pl.pallas_call(
        _win_kernel,
        grid=(NM, IN_WIDTH // TN_IN),
        in_specs=[
            pl.BlockSpec((TM, D), lambda m, n: (m, 0)),
            pl.BlockSpec((None, 4, D), lambda m, n: (layer, 0, 0)),
            pl.BlockSpec((None, MOD_ROWS, D), lambda m, n: (layer, 0, 0)),
            pl.BlockSpec((None, MOD_ROWS, D), lambda m, n: (layer, 0, 1)),
            pl.BlockSpec((None, D, TN_IN), lambda m, n: (layer, 0, n)),
        ],
        out_specs=pl.BlockSpec((TM, TN_IN), lambda m, n: (m, n)),
        out_shape=jax.ShapeDtypeStruct((T, IN_WIDTH), bf16),
        scratch_shapes=[pltpu.VMEM((TM, D), bf16)],
        compiler_params=_cparams(("arbitrary", "arbitrary")),
        name="in_proj",
    )(h, norm_g, mod, mod, w_in)


def _seqmix_kernel(zp_c, zp_p, zp_n, zc_c, zc_p, zc_n, pw_ref, ps_ref, dw_ref, db_ref,
                   lng_ref, lnb_ref, po_ref, co_ref, xe, ge, gs):
    i = pl.program_id(0)
    has_prev = jnp.where(jnp.logical_and(i != 0, i != NS_LAT), 1.0, 0.0).astype(f32)
    has_next = jnp.where(i < NS_LAT - 1, 1.0, 0.0).astype(f32)
    is_lat = i < NS_LAT
    pos = jnp.where(is_lat, i * TS, 0) + lax.broadcasted_iota(jnp.int32, (TS, 1), 0)
    seq_len = jnp.where(is_lat, S, C)

    xe[0:HALO, :] = zp_p[...].astype(f32) * has_prev
    xe[HALO:HALO + TS, :] = zp_c[...].astype(f32)
    xe[HALO + TS:, :] = zp_n[...].astype(f32) * has_next
    outs = []
    for g, w in enumerate(POOL_WINDOWS):
        cols = slice(g * POOL_GROUP, (g + 1) * POOL_GROUP)
        acc = xe[pl.ds(HALO - w // 2, TS), cols]
        for d in range(-w // 2 + 1, w // 2):
            acc = acc + xe[pl.ds(HALO + d, TS), cols]
        lo = jnp.clip(pos - w // 2, 0, seq_len)
        hi = jnp.clip(pos - w // 2 + w, 0, seq_len)
        diff = acc / (hi - lo).astype(f32) - xe[HALO:HALO + TS, cols]
        outs.append(_dot(diff.astype(bf16), pw_ref[g].astype(bf16)))
    po_ref[...] = (jnp.concatenate(outs, axis=1) * ps_ref[...]).astype(bf16)

    def glu(ref):
        zz = ref[...].astype(f32)
        return zz[:, :CONV_WIDTH] * _sigmoid(zz[:, CONV_WIDTH:])

    ge[0:HALO, :] = glu(zc_p) * has_prev
    ge[HALO:HALO + TS, :] = glu(zc_c)
    ge[HALO + TS:, :] = glu(zc_n) * has_next
    span = TS + 2 * HALO - SUBLANES
    for s in range(SUBLANES):
        gs[s] = ge[pl.ds(s, span), :]
    base = HALO - CONV_K // 2
    acc = None
    for k in range(CONV_K):
        shift, start = (base + k) % SUBLANES, (base + k) // SUBLANES * SUBLANES
        term = gs[shift, pl.ds(start, TS), :] * dw_ref[k:k + 1, :]
        acc = term if acc is None else acc + term
    y = acc + db_ref[...]
    yc = y - jnp.mean(y, axis=-1, keepdims=True)
    yn = yc * lax.rsqrt(jnp.mean(yc * yc, axis=-1, keepdims=True) + EPS) * lng_ref[...] + lnb_ref[...]
    co_ref[...] = _silu(yn).astype(bf16)


def _seq_mixers(z, pool_w, pool_scale, conv_dw, conv_db, conv_ln_g, conv_ln_b, layer):
    rb = TS // HALO
    last = T // HALO - 1
    pcol = POOL_OFF // POOL_WIDTH
    ccol = CONV_OFF // (2 * CONV_WIDTH)

    def prev(i):
        return jnp.maximum(i * rb - 1, 0)

    def nxt(i):
        return jnp.minimum((i + 1) * rb, last)

    vec = lambda width: pl.BlockSpec((None, 1, width), lambda i: (layer, 0, 0))
    return pl.pallas_call(
        _seqmix_kernel,
        grid=(NS,),
        in_specs=[
            pl.BlockSpec((TS, POOL_WIDTH), lambda i: (i, pcol)),
            pl.BlockSpec((HALO, POOL_WIDTH), lambda i: (prev(i), pcol)),
            pl.BlockSpec((HALO, POOL_WIDTH), lambda i: (nxt(i), pcol)),
            pl.BlockSpec((TS, 2 * CONV_WIDTH), lambda i: (i, ccol)),
            pl.BlockSpec((HALO, 2 * CONV_WIDTH), lambda i: (prev(i), ccol)),
            pl.BlockSpec((HALO, 2 * CONV_WIDTH), lambda i: (nxt(i), ccol)),
            pl.BlockSpec((None, 4, POOL_GROUP, POOL_GROUP), lambda i: (layer, 0, 0, 0)),
            vec(POOL_WIDTH),
            pl.BlockSpec((None, CONV_K, CONV_WIDTH), lambda i: (layer, 0, 0)),
            vec(CONV_WIDTH), vec(CONV_WIDTH), vec(CONV_WIDTH),
        ],
        out_specs=[
            pl.BlockSpec((TS, POOL_WIDTH), lambda i: (i, 0)),
            pl.BlockSpec((TS, CONV_WIDTH), lambda i: (i, 0)),
        ],
        out_shape=[
            jax.ShapeDtypeStruct((T, POOL_WIDTH), bf16),
            jax.ShapeDtypeStruct((T, CONV_WIDTH), bf16),
        ],
        scratch_shapes=[
            pltpu.VMEM((TS + 2 * HALO, POOL_WIDTH), f32),
            pltpu.VMEM((TS + 2 * HALO, CONV_WIDTH), f32),
            pltpu.VMEM((SUBLANES, TS + 2 * HALO - SUBLANES, CONV_WIDTH), f32),
        ],
        compiler_params=_cparams(("arbitrary",)),
        name="seq_mixers",
    )(z, z, z, z, z, z, pool_w, pool_scale.reshape(DEPTH, 1, POOL_WIDTH), conv_dw,
      conv_db.reshape(DEPTH, 1, CONV_WIDTH), conv_ln_g.reshape(DEPTH, 1, CONV_WIDTH),
      conv_ln_b.reshape(DEPTH, 1, CONV_WIDTH))


def _rope(x, cos_t, sin_t):
    lane = lax.broadcasted_iota(jnp.int32, x.shape, 1) & (2 * ROPE_FREQS - 1)
    partner = jnp.where(lane < ROPE_FREQS, pltpu.roll(x, HEAD_DIM - ROPE_FREQS, 1),
                        pltpu.roll(x, ROPE_FREQS, 1))
    return x * cos_t + partner * sin_t


def _attn_kernel(sink_ref, q_ref, kc_ref, kp_ref, kn_ref, vc_ref, vp_ref, vn_ref, kx_ref, vx_ref,
                 cq_ref, sq_ref, cp_ref, sp_ref, cn_ref, sn_ref, o_ref, *, layer):
    hk = pl.program_id(0)
    i = pl.program_id(1)
    cq = cq_ref[...]
    sq = sq_ref[...]
    qs = []
    for j in range(GQA):
        qj = q_ref[:, j * HEAD_DIM:(j + 1) * HEAD_DIM].astype(f32)
        qs.append(_rope(qj, cq, sq).astype(bf16))
    q = jnp.concatenate(qs, axis=0)
    k = jnp.concatenate([
        _rope(kp_ref[...].astype(f32), cp_ref[...], sp_ref[...]).astype(bf16),
        _rope(kc_ref[...].astype(f32), cq, sq).astype(bf16),
        _rope(kn_ref[...].astype(f32), cn_ref[...], sn_ref[...]).astype(bf16)], axis=0)
    v = jnp.concatenate([vp_ref[...], vc_ref[...], vn_ref[...]], axis=0)
    kw = k.shape[0]

    nt = (((1,), (1,)), ((), ()))
    s_loc = lax.dot_general(q, k, nt, preferred_element_type=f32) * ATTN_SCALE
    s_ctx = lax.dot_general(q, kx_ref[...], nt, preferred_element_type=f32) * ATTN_SCALE

    qrow = lax.broadcasted_iota(jnp.int32, (GQA * TS, 1), 0) & (TS - 1)
    krel = lax.broadcasted_iota(jnp.int32, (1, kw), 1) - WINDOW
    kpos = i * TS + krel
    k_ok = jnp.logical_and(jnp.logical_and(kpos >= 0, kpos < S), i < NS_LAT)
    valid = jnp.logical_and(jnp.abs(krel - qrow) <= WINDOW, k_ok)
    s_loc = jnp.where(valid, s_loc, NEG_INF)

    sink = jnp.concatenate(
        [jnp.full((TS, 1), sink_ref[layer, hk * GQA + j], f32) for j in range(GQA)], axis=0)
    mx = jnp.maximum(jnp.maximum(jnp.max(s_loc, axis=-1, keepdims=True),
                                 jnp.max(s_ctx, axis=-1, keepdims=True)), sink)
    p_loc = jnp.exp(s_loc - mx)
    p_ctx = jnp.exp(s_ctx - mx)
    denom = (jnp.sum(p_loc, axis=-1, keepdims=True) + jnp.sum(p_ctx, axis=-1, keepdims=True)
             + jnp.exp(sink - mx))
    o = (_dot(p_loc.astype(bf16), v) + _dot(p_ctx.astype(bf16), vx_ref[...])) / denom
    for j in range(GQA):
        o_ref[:, j * HEAD_DIM:(j + 1) * HEAD_DIM] = o[j * TS:(j + 1) * TS, :].astype(bf16)


def _attention(z, attn_sink, cos_t, sin_t, layer):
    qcol = Q_OFF // (GQA * HEAD_DIM)
    kcol = K_OFF // HEAD_DIM
    vcol = V_OFF // HEAD_DIM
    hb = TS // WINDOW
    last = T // WINDOW - 1
    ctx_blk = S // TS

    def prev(i):
        return jnp.maximum(i * hb - 1, 0)

    def nxt(i):
        return jnp.minimum((i + 1) * hb, last)

    def kv_specs(col):
        return [
            pl.BlockSpec((TS, HEAD_DIM), lambda h, i: (i, col + h)),
            pl.BlockSpec((WINDOW, HEAD_DIM), lambda h, i: (prev(i), col + h)),
            pl.BlockSpec((WINDOW, HEAD_DIM), lambda h, i: (nxt(i), col + h)),
        ]

    tab_specs = [
        pl.BlockSpec((TS, HEAD_DIM), lambda h, i: (i, 0)),
        pl.BlockSpec((TS, HEAD_DIM), lambda h, i: (i, 0)),
        pl.BlockSpec((WINDOW, HEAD_DIM), lambda h, i: (prev(i), 0)),
        pl.BlockSpec((WINDOW, HEAD_DIM), lambda h, i: (prev(i), 0)),
        pl.BlockSpec((WINDOW, HEAD_DIM), lambda h, i: (nxt(i), 0)),
        pl.BlockSpec((WINDOW, HEAD_DIM), lambda h, i: (nxt(i), 0)),
    ]
    return pl.pallas_call(
        functools.partial(_attn_kernel, layer=layer),
        grid=(N_KV_HEADS, NS),
        in_specs=[pl.BlockSpec(memory_space=pltpu.SMEM),
                  pl.BlockSpec((TS, GQA * HEAD_DIM), lambda h, i: (i, qcol + h))]
                 + kv_specs(kcol) + kv_specs(vcol)
                 + [pl.BlockSpec((TS, HEAD_DIM), lambda h, i: (ctx_blk, kcol + h)),
                    pl.BlockSpec((TS, HEAD_DIM), lambda h, i: (ctx_blk, vcol + h))]
                 + tab_specs,
        out_specs=pl.BlockSpec((TS, GQA * HEAD_DIM), lambda h, i: (i, h)),
        out_shape=jax.ShapeDtypeStruct((T, N_Q_HEADS * HEAD_DIM), bf16),
        compiler_params=_cparams(("arbitrary", "arbitrary")),
        name="attention",
    )(attn_sink, z, z, z, z, z, z, z, z, z, cos_t, sin_t, cos_t, sin_t, cos_t, sin_t)


NB_MERGE = D // TN_MERGE


def _merge_kernel(h_ref, p_ref, a_ref, c_ref, zg0, zg1, zg2, bg0, bg1, bg2, wp, wa, wc, wo,
                  ng_ref, g1_ref, o_ref, act_ref):
    m = pl.program_id(0)
    n = pl.program_id(1)

    def gate(zg, bg):
        return _sigmoid(zg[...].astype(f32) + bg[...])

    def up():
        merged = (gate(zg0, bg0) * _dot(p_ref[...], wp[...].astype(bf16))
                  + gate(zg1, bg1) * _dot(a_ref[...], wa[...].astype(bf16))
                  + gate(zg2, bg2) * _dot(c_ref[...], wc[...].astype(bf16)))
        return merged.astype(bf16)

    def down():
        return _dot(act_ref[...], wo[...].astype(bf16))

    _skewed_pipeline(n, NB_MERGE, up, down, act_ref, o_ref)

    @pl.when(n == NB_MERGE)
    def _():
        _residual_tile(h_ref, o_ref, ng_ref[1:2, :], g1_ref, m)


def _merge(h, z, pool_o, attn_o, conv_o, b_gate, w_pool_up, w_attn_up, w_conv_up, w_out,
           norm_g, mod, layer):
    nb = NB_MERGE
    bg = b_gate.reshape(DEPTH, 1, GATE_WIDTH)
    upc = lambda n: _up_chunk(n, nb)
    zgate = lambda b: pl.BlockSpec((TM, TN_MERGE), lambda m, n: (m, b * nb + upc(n)))
    bgate = lambda b: pl.BlockSpec((None, 1, TN_MERGE), lambda m, n: (layer, 0, b * nb + upc(n)))
    wup = lambda k: pl.BlockSpec((None, k, TN_MERGE), lambda m, n: (layer, 0, upc(n)))
    return pl.pallas_call(
        _merge_kernel,
        grid=(NM, nb + 1),
        in_specs=[
            pl.BlockSpec((TM, D), lambda m, n: (m, 0), pipeline_mode=pl.Buffered(1)),
            pl.BlockSpec((TM, POOL_WIDTH), lambda m, n: (m, 0), pipeline_mode=pl.Buffered(1)),
            pl.BlockSpec((TM, N_Q_HEADS * HEAD_DIM), lambda m, n: (m, 0), pipeline_mode=pl.Buffered(1)),
            pl.BlockSpec((TM, CONV_WIDTH), lambda m, n: (m, 0), pipeline_mode=pl.Buffered(1)),
            zgate(0), zgate(1), zgate(2), bgate(0), bgate(1), bgate(2),
            wup(POOL_WIDTH), wup(N_Q_HEADS * HEAD_DIM), wup(CONV_WIDTH),
            pl.BlockSpec((None, TN_MERGE, D), lambda m, n: (layer, _down_chunk(n, nb), 0)),
            pl.BlockSpec((None, 4, D), lambda m, n: (layer, 0, 0)),
            pl.BlockSpec((None, MOD_ROWS, D), lambda m, n: (layer, 0, 2)),
        ],
        out_specs=pl.BlockSpec((TM, D), lambda m, n: (m, 0), pipeline_mode=pl.Buffered(1)),
        out_shape=jax.ShapeDtypeStruct((T, D), f32),
        scratch_shapes=[pltpu.VMEM((TM, TN_MERGE), bf16)],
        compiler_params=_cparams(("arbitrary", "arbitrary")),
        name="merge",
    )(h, pool_o, attn_o, conv_o, z, z, z, bg, bg, bg, w_pool_up, w_attn_up, w_conv_up, w_out,
      norm_g, mod)


def _ffn_prologue(h_ref, ng_ref, sh_ref, sc_ref, u_ref, m):
    def store(rows, u):
        u_ref[rows, :] = u.astype(bf16)
    _norm_mod_tile(h_ref, ng_ref[2:3, :], sh_ref, sc_ref, m, store)


def _ffn_epilogue(h_ref, ng_ref, g2_ref, o_ref, m):
    _residual_tile(h_ref, o_ref, ng_ref[3:4, :], g2_ref, m)


NF = FFN // TF


def _skewed_pipeline(step, n_chunks, up, down, act_ref, o_ref):
    @pl.when(step == 0)
    def _():
        act_ref[...] = up()

    @pl.when(step == 1)
    def _():
        part = down()
        act = up()
        o_ref[...] = part
        act_ref[...] = act

    @pl.when(jnp.logical_and(step > 1, step < n_chunks))
    def _():
        part = down()
        act = up()
        o_ref[...] += part
        act_ref[...] = act

    @pl.when(step == n_chunks)
    def _():
        o_ref[...] += down()


def _up_chunk(step, n_chunks):
    return jnp.minimum(step, n_chunks - 1)


def _down_chunk(step, n_chunks):
    del n_chunks
    return jnp.maximum(step - 1, 0)


def _swiglu_pipeline(f, u_ref, w1, w3, w2, act_ref, o_ref):
    def up():
        u = u_ref[...]
        act = _silu(_dot(u, w1[...].astype(bf16))) * _dot(u, w3[...].astype(bf16))
        return act.astype(bf16)

    def down():
        return _dot(act_ref[...], w2[...].astype(bf16))

    _skewed_pipeline(f, NF, up, down, act_ref, o_ref)


def _ffn_kernel(h_ref, ng_ref, sh_ref, sc_ref, g2_ref, w1, w3, w2, o_ref, u_ref, act_ref):
    m = pl.program_id(0)
    f = pl.program_id(1)

    @pl.when(f == 0)
    def _():
        _ffn_prologue(h_ref, ng_ref, sh_ref, sc_ref, u_ref, m)

    _swiglu_pipeline(f, u_ref, w1, w3, w2, act_ref, o_ref)

    @pl.when(f == NF)
    def _():
        _ffn_epilogue(h_ref, ng_ref, g2_ref, o_ref, m)


def _mod_specs(layer, chunks):
    specs = [pl.BlockSpec((TM, D), lambda m, *r: (m, 0), pipeline_mode=pl.Buffered(1)),
             pl.BlockSpec((None, 4, D), lambda m, *r: (layer, 0, 0))]
    for ch in chunks:
        specs.append(pl.BlockSpec((None, MOD_ROWS, D), lambda m, *r, ch=ch: (layer, 0, ch)))
    return specs


def _dense_ffn(h, norm_g, mod, w1, w3, w2, layer):
    j = layer // 2
    return pl.pallas_call(
        _ffn_kernel,
        grid=(NM, NF + 1),
        in_specs=_mod_specs(layer, (3, 4, 5)) + [
            pl.BlockSpec((None, D, TF), lambda m, f: (j, 0, _up_chunk(f, NF))),
            pl.BlockSpec((None, D, TF), lambda m, f: (j, 0, _up_chunk(f, NF))),
            pl.BlockSpec((None, TF, D), lambda m, f: (j, _down_chunk(f, NF), 0)),
        ],
        out_specs=pl.BlockSpec((TM, D), lambda m, f: (m, 0)),
        out_shape=jax.ShapeDtypeStruct((T, D), f32),
        scratch_shapes=[pltpu.VMEM((TM, D), bf16), pltpu.VMEM((TM, TF), bf16)],
        compiler_params=_cparams(("arbitrary", "arbitrary")),
        name="dense_ffn",
    )(h, norm_g, mod, mod, mod, w1, w3, w2)


def _split_bf16(x):
    hi = x.astype(bf16)
    lo = (x - hi.astype(f32)).astype(bf16)
    return hi, lo


R_E0, R_E1, R_G0, R_G1, R_R0, R_R1 = range(6)


def _router_kernel(h_ref, ng_ref, sh_ref, sc_ref, rw_ref, route_ref, cnt_ref, u_ref, base_ref, lg_ref):
    m = pl.program_id(0)

    @pl.when(m == 0)
    def _():
        base_ref[...] = jnp.zeros_like(base_ref)

    def store(rows, u):
        u_ref[rows, :] = u

    _norm_mod_tile(h_ref, ng_ref[2:3, :], sh_ref, sc_ref, m, store)

    wh, wl = _split_bf16(rw_ref[...])
    for half in range(2):
        sl = pl.ds(half * (TM // 2), TM // 2)
        uh, ul = _split_bf16(u_ref[sl, :])
        lg_ref[sl, :] = _dot(uh, wh) + (_dot(uh, wl) + _dot(ul, wh))

    ri = lax.broadcasted_iota(jnp.int32, (ROW_CHUNK, ROW_CHUNK), 0)
    ci = lax.broadcasted_iota(jnp.int32, (ROW_CHUNK, ROW_CHUNK), 1)
    earlier = jnp.where(ri > ci, 1.0, 0.0).astype(bf16)

    def chunk(rows, off):
        del off
        logits = lg_ref[rows, :]
        lane = lax.broadcasted_iota(jnp.int32, logits.shape, 1)
        lg = jnp.where(lane < N_EXPERTS, logits, -jnp.inf)
        m1 = jnp.max(lg, axis=-1, keepdims=True)
        i1 = jnp.min(jnp.where(lg == m1, lane, ROUTER_PAD), axis=-1, keepdims=True)
        lg2 = jnp.where(lane == i1, -jnp.inf, lg)
        m2 = jnp.max(lg2, axis=-1, keepdims=True)
        i2 = jnp.min(jnp.where(lg2 == m2, lane, ROUTER_PAD), axis=-1, keepdims=True)
        e2 = jnp.exp(m2 - m1)
        den = 1.0 + e2
        hit1 = lane == i1
        hit2 = lane == i2
        onehot = jnp.where(jnp.logical_or(hit1, hit2), 1.0, 0.0)
        before = _dot(earlier, onehot.astype(bf16)) + base_ref[...]
        r1 = jnp.sum(jnp.where(hit1, before, 0.0), axis=-1, keepdims=True)
        r2 = jnp.sum(jnp.where(hit2, before, 0.0), axis=-1, keepdims=True)
        base_ref[...] = base_ref[...] + jnp.sum(onehot, axis=0, keepdims=True)
        vals = (i1.astype(f32), i2.astype(f32), 1.0 / den, e2 / den, r1, r2)
        route = jnp.zeros(logits.shape, f32)
        for col, val in enumerate(vals):
            route = jnp.where(lane == col, val, route)
        route_ref[rows, :] = route

    _for_row_chunks(chunk)
    cnt_ref[...] = jnp.broadcast_to(base_ref[...], cnt_ref.shape)


def _router(h, norm_g, mod, router_w_pad, layer):
    j = layer // 2
    return pl.pallas_call(
        _router_kernel,
        grid=(NM,),
        in_specs=_mod_specs(layer, (3, 4)) + [
            pl.BlockSpec((None, D, ROUTER_PAD), lambda m: (j, 0, 0)),
        ],
        out_specs=[
            pl.BlockSpec((TM, ROUTER_PAD), lambda m: (m, 0)),
            pl.BlockSpec((8, ROUTER_PAD), lambda m: (0, 0)),
            pl.BlockSpec((TM, D), lambda m: (m, 0)),
        ],
        out_shape=[
            jax.ShapeDtypeStruct((T, ROUTER_PAD), f32),
            jax.ShapeDtypeStruct((8, ROUTER_PAD), f32),
            jax.ShapeDtypeStruct((T, D), f32),
        ],
        scratch_shapes=[pltpu.VMEM((1, ROUTER_PAD), f32), pltpu.VMEM((TM, ROUTER_PAD), f32)],
        compiler_params=_cparams(("arbitrary",)),
        name="router",
    )(h, norm_g, mod, mod, router_w_pad)


def _dispatch_plan(route, counts):
    expert = route[:, R_E0:R_E1 + 1].astype(jnp.int32)
    rank = route[:, R_R0:R_R1 + 1].astype(jnp.int32)
    cnt = counts[0, :N_EXPERTS].astype(jnp.int32)
    tiles = (cnt + TG - 1) // TG
    tile_end = jnp.cumsum(tiles)
    start = (tile_end - tiles) * TG
    dest = (start[expert] + rank).reshape(2 * T)
    n_used = tile_end[-1]
    tile_id = jnp.minimum(jnp.arange(NT, dtype=jnp.int32), n_used - 1)
    tile_expert = jnp.sum(tile_id[:, None] >= tile_end[None, :], axis=1).astype(jnp.int32)
    rows_used = cnt[tile_expert] - (tile_id - (tile_end - tiles)[tile_expert]) * TG
    tile_mode = jnp.where(jnp.arange(NT, dtype=jnp.int32) >= n_used, TILE_UNUSED,
                          jnp.where(rows_used <= TG // 2, TILE_HALF, TILE_FULL)).astype(jnp.int32)
    return dest, tile_expert, tile_mode


def _row_copy(src_ref, src_row, dst_ref, dst_row, sem):
    return pltpu.make_async_copy(src_ref.at[pl.ds(src_row, 1)], dst_ref.at[pl.ds(dst_row, 1)], sem)


def _dispatch_kernel(dest_ref, u_ref, xs_in_ref, xs_ref, sem):
    del xs_in_ref
    m = pl.program_id(0)

    def issue(r, carry):
        t = m * TM + r
        _row_copy(u_ref, r, xs_ref, dest_ref[2 * t], sem).start(priority=0)
        _row_copy(u_ref, r, xs_ref, dest_ref[2 * t + 1], sem).start(priority=1)
        return carry

    lax.fori_loop(0, TM, issue, 0, unroll=8)
    for _ in range(2):
        pltpu.make_async_copy(u_ref, xs_ref.at[pl.ds(0, TM)], sem).wait()


def _dispatch(dest, u):
    xs0 = jnp.zeros((NT * TG, D), f32)
    return pl.pallas_call(
        _dispatch_kernel,
        grid_spec=pltpu.PrefetchScalarGridSpec(
            num_scalar_prefetch=1,
            grid=(NM,),
            in_specs=[pl.BlockSpec((TM, D), lambda m, d: (m, 0)), pl.BlockSpec(memory_space=pl.ANY)],
            out_specs=pl.BlockSpec(memory_space=pl.ANY),
            scratch_shapes=[pltpu.SemaphoreType.DMA(())],
        ),
        out_shape=jax.ShapeDtypeStruct((NT * TG, D), f32),
        input_output_aliases={2: 0},
        compiler_params=_cparams(("arbitrary",)),
        name="moe_dispatch",
    )(dest, u, xs0)


TILE_UNUSED, TILE_HALF, TILE_FULL = 0, 1, 2


def _group_kernel(te_ref, tm_ref, xs_ref, w1, w3, w2, ys_ref, u_ref, act_ref):
    del te_ref
    j = pl.program_id(0)
    f = pl.program_id(1)
    mode = tm_ref[j]

    @pl.when(jnp.logical_and(mode != TILE_FULL, f == 0))
    def _():
        ys_ref[...] = jnp.zeros_like(ys_ref)

    def run(rows):
        sl = pl.ds(0, rows)

        @pl.when(f == 0)
        def _():
            u_ref[sl, :] = xs_ref[sl, :].astype(bf16)

        _swiglu_pipeline(f, u_ref.at[sl], w1, w3, w2, act_ref.at[sl], ys_ref.at[sl])

    @pl.when(mode == TILE_FULL)
    def _():
        run(TG)

    @pl.when(mode == TILE_HALF)
    def _():
        run(TG // 2)


def _grouped_ffn(tile_expert, tile_mode, xs, w1, w3, w2, layer):
    jl = layer // 2

    def chunk(j, f, tmode, which):
        return jnp.where(tmode[j] != TILE_UNUSED, which(f, NF), NF - 1)

    return pl.pallas_call(
        _group_kernel,
        grid_spec=pltpu.PrefetchScalarGridSpec(
            num_scalar_prefetch=2,
            grid=(NT, NF + 1),
            in_specs=[
                pl.BlockSpec((TG, D), lambda j, f, te, tu: (j, 0)),
                pl.BlockSpec((None, None, D, TF),
                             lambda j, f, te, tu: (jl, te[j], 0, chunk(j, f, tu, _up_chunk))),
                pl.BlockSpec((None, None, D, TF),
                             lambda j, f, te, tu: (jl, te[j], 0, chunk(j, f, tu, _up_chunk))),
                pl.BlockSpec((None, None, TF, D),
                             lambda j, f, te, tu: (jl, te[j], chunk(j, f, tu, _down_chunk), 0)),
            ],
            out_specs=pl.BlockSpec((TG, D), lambda j, f, te, tu: (j, 0)),
            scratch_shapes=[pltpu.VMEM((TG, D), bf16), pltpu.VMEM((TG, TF), bf16)],
        ),
        out_shape=jax.ShapeDtypeStruct((NT * TG, D), f32),
        compiler_params=_cparams(("arbitrary", "arbitrary")),
        name="moe_grouped_ffn",
    )(tile_expert, tile_mode, xs, w1, w3, w2)


def _combine_kernel(dest_ref, h_ref, route_ref, ng_ref, g2_ref, ys_ref, o_ref, ybuf, sem):
    m = pl.program_id(0)

    def issue(r, carry):
        t = m * TM + r
        _row_copy(ys_ref, dest_ref[2 * t], ybuf.at[0], r, sem).start(priority=0)
        _row_copy(ys_ref, dest_ref[2 * t + 1], ybuf.at[1], r, sem).start(priority=1)
        return carry

    lax.fori_loop(0, TM, issue, 0, unroll=8)
    for k in range(2):
        pltpu.make_async_copy(ys_ref.at[pl.ds(0, TM)], ybuf.at[k], sem).wait()

    def chunk(rows, off):
        route = route_ref[rows, :]
        y = route[:, R_G0:R_G0 + 1] * ybuf[0, rows, :] + route[:, R_G1:R_G1 + 1] * ybuf[1, rows, :]
        o_ref[rows, :] = h_ref[rows, :] + _mod_row(g2_ref, m * TM + off) * _rms(y, ng_ref[3:4, :])

    _for_norm_chunks(chunk)


def _combine(dest, h, route, norm_g, mod, ys, layer):
    return pl.pallas_call(
        _combine_kernel,
        grid_spec=pltpu.PrefetchScalarGridSpec(
            num_scalar_prefetch=1,
            grid=(NM,),
            in_specs=[
                pl.BlockSpec((TM, D), lambda m, d: (m, 0), pipeline_mode=pl.Buffered(1)),
                pl.BlockSpec((TM, ROUTER_PAD), lambda m, d: (m, 0)),
                pl.BlockSpec((None, 4, D), lambda m, d: (layer, 0, 0)),
                pl.BlockSpec((None, MOD_ROWS, D), lambda m, d: (layer, 0, 5)),
                pl.BlockSpec(memory_space=pl.ANY),
            ],
            out_specs=pl.BlockSpec((TM, D), lambda m, d: (m, 0)),
            scratch_shapes=[pltpu.VMEM((2, TM, D), f32), pltpu.SemaphoreType.DMA(())],
        ),
        out_shape=jax.ShapeDtypeStruct((T, D), f32),
        compiler_params=_cparams(("arbitrary",)),
        name="moe_combine",
    )(dest, h, route, norm_g, mod, ys)


def _moe_ffn(h, norm_g, mod, router_w_pad, w1, w3, w2, layer):
    route, counts, u = _router(h, norm_g, mod, router_w_pad, layer)
    dest, tile_expert, tile_mode = _dispatch_plan(route, counts)
    xs = _dispatch(dest, u)
    ys = _grouped_ffn(tile_expert, tile_mode, xs, w1, w3, w2, layer)
    return _combine(dest, h, route, norm_g, mod, ys, layer)


def _rope_tables():
    t = jnp.arange(S)
    row = (t // GRID_W).astype(f32)
    col = (t % GRID_W).astype(f32)
    inv = ROPE_BASE ** (-jnp.arange(ROPE_FREQS, dtype=f32) / ROPE_FREQS)
    a0 = row[:, None] * inv
    a1 = col[:, None] * inv
    cos_t = jnp.concatenate([jnp.cos(a0), jnp.cos(a0), jnp.cos(a1), jnp.cos(a1)], axis=1)
    sin_t = jnp.concatenate([-jnp.sin(a0), jnp.sin(a0), -jnp.sin(a1), jnp.sin(a1)], axis=1)
    cos_t = jnp.concatenate([cos_t, jnp.ones((C, HEAD_DIM), f32)], axis=0)
    sin_t = jnp.concatenate([sin_t, jnp.zeros((C, HEAD_DIM), f32)], axis=0)
    return cos_t, sin_t


def kernel(x, c, ctx, c_ctx, w_mod, b_mod, norm_g, w_in, b_gate, pool_w, pool_scale, w_pool_up,
           attn_sink, w_attn_up, conv_dw, conv_db, conv_ln_g, conv_ln_b, w_conv_up, w_out,
           ffn_w1, ffn_w3, ffn_w2, router_w, moe_w1, moe_w3, moe_w2):
    assert x.shape == (1, S, D) and ctx.shape == (1, C, D)
    cos_t, sin_t = _rope_tables()
    cc = jnp.concatenate([c, c_ctx[None, :], jnp.zeros((MOD_ROWS - 2, D), f32)], axis=0)
    mod = _modulation(cc, w_mod, b_mod)
    router_w_pad = jnp.pad(router_w, ((0, 0), (0, 0), (0, ROUTER_PAD - N_EXPERTS)))
    h = jnp.concatenate([x[0], ctx[0]], axis=0)
    for i in range(DEPTH):
        z = _in_proj(h, norm_g, mod, w_in, i)
        pool_o, conv_o = _seq_mixers(z, pool_w, pool_scale, conv_dw, conv_db, conv_ln_g, conv_ln_b, i)
        attn_o = _attention(z, attn_sink, cos_t, sin_t, i)
        h = _merge(h, z, pool_o, attn_o, conv_o, b_gate, w_pool_up, w_attn_up, w_conv_up, w_out,
                   norm_g, mod, i)
        if i % 2 == 0:
            h = _dense_ffn(h, norm_g, mod, ffn_w1, ffn_w3, ffn_w2, i)
        else:
            h = _moe_ffn(h, norm_g, mod, router_w_pad, moe_w1, moe_w3, moe_w2, i)
    return h[:S][None]
```

```python
import functools

import jax
import jax.numpy as jnp
from jax import lax
from jax.experimental import pallas as pl
from jax.experimental.pallas import tpu as pltpu

f32 = jnp.float32
bf16 = jnp.bfloat16

D = 2048
S = 8192
C = 256
T = S + C
DEPTH = 4
GRID_W = 64
HEAD_DIM = 128
POOL_WIDTH = 512
POOL_WINDOWS = (2, 4, 8, 16)
POOL_GROUP = 128
N_Q_HEADS = 8
N_KV_HEADS = 2
GQA = 4
WINDOW = 128
ATTN_SCALE = HEAD_DIM ** -0.5
ROPE_BASE = 10000.0
ROPE_FREQS = 32
CONV_WIDTH = 512
CONV_K = 31
GATE_WIDTH = 3 * D
POOL_OFF = GATE_WIDTH
Q_OFF = POOL_OFF + POOL_WIDTH
K_OFF = Q_OFF + N_Q_HEADS * HEAD_DIM
V_OFF = K_OFF + N_KV_HEADS * HEAD_DIM
CONV_OFF = V_OFF + N_KV_HEADS * HEAD_DIM
IN_WIDTH = CONV_OFF + 2 * CONV_WIDTH
FFN = 5632
N_EXPERTS = 8
N_MOD = 6
EPS = 1e-6
NEG_INF = -1e30

VMEM_LIMIT_BYTES = 56 * 1024 * 1024

TM = 1056
NM = T // TM
TS = 256
NS = T // TS
NS_LAT = S // TS
HALO = 16
SUBLANES = 8
TN_IN = 1024
TN_MERGE = 256
TF = 256
TN_MOD = 1024
ROUTER_PAD = 128
ROW_CHUNK = 96
NORM_CHUNK = 32
NORM_UNROLL = 3
assert S % NORM_CHUNK == 0 and TM % (NORM_CHUNK * NORM_UNROLL) == 0
TG = 1024
NT = (2 * T + N_EXPERTS * (TG - 1) + TG - 1) // TG
MOD_ROWS = 16


def _cparams(semantics):
    return pltpu.CompilerParams(dimension_semantics=semantics,
                                vmem_limit_bytes=VMEM_LIMIT_BYTES)


def _sigmoid(x):
    return 1.0 / (1.0 + jnp.exp(-x))


def _silu(x):
    return x * _sigmoid(x)


def _mod_row(mod_ref, row0):
    return mod_ref[pl.ds(jnp.where(row0 >= S, 1, 0), 1), :]


def _rms(x, g):
    return x * lax.rsqrt(jnp.mean(x * x, axis=-1, keepdims=True) + EPS) * g


def _for_row_chunks(fn):
    def step(r, carry):
        off = pl.multiple_of(r * ROW_CHUNK, ROW_CHUNK)
        fn(pl.ds(off, ROW_CHUNK), off)
        return carry
    lax.fori_loop(0, TM // ROW_CHUNK, step, 0, unroll=True)


def _for_norm_chunks(fn):
    def step(r, carry):
        off = pl.multiple_of(r * NORM_CHUNK, NORM_CHUNK)
        fn(pl.ds(off, NORM_CHUNK), off)
        return carry
    lax.fori_loop(0, TM // NORM_CHUNK, step, 0, unroll=NORM_UNROLL)


def _norm_mod_tile(h_ref, g, sh_ref, sc_ref, m, store):
    def chunk(rows, off):
        row0 = m * TM + off
        u = _rms(h_ref[rows, :], g) * (1.0 + _mod_row(sc_ref, row0)) + _mod_row(sh_ref, row0)
        store(rows, u)
    _for_norm_chunks(chunk)


def _residual_tile(h_ref, y_ref, o_ref, g, gate_ref, m):
    def chunk(rows, off):
        o_ref[rows, :] = h_ref[rows, :] + _mod_row(gate_ref, m * TM + off) * _rms(y_ref[rows, :], g)
    _for_norm_chunks(chunk)


def _dot(a, b):
    return jnp.dot(a, b, preferred_element_type=f32)


def _mod_kernel(cc_ref, w_ref, b_ref, o_ref):
    s = _silu(cc_ref[...])
    o_ref[...] = _dot(s.astype(bf16), w_ref[...].astype(bf16)) + b_ref[...]


def _modulation(cc, w_mod, b_mod):
    nn = (N_MOD * D) // TN_MOD
    return pl.pallas_call(
        _mod_kernel,
        grid=(DEPTH, nn),
        in_specs=[
            pl.BlockSpec((MOD_ROWS, D), lambda l, n: (0, 0)),
            pl.BlockSpec((None, D, TN_MOD), lambda l, n: (l, 0, n)),
            pl.BlockSpec((None, 1, TN_MOD), lambda l, n: (l, 0, n)),
        ],
        out_specs=pl.BlockSpec((None, MOD_ROWS, TN_MOD), lambda l, n: (l, 0, n)),
        out_shape=jax.ShapeDtypeStruct((DEPTH, MOD_ROWS, N_MOD * D), f32),
        compiler_params=_cparams(("arbitrary", "arbitrary")),
        name="modulation",
    )(cc, w_mod, b_mod.reshape(DEPTH, 1, N_MOD * D))


def _win_kernel(h_ref, ng_ref, sh_ref, sc_ref, w_ref, z_ref, u_ref):
    m = pl.program_id(0)

    @pl.when(pl.program_id(1) == 0)
    def _():
        def store(rows, u):
            u_ref[rows, :] = u.astype(bf16)
        _norm_mod_tile(h_ref, ng_ref[0:1, :], sh_ref, sc_ref, m, store)

    z_ref[...] = _dot(u_ref[...], w_ref[...].astype(bf16)).astype(bf16)


def _in_proj(h, norm_g, mod, w_in, layer):
    return pl.pallas_call(
        _win_kernel,
        grid=(NM, IN_WIDTH // TN_IN),
        in_specs=[
            pl.BlockSpec((TM, D), lambda m, n: (m, 0)),
            pl.BlockSpec((None, 4, D), lambda m, n: (layer, 0, 0)),
            pl.BlockSpec((None, MOD_ROWS, D), lambda m, n: (layer, 0, 0)),
            pl.BlockSpec((None, MOD_ROWS, D), lambda m, n: (layer, 0, 1)),
            pl.BlockSpec((None, D, TN_IN), lambda m, n: (layer, 0, n)),
        ],
        out_specs=pl.BlockSpec((TM, TN_IN), lambda m, n: (m, n)),
        out_shape=jax.ShapeDtypeStruct((T, IN_WIDTH), bf16),
        scratch_shapes=[pltpu.VMEM((TM, D), bf16)],
        compiler_params=_cparams(("arbitrary", "arbitrary")),
        name="in_proj",
    )(h, norm_g, mod, mod, w_in)


def _seqmix_kernel(zp_c, zp_p, zp_n, zc_c, zc_p, zc_n, pw_ref, ps_ref, dw_ref, db_ref,
                   lng_ref, lnb_ref, po_ref, co_ref, xe, ge, gs):
    i = pl.program_id(0)
    has_prev = jnp.where(jnp.logical_and(i != 0, i != NS_LAT), 1.0, 0.0).astype(f32)
    has_next = jnp.where(i < NS_LAT - 1, 1.0, 0.0).astype(f32)
    is_lat = i < NS_LAT
    pos = jnp.where(is_lat, i * TS, 0) + lax.broadcasted_iota(jnp.int32, (TS, 1), 0)
    seq_len = jnp.where(is_lat, S, C)

    xe[0:HALO, :] = zp_p[...].astype(f32) * has_prev
    xe[HALO:HALO + TS, :] = zp_c[...].astype(f32)
    xe[HALO + TS:, :] = zp_n[...].astype(f32) * has_next
    outs = []
    for g, w in enumerate(POOL_WINDOWS):
        cols = slice(g * POOL_GROUP, (g + 1) * POOL_GROUP)
        acc = xe[pl.ds(HALO - w // 2, TS), cols]
        for d in range(-w // 2 + 1, w // 2):
            acc = acc + xe[pl.ds(HALO + d, TS), cols]
        lo = jnp.clip(pos - w // 2, 0, seq_len)
        hi = jnp.clip(pos - w // 2 + w, 0, seq_len)
        diff = acc / (hi - lo).astype(f32) - xe[HALO:HALO + TS, cols]
        outs.append(_dot(diff.astype(bf16), pw_ref[g].astype(bf16)))
    po_ref[...] = (jnp.concatenate(outs, axis=1) * ps_ref[...]).astype(bf16)

    def glu(ref):
        zz = ref[...].astype(f32)
        return zz[:, :CONV_WIDTH] * _sigmoid(zz[:, CONV_WIDTH:])

    ge[0:HALO, :] = glu(zc_p) * has_prev
    ge[HALO:HALO + TS, :] = glu(zc_c)
    ge[HALO + TS:, :] = glu(zc_n) * has_next
    span = TS + 2 * HALO - SUBLANES
    for s in range(SUBLANES):
        gs[s] = ge[pl.ds(s, span), :]
    base = HALO - CONV_K // 2
    acc = None
    for k in range(CONV_K):
        shift, start = (base + k) % SUBLANES, (base + k) // SUBLANES * SUBLANES
        term = gs[shift, pl.ds(start, TS), :] * dw_ref[k:k + 1, :]
        acc = term if acc is None else acc + term
    y = acc + db_ref[...]
    yc = y - jnp.mean(y, axis=-1, keepdims=True)
    yn = yc * lax.rsqrt(jnp.mean(yc * yc, axis=-1, keepdims=True) + EPS) * lng_ref[...] + lnb_ref[...]
    co_ref[...] = _silu(yn).astype(bf16)


def _seq_mixers(z, pool_w, pool_scale, conv_dw, conv_db, conv_ln_g, conv_ln_b, layer):
    rb = TS // HALO
    last = T // HALO - 1
    pcol = POOL_OFF // POOL_WIDTH
    ccol = CONV_OFF // (2 * CONV_WIDTH)

    def prev(i):
        return jnp.maximum(i * rb - 1, 0)

    def nxt(i):
        return jnp.minimum((i + 1) * rb, last)

    vec = lambda width: pl.BlockSpec((None, 1, width), lambda i: (layer, 0, 0))
    return pl.pallas_call(
        _seqmix_kernel,
        grid=(NS,),
        in_specs=[
            pl.BlockSpec((TS, POOL_WIDTH), lambda i: (i, pcol)),
            pl.BlockSpec((HALO, POOL_WIDTH), lambda i: (prev(i), pcol)),
            pl.BlockSpec((HALO, POOL_WIDTH), lambda i: (nxt(i), pcol)),
            pl.BlockSpec((TS, 2 * CONV_WIDTH), lambda i: (i, ccol)),
            pl.BlockSpec((HALO, 2 * CONV_WIDTH), lambda i: (prev(i), ccol)),
            pl.BlockSpec((HALO, 2 * CONV_WIDTH), lambda i: (nxt(i), ccol)),
            pl.BlockSpec((None, 4, POOL_GROUP, POOL_GROUP), lambda i: (layer, 0, 0, 0)),
            vec(POOL_WIDTH),
            pl.BlockSpec((None, CONV_K, CONV_WIDTH), lambda i: (layer, 0, 0)),
            vec(CONV_WIDTH), vec(CONV_WIDTH), vec(CONV_WIDTH),
        ],
        out_specs=[
            pl.BlockSpec((TS, POOL_WIDTH), lambda i: (i, 0)),
            pl.BlockSpec((TS, CONV_WIDTH), lambda i: (i, 0)),
        ],
        out_shape=[
            jax.ShapeDtypeStruct((T, POOL_WIDTH), bf16),
            jax.ShapeDtypeStruct((T, CONV_WIDTH), bf16),
        ],
        scratch_shapes=[
            pltpu.VMEM((TS + 2 * HALO, POOL_WIDTH), f32),
            pltpu.VMEM((TS + 2 * HALO, CONV_WIDTH), f32),
            pltpu.VMEM((SUBLANES, TS + 2 * HALO - SUBLANES, CONV_WIDTH), f32),
        ],
        compiler_params=_cparams(("arbitrary",)),
        name="seq_mixers",
    )(z, z, z, z, z, z, pool_w, pool_scale.reshape(DEPTH, 1, POOL_WIDTH), conv_dw,
      conv_db.reshape(DEPTH, 1, CONV_WIDTH), conv_ln_g.reshape(DEPTH, 1, CONV_WIDTH),
      conv_ln_b.reshape(DEPTH, 1, CONV_WIDTH))


def _rope(x, cos_t, sin_t):
    lane = lax.broadcasted_iota(jnp.int32, x.shape, 1) & (2 * ROPE_FREQS - 1)
    partner = jnp.where(lane < ROPE_FREQS, pltpu.roll(x, HEAD_DIM - ROPE_FREQS, 1),
                        pltpu.roll(x, ROPE_FREQS, 1))
    return x * cos_t + partner * sin_t


def _attn_kernel(sink_ref, q_ref, kc_ref, kp_ref, kn_ref, vc_ref, vp_ref, vn_ref, kx_ref, vx_ref,
                 cq_ref, sq_ref, cp_ref, sp_ref, cn_ref, sn_ref, o_ref, *, layer):
    hk = pl.program_id(0)
    i = pl.program_id(1)
    cq = cq_ref[...]
    sq = sq_ref[...]
    qs = []
    for j in range(GQA):
        qj = q_ref[:, j * HEAD_DIM:(j + 1) * HEAD_DIM].astype(f32)
        qs.append(_rope(qj, cq, sq).astype(bf16))
    q = jnp.concatenate(qs, axis=0)
    k = jnp.concatenate([
        _rope(kp_ref[...].astype(f32), cp_ref[...], sp_ref[...]).astype(bf16),
        _rope(kc_ref[...].astype(f32), cq, sq).astype(bf16),
        _rope(kn_ref[...].astype(f32), cn_ref[...], sn_ref[...]).astype(bf16)], axis=0)
    v = jnp.concatenate([vp_ref[...], vc_ref[...], vn_ref[...]], axis=0)
    kw = k.shape[0]

    nt = (((1,), (1,)), ((), ()))
    s_loc = lax.dot_general(q, k, nt, preferred_element_type=f32) * ATTN_SCALE
    s_ctx = lax.dot_general(q, kx_ref[...], nt, preferred_element_type=f32) * ATTN_SCALE

    qrow = lax.broadcasted_iota(jnp.int32, (GQA * TS, 1), 0) & (TS - 1)
    krel = lax.broadcasted_iota(jnp.int32, (1, kw), 1) - WINDOW
    kpos = i * TS + krel
    k_ok = jnp.logical_and(jnp.logical_and(kpos >= 0, kpos < S), i < NS_LAT)
    valid = jnp.logical_and(jnp.abs(krel - qrow) <= WINDOW, k_ok)
    s_loc = jnp.where(valid, s_loc, NEG_INF)

    sink = jnp.concatenate(
        [jnp.full((TS, 1), sink_ref[layer, hk * GQA + j], f32) for j in range(GQA)], axis=0)
    mx = jnp.maximum(jnp.maximum(jnp.max(s_loc, axis=-1, keepdims=True),
                                 jnp.max(s_ctx, axis=-1, keepdims=True)), sink)
    p_loc = jnp.exp(s_loc - mx)
    p_ctx = jnp.exp(s_ctx - mx)
    denom = (jnp.sum(p_loc, axis=-1, keepdims=True) + jnp.sum(p_ctx, axis=-1, keepdims=True)
             + jnp.exp(sink - mx))
    o = (_dot(p_loc.astype(bf16), v) + _dot(p_ctx.astype(bf16), vx_ref[...])) / denom
    for j in range(GQA):
        o_ref[:, j * HEAD_DIM:(j + 1) * HEAD_DIM] = o[j * TS:(j + 1) * TS, :].astype(bf16)


def _attention(z, attn_sink, cos_t, sin_t, layer):
    qcol = Q_OFF // (GQA * HEAD_DIM)
    kcol = K_OFF // HEAD_DIM
    vcol = V_OFF // HEAD_DIM
    hb = TS // WINDOW
    last = T // WINDOW - 1
    ctx_blk = S // TS

    def prev(i):
        return jnp.maximum(i * hb - 1, 0)

    def nxt(i):
        return jnp.minimum((i + 1) * hb, last)

    def kv_specs(col):
        return [
            pl.BlockSpec((TS, HEAD_DIM), lambda h, i: (i, col + h)),
            pl.BlockSpec((WINDOW, HEAD_DIM), lambda h, i: (prev(i), col + h)),
            pl.BlockSpec((WINDOW, HEAD_DIM), lambda h, i: (nxt(i), col + h)),
        ]

    tab_specs = [
        pl.BlockSpec((TS, HEAD_DIM), lambda h, i: (i, 0)),
        pl.BlockSpec((TS, HEAD_DIM), lambda h, i: (i, 0)),
        pl.BlockSpec((WINDOW, HEAD_DIM), lambda h, i: (prev(i), 0)),
        pl.BlockSpec((WINDOW, HEAD_DIM), lambda h, i: (prev(i), 0)),
        pl.BlockSpec((WINDOW, HEAD_DIM), lambda h, i: (nxt(i), 0)),
        pl.BlockSpec((WINDOW, HEAD_DIM), lambda h, i: (nxt(i), 0)),
    ]
    return pl.pallas_call(
        functools.partial(_attn_kernel, layer=layer),
        grid=(N_KV_HEADS, NS),
        in_specs=[pl.BlockSpec(memory_space=pltpu.SMEM),
                  pl.BlockSpec((TS, GQA * HEAD_DIM), lambda h, i: (i, qcol + h))]
                 + kv_specs(kcol) + kv_specs(vcol)
                 + [pl.BlockSpec((TS, HEAD_DIM), lambda h, i: (ctx_blk, kcol + h)),
                    pl.BlockSpec((TS, HEAD_DIM), lambda h, i: (ctx_blk, vcol + h))]
                 + tab_specs,
        out_specs=pl.BlockSpec((TS, GQA * HEAD_DIM), lambda h, i: (i, h)),
        out_shape=jax.ShapeDtypeStruct((T, N_Q_HEADS * HEAD_DIM), bf16),
        compiler_params=_cparams(("arbitrary", "arbitrary")),
        name="attention",
    )(attn_sink, z, z, z, z, z, z, z, z, z, cos_t, sin_t, cos_t, sin_t, cos_t, sin_t)


NB_MERGE = D // TN_MERGE


def _merge_kernel(h_ref, p_ref, a_ref, c_ref, zg0, zg1, zg2, bg0, bg1, bg2, wp, wa, wc, wo,
                  ng_ref, g1_ref, o_ref, act_ref):
    m = pl.program_id(0)
    n = pl.program_id(1)

    def gate(zg, bg):
        return _sigmoid(zg[...].astype(f32) + bg[...])

    def up():
        merged = (gate(zg0, bg0) * _dot(p_ref[...], wp[...].astype(bf16))
                  + gate(zg1, bg1) * _dot(a_ref[...], wa[...].astype(bf16))
                  + gate(zg2, bg2) * _dot(c_ref[...], wc[...].astype(bf16)))
        return merged.astype(bf16)

    def down():
        return _dot(act_ref[...], wo[...].astype(bf16))

    _skewed_pipeline(n, NB_MERGE, up, down, act_ref, o_ref)

    @pl.when(n == NB_MERGE)
    def _():
        _residual_tile(h_ref, o_ref, o_ref, ng_ref[1:2, :], g1_ref, m)


def _merge(h, z, pool_o, attn_o, conv_o, b_gate, w_pool_up, w_attn_up, w_conv_up, w_out,
           norm_g, mod, layer):
    nb = NB_MERGE
    bg = b_gate.reshape(DEPTH, 1, GATE_WIDTH)
    upc = lambda n: _up_chunk(n, nb)
    zgate = lambda b: pl.BlockSpec((TM, TN_MERGE), lambda m, n: (m, b * nb + upc(n)))
    bgate = lambda b: pl.BlockSpec((None, 1, TN_MERGE), lambda m, n: (layer, 0, b * nb + upc(n)))
    wup = lambda k: pl.BlockSpec((None, k, TN_MERGE), lambda m, n: (layer, 0, upc(n)))
    return pl.pallas_call(
        _merge_kernel,
        grid=(NM, nb + 1),
        in_specs=[
            pl.BlockSpec((TM, D), lambda m, n: (m, 0), pipeline_mode=pl.Buffered(1)),
            pl.BlockSpec((TM, POOL_WIDTH), lambda m, n: (m, 0)),
            pl.BlockSpec((TM, N_Q_HEADS * HEAD_DIM), lambda m, n: (m, 0)),
            pl.BlockSpec((TM, CONV_WIDTH), lambda m, n: (m, 0)),
            zgate(0), zgate(1), zgate(2), bgate(0), bgate(1), bgate(2),
            wup(POOL_WIDTH), wup(N_Q_HEADS * HEAD_DIM), wup(CONV_WIDTH),
            pl.BlockSpec((None, TN_MERGE, D), lambda m, n: (layer, _down_chunk(n, nb), 0)),
            pl.BlockSpec((None, 4, D), lambda m, n: (layer, 0, 0)),
            pl.BlockSpec((None, MOD_ROWS, D), lambda m, n: (layer, 0, 2)),
        ],
        out_specs=pl.BlockSpec((TM, D), lambda m, n: (m, 0)),
        out_shape=jax.ShapeDtypeStruct((T, D), f32),
        scratch_shapes=[pltpu.VMEM((TM, TN_MERGE), bf16)],
        compiler_params=_cparams(("arbitrary", "arbitrary")),
        name="merge",
    )(h, pool_o, attn_o, conv_o, z, z, z, bg, bg, bg, w_pool_up, w_attn_up, w_conv_up, w_out,
      norm_g, mod)


def _ffn_prologue(h_ref, ng_ref, sh_ref, sc_ref, u_ref, m):
    def store(rows, u):
        u_ref[rows, :] = u.astype(bf16)
    _norm_mod_tile(h_ref, ng_ref[2:3, :], sh_ref, sc_ref, m, store)


def _ffn_epilogue(h_ref, ng_ref, g2_ref, y_ref, o_ref, m):
    _residual_tile(h_ref, y_ref, o_ref, ng_ref[3:4, :], g2_ref, m)


NF = FFN // TF


def _skewed_pipeline(step, n_chunks, up, down, act_ref, o_ref):
    @pl.when(step == 0)
    def _():
        act_ref[...] = up()

    @pl.when(step == 1)
    def _():
        part = down()
        act = up()
        o_ref[...] = part
        act_ref[...] = act

    @pl.when(jnp.logical_and(step > 1, step < n_chunks))
    def _():
        part = down()
        act = up()
        o_ref[...] += part
        act_ref[...] = act

    @pl.when(step == n_chunks)
    def _():
        o_ref[...] += down()


def _up_chunk(step, n_chunks):
    return jnp.minimum(step, n_chunks - 1)


def _down_chunk(step, n_chunks):
    del n_chunks
    return jnp.maximum(step - 1, 0)


def _swiglu_pipeline(f, u_ref, w1, w3, w2, act_ref, o_ref):
    def up():
        u = u_ref[...]
        act = _silu(_dot(u, w1[...].astype(bf16))) * _dot(u, w3[...].astype(bf16))
        return act.astype(bf16)

    def down():
        return _dot(act_ref[...], w2[...].astype(bf16))

    _skewed_pipeline(f, NF, up, down, act_ref, o_ref)


def _ffn_kernel(h_ref, ng_ref, sh_ref, sc_ref, g2_ref, w1, w3, w2, o_ref, u_ref, act_ref, acc_ref):
    m = pl.program_id(0)
    f = pl.program_id(1)

    @pl.when(f == 0)
    def _():
        _ffn_prologue(h_ref, ng_ref, sh_ref, sc_ref, u_ref, m)

    _swiglu_pipeline(f, u_ref, w1, w3, w2, act_ref, acc_ref)

    @pl.when(f == NF)
    def _():
        _ffn_epilogue(h_ref, ng_ref, g2_ref, acc_ref, o_ref, m)


def _mod_specs(layer, chunks):
    specs = [pl.BlockSpec((TM, D), lambda m, *r: (m, 0), pipeline_mode=pl.Buffered(1)),
             pl.BlockSpec((None, 4, D), lambda m, *r: (layer, 0, 0))]
    for ch in chunks:
        specs.append(pl.BlockSpec((None, MOD_ROWS, D), lambda m, *r, ch=ch: (layer, 0, ch)))
    return specs


def _dense_ffn(h, norm_g, mod, w1, w3, w2, layer):
    j = layer // 2
    return pl.pallas_call(
        _ffn_kernel,
        grid=(NM, NF + 1),
        in_specs=_mod_specs(layer, (3, 4, 5)) + [
            pl.BlockSpec((None, D, TF), lambda m, f: (j, 0, _up_chunk(f, NF))),
            pl.BlockSpec((None, D, TF), lambda m, f: (j, 0, _up_chunk(f, NF))),
            pl.BlockSpec((None, TF, D), lambda m, f: (j, _down_chunk(f, NF), 0)),
        ],
        out_specs=pl.BlockSpec((TM, D), lambda m, f: (m, 0)),
        out_shape=jax.ShapeDtypeStruct((T, D), f32),
        scratch_shapes=[pltpu.VMEM((TM, D), bf16), pltpu.VMEM((TM, TF), bf16), pltpu.VMEM((TM, D), f32)],
        compiler_params=_cparams(("arbitrary", "arbitrary")),
        name="dense_ffn",
    )(h, norm_g, mod, mod, mod, w1, w3, w2)


def _split_bf16(x):
    hi = x.astype(bf16)
    lo = (x - hi.astype(f32)).astype(bf16)
    return hi, lo


R_E0, R_E1, R_G0, R_G1, R_R0, R_R1 = range(6)


def _router_kernel(h_ref, ng_ref, sh_ref, sc_ref, rw_ref, route_ref, cnt_ref, u_ref, base_ref, lg_ref):
    m = pl.program_id(0)

    @pl.when(m == 0)
    def _():
        base_ref[...] = jnp.zeros_like(base_ref)

    def store(rows, u):
        u_ref[rows, :] = u

    _norm_mod_tile(h_ref, ng_ref[2:3, :], sh_ref, sc_ref, m, store)

    wh, wl = _split_bf16(rw_ref[...])
    for half in range(2):
        sl = pl.ds(half * (TM // 2), TM // 2)
        uh, ul = _split_bf16(u_ref[sl, :])
        lg_ref[sl, :] = _dot(uh, wh) + (_dot(uh, wl) + _dot(ul, wh))

    ri = lax.broadcasted_iota(jnp.int32, (ROW_CHUNK, ROW_CHUNK), 0)
    ci = lax.broadcasted_iota(jnp.int32, (ROW_CHUNK, ROW_CHUNK), 1)
    earlier = jnp.where(ri > ci, 1.0, 0.0).astype(bf16)

    def chunk(rows, off):
        del off
        logits = lg_ref[rows, :]
        lane = lax.broadcasted_iota(jnp.int32, logits.shape, 1)
        lg = jnp.where(lane < N_EXPERTS, logits, -jnp.inf)
        m1 = jnp.max(lg, axis=-1, keepdims=True)
        i1 = jnp.min(jnp.where(lg == m1, lane, ROUTER_PAD), axis=-1, keepdims=True)
        lg2 = jnp.where(lane == i1, -jnp.inf, lg)
        m2 = jnp.max(lg2, axis=-1, keepdims=True)
        i2 = jnp.min(jnp.where(lg2 == m2, lane, ROUTER_PAD), axis=-1, keepdims=True)
        e2 = jnp.exp(m2 - m1)
        den = 1.0 + e2
        hit1 = lane == i1
        hit2 = lane == i2
        onehot = jnp.where(jnp.logical_or(hit1, hit2), 1.0, 0.0)
        before = _dot(earlier, onehot.astype(bf16)) + base_ref[...]
        r1 = jnp.sum(jnp.where(hit1, before, 0.0), axis=-1, keepdims=True)
        r2 = jnp.sum(jnp.where(hit2, before, 0.0), axis=-1, keepdims=True)
        base_ref[...] = base_ref[...] + jnp.sum(onehot, axis=0, keepdims=True)
        vals = (i1.astype(f32), i2.astype(f32), 1.0 / den, e2 / den, r1, r2)
        route = jnp.zeros(logits.shape, f32)
        for col, val in enumerate(vals):
            route = jnp.where(lane == col, val, route)
        route_ref[rows, :] = route

    _for_row_chunks(chunk)
    cnt_ref[...] = jnp.broadcast_to(base_ref[...], cnt_ref.shape)


def _router(h, norm_g, mod, router_w_pad, layer):
    j = layer // 2
    return pl.pallas_call(
        _router_kernel,
        grid=(NM,),
        in_specs=_mod_specs(layer, (3, 4)) + [
            pl.BlockSpec((None, D, ROUTER_PAD), lambda m: (j, 0, 0)),
        ],
        out_specs=[
            pl.BlockSpec((TM, ROUTER_PAD), lambda m: (m, 0)),
            pl.BlockSpec((8, ROUTER_PAD), lambda m: (0, 0)),
            pl.BlockSpec((TM, D), lambda m: (m, 0)),
        ],
        out_shape=[
            jax.ShapeDtypeStruct((T, ROUTER_PAD), f32),
            jax.ShapeDtypeStruct((8, ROUTER_PAD), f32),
            jax.ShapeDtypeStruct((T, D), f32),
        ],
        scratch_shapes=[pltpu.VMEM((1, ROUTER_PAD), f32), pltpu.VMEM((TM, ROUTER_PAD), f32)],
        compiler_params=_cparams(("arbitrary",)),
        name="router",
    )(h, norm_g, mod, mod, router_w_pad)


def _dispatch_plan(route, counts):
    expert = route[:, R_E0:R_E1 + 1].astype(jnp.int32)
    rank = route[:, R_R0:R_R1 + 1].astype(jnp.int32)
    cnt = counts[0, :N_EXPERTS].astype(jnp.int32)
    tiles = (cnt + TG - 1) // TG
    tile_end = jnp.cumsum(tiles)
    start = (tile_end - tiles) * TG
    dest = (start[expert] + rank).reshape(2 * T)
    n_used = tile_end[-1]
    tile_id = jnp.minimum(jnp.arange(NT, dtype=jnp.int32), n_used - 1)
    tile_expert = jnp.sum(tile_id[:, None] >= tile_end[None, :], axis=1).astype(jnp.int32)
    rows_used = cnt[tile_expert] - (tile_id - (tile_end - tiles)[tile_expert]) * TG
    tile_mode = jnp.where(jnp.arange(NT, dtype=jnp.int32) >= n_used, TILE_UNUSED,
                          jnp.where(rows_used <= TG // 2, TILE_HALF, TILE_FULL)).astype(jnp.int32)
    return dest, tile_expert, tile_mode


def _row_copy(src_ref, src_row, dst_ref, dst_row, sem):
    return pltpu.make_async_copy(src_ref.at[pl.ds(src_row, 1)], dst_ref.at[pl.ds(dst_row, 1)], sem)


def _dispatch_kernel(dest_ref, u_ref, xs_in_ref, xs_ref, sem):
    del xs_in_ref
    m = pl.program_id(0)

    def issue(r, carry):
        t = m * TM + r
        _row_copy(u_ref, r, xs_ref, dest_ref[2 * t], sem).start()
        _row_copy(u_ref, r, xs_ref, dest_ref[2 * t + 1], sem).start()
        return carry

    lax.fori_loop(0, TM, issue, 0, unroll=8)
    for _ in range(2):
        pltpu.make_async_copy(u_ref, xs_ref.at[pl.ds(0, TM)], sem).wait()


def _dispatch(dest, u):
    xs0 = jnp.zeros((NT * TG, D), f32)
    return pl.pallas_call(
        _dispatch_kernel,
        grid_spec=pltpu.PrefetchScalarGridSpec(
            num_scalar_prefetch=1,
            grid=(NM,),
            in_specs=[pl.BlockSpec((TM, D), lambda m, d: (m, 0)), pl.BlockSpec(memory_space=pl.ANY)],
            out_specs=pl.BlockSpec(memory_space=pl.ANY),
            scratch_shapes=[pltpu.SemaphoreType.DMA(())],
        ),
        out_shape=jax.ShapeDtypeStruct((NT * TG, D), f32),
        input_output_aliases={2: 0},
        compiler_params=_cparams(("arbitrary",)),
        name="moe_dispatch",
    )(dest, u, xs0)


TILE_UNUSED, TILE_HALF, TILE_FULL = 0, 1, 2


def _group_kernel(te_ref, tm_ref, xs_ref, w1, w3, w2, ys_ref, u_ref, act_ref):
    del te_ref
    j = pl.program_id(0)
    f = pl.program_id(1)
    mode = tm_ref[j]

    @pl.when(jnp.logical_and(mode != TILE_FULL, f == 0))
    def _():
        ys_ref[...] = jnp.zeros_like(ys_ref)

    def run(rows):
        sl = pl.ds(0, rows)

        @pl.when(f == 0)
        def _():
            u_ref[sl, :] = xs_ref[sl, :].astype(bf16)

        _swiglu_pipeline(f, u_ref.at[sl], w1, w3, w2, act_ref.at[sl], ys_ref.at[sl])

    @pl.when(mode == TILE_FULL)
    def _():
        run(TG)

    @pl.when(mode == TILE_HALF)
    def _():
        run(TG // 2)


def _grouped_ffn(tile_expert, tile_mode, xs, w1, w3, w2, layer):
    jl = layer // 2

    def chunk(j, f, tmode, which):
        return jnp.where(tmode[j] != TILE_UNUSED, which(f, NF), NF - 1)

    return pl.pallas_call(
        _group_kernel,
        grid_spec=pltpu.PrefetchScalarGridSpec(
            num_scalar_prefetch=2,
            grid=(NT, NF + 1),
            in_specs=[
                pl.BlockSpec((TG, D), lambda j, f, te, tu: (j, 0)),
                pl.BlockSpec((None, None, D, TF),
                             lambda j, f, te, tu: (jl, te[j], 0, chunk(j, f, tu, _up_chunk))),
                pl.BlockSpec((None, None, D, TF),
                             lambda j, f, te, tu: (jl, te[j], 0, chunk(j, f, tu, _up_chunk))),
                pl.BlockSpec((None, None, TF, D),
                             lambda j, f, te, tu: (jl, te[j], chunk(j, f, tu, _down_chunk), 0)),
            ],
            out_specs=pl.BlockSpec((TG, D), lambda j, f, te, tu: (j, 0)),
            scratch_shapes=[pltpu.VMEM((TG, D), bf16), pltpu.VMEM((TG, TF), bf16)],
        ),
        out_shape=jax.ShapeDtypeStruct((NT * TG, D), f32),
        compiler_params=_cparams(("arbitrary", "arbitrary")),
        name="moe_grouped_ffn",
    )(tile_expert, tile_mode, xs, w1, w3, w2)


COMBINE_PHASES = ((0, 384), (384, 384), (768, 288))
ISSUE_UNROLL = 8
assert sum(n for _, n in COMBINE_PHASES) == TM
assert all(n % ISSUE_UNROLL == 0 and n % (NORM_CHUNK * NORM_UNROLL) == 0 for _, n in COMBINE_PHASES)


def _combine_kernel(dest_ref, h_ref, route_ref, ng_ref, g2_ref, ys_ref, o_ref, ybuf, sems):
    m = pl.program_id(0)

    for p, (start, n) in enumerate(COMBINE_PHASES):
        def issue(r, carry, start=start, p=p):
            row = start + r
            t = m * TM + row
            _row_copy(ys_ref, dest_ref[2 * t], ybuf.at[0], row, sems.at[p]).start()
            _row_copy(ys_ref, dest_ref[2 * t + 1], ybuf.at[1], row, sems.at[p]).start()
            return carry

        lax.fori_loop(0, n, issue, 0, unroll=ISSUE_UNROLL)

    for p, (start, n) in enumerate(COMBINE_PHASES):
        for k in range(2):
            pltpu.make_async_copy(ys_ref.at[pl.ds(0, n)], ybuf.at[k, pl.ds(start, n)], sems.at[p]).wait()

        def chunk(c, carry, start=start):
            off = pl.multiple_of(start + c * NORM_CHUNK, NORM_CHUNK)
            rows = pl.ds(off, NORM_CHUNK)
            route = route_ref[rows, :]
            y = route[:, R_G0:R_G0 + 1] * ybuf[0, rows, :] + route[:, R_G1:R_G1 + 1] * ybuf[1, rows, :]
            o_ref[rows, :] = h_ref[rows, :] + _mod_row(g2_ref, m * TM + off) * _rms(y, ng_ref[3:4, :])
            return carry

        lax.fori_loop(0, n // NORM_CHUNK, chunk, 0, unroll=NORM_UNROLL)


def _combine(dest, h, route, norm_g, mod, ys, layer):
    return pl.pallas_call(
        _combine_kernel,
        grid_spec=pltpu.PrefetchScalarGridSpec(
            num_scalar_prefetch=1,
            grid=(NM,),
            in_specs=[
                pl.BlockSpec((TM, D), lambda m, d: (m, 0), pipeline_mode=pl.Buffered(1)),
                pl.BlockSpec((TM, ROUTER_PAD), lambda m, d: (m, 0)),
                pl.BlockSpec((None, 4, D), lambda m, d: (layer, 0, 0)),
                pl.BlockSpec((None, MOD_ROWS, D), lambda m, d: (layer, 0, 5)),
                pl.BlockSpec(memory_space=pl.ANY),
            ],
            out_specs=pl.BlockSpec((TM, D), lambda m, d: (m, 0)),
            scratch_shapes=[pltpu.VMEM((2, TM, D), f32),
                            pltpu.SemaphoreType.DMA((len(COMBINE_PHASES),))],
        ),
        out_shape=jax.ShapeDtypeStruct((T, D), f32),
        compiler_params=_cparams(("arbitrary",)),
        name="moe_combine",
    )(dest, h, route, norm_g, mod, ys)


def _moe_ffn(h, norm_g, mod, router_w_pad, w1, w3, w2, layer):
    route, counts, u = _router(h, norm_g, mod, router_w_pad, layer)
    dest, tile_expert, tile_mode = _dispatch_plan(route, counts)
    xs = _dispatch(dest, u)
    ys = _grouped_ffn(tile_expert, tile_mode, xs, w1, w3, w2, layer)
    return _combine(dest, h, route, norm_g, mod, ys, layer)


def _rope_tables():
    t = jnp.arange(S)
    row = (t // GRID_W).astype(f32)
    col = (t % GRID_W).astype(f32)
    inv = ROPE_BASE ** (-jnp.arange(ROPE_FREQS, dtype=f32) / ROPE_FREQS)
    a0 = row[:, None] * inv
    a1 = col[:, None] * inv
    cos_t = jnp.concatenate([jnp.cos(a0), jnp.cos(a0), jnp.cos(a1), jnp.cos(a1)], axis=1)
    sin_t = jnp.concatenate([-jnp.sin(a0), jnp.sin(a0), -jnp.sin(a1), jnp.sin(a1)], axis=1)
    cos_t = jnp.concatenate([cos_t, jnp.ones((C, HEAD_DIM), f32)], axis=0)
    sin_t = jnp.concatenate([sin_t, jnp.zeros((C, HEAD_DIM), f32)], axis=0)
    return cos_t, sin_t


def kernel(x, c, ctx, c_ctx, w_mod, b_mod, norm_g, w_in, b_gate, pool_w, pool_scale, w_pool_up,
           attn_sink, w_attn_up, conv_dw, conv_db, conv_ln_g, conv_ln_b, w_conv_up, w_out,
           ffn_w1, ffn_w3, ffn_w2, router_w, moe_w1, moe_w3, moe_w2):
    assert x.shape == (1, S, D) and ctx.shape == (1, C, D)
    cos_t, sin_t = _rope_tables()
    cc = jnp.concatenate([c, c_ctx[None, :], jnp.zeros((MOD_ROWS - 2, D), f32)], axis=0)
    mod = _modulation(cc, w_mod, b_mod)
    router_w_pad = jnp.pad(router_w, ((0, 0), (0, 0), (0, ROUTER_PAD - N_EXPERTS)))
    h = jnp.concatenate([x[0], ctx[0]], axis=0)
    for i in range(DEPTH):
        z = _in_proj(h, norm_g, mod, w_in, i)
        pool_o, conv_o = _seq_mixers(z, pool_w, pool_scale, conv_dw, conv_db, conv_ln_g, conv_ln_b, i)
        attn_o = _attention(z, attn_sink, cos_t, sin_t, i)
        h = _merge(h, z, pool_o, attn_o, conv_o, b_gate, w_pool_up, w_attn_up, w_conv_up, w_out,
                   norm_g, mod, i)
        if i % 2 == 0:
            h = _dense_ffn(h, norm_g, mod, ffn_w1, ffn_w3, ffn_w2, i)
        else:
            h = _moe_ffn(h, norm_g, mod, router_w_pad, moe_w1, moe_w3, moe_w2, i)
    return h[:S][None]
```

```python
import functools

import jax
import jax.numpy as jnp
from jax import lax
from jax.experimental import pallas as pl
from jax.experimental.pallas import tpu as pltpu

f32 = jnp.float32
bf16 = jnp.bfloat16

D = 2048
S = 8192
C = 256
T = S + C
DEPTH = 4
GRID_W = 64
HEAD_DIM = 128
POOL_WIDTH = 512
POOL_WINDOWS = (2, 4, 8, 16)
POOL_GROUP = 128
N_Q_HEADS = 8
N_KV_HEADS = 2
GQA = 4
WINDOW = 128
ATTN_SCALE = HEAD_DIM ** -0.5
ROPE_BASE = 10000.0
ROPE_FREQS = 32
CONV_WIDTH = 512
CONV_K = 31
GATE_WIDTH = 3 * D
POOL_OFF = GATE_WIDTH
Q_OFF = POOL_OFF + POOL_WIDTH
K_OFF = Q_OFF + N_Q_HEADS * HEAD_DIM
V_OFF = K_OFF + N_KV_HEADS * HEAD_DIM
CONV_OFF = V_OFF + N_KV_HEADS * HEAD_DIM
IN_WIDTH = CONV_OFF + 2 * CONV_WIDTH
FFN = 5632
N_EXPERTS = 8
N_MOD = 6
EPS = 1e-6
NEG_INF = -1e30

VMEM_LIMIT_BYTES = 56 * 1024 * 1024

TM = 1056
NM = T // TM
TS = 256
NS = T // TS
NS_LAT = S // TS
HALO = 16
SUBLANES = 8
TN_IN = 1024
TN_MERGE = 256
TF = 256
TN_MOD = 1024
ROUTER_PAD = 128
ROW_CHUNK = 96
NORM_CHUNK = 32
NORM_UNROLL = 3
assert S % NORM_CHUNK == 0 and TM % (NORM_CHUNK * NORM_UNROLL) == 0
TG = 1024
NT = (2 * T + N_EXPERTS * (TG - 1) + TG - 1) // TG
MOD_ROWS = 16


def _cparams(semantics):
    return pltpu.CompilerParams(dimension_semantics=semantics,
                                vmem_limit_bytes=VMEM_LIMIT_BYTES)


def _sigmoid(x):
    return 1.0 / (1.0 + jnp.exp(-x))


def _silu(x):
    return x * _sigmoid(x)


def _mod_row(mod_ref, row0):
    return mod_ref[pl.ds(jnp.where(row0 >= S, 1, 0), 1), :]


def _rms(x, g):
    return x * lax.rsqrt(jnp.mean(x * x, axis=-1, keepdims=True) + EPS) * g


def _for_row_chunks(fn):
    def step(r, carry):
        off = pl.multiple_of(r * ROW_CHUNK, ROW_CHUNK)
        fn(pl.ds(off, ROW_CHUNK), off)
        return carry
    lax.fori_loop(0, TM // ROW_CHUNK, step, 0, unroll=True)


def _for_norm_chunks(fn):
    def step(r, carry):
        off = pl.multiple_of(r * NORM_CHUNK, NORM_CHUNK)
        fn(pl.ds(off, NORM_CHUNK), off)
        return carry
    lax.fori_loop(0, TM // NORM_CHUNK, step, 0, unroll=NORM_UNROLL)


def _norm_mod_tile(h_ref, g, sh_ref, sc_ref, m, store):
    def chunk(rows, off):
        row0 = m * TM + off
        u = _rms(h_ref[rows, :], g) * (1.0 + _mod_row(sc_ref, row0)) + _mod_row(sh_ref, row0)
        store(rows, u)
    _for_norm_chunks(chunk)


def _residual_tile(h_ref, y_ref, o_ref, g, gate_ref, m):
    def chunk(rows, off):
        o_ref[rows, :] = h_ref[rows, :] + _mod_row(gate_ref, m * TM + off) * _rms(y_ref[rows, :], g)
    _for_norm_chunks(chunk)


def _dot(a, b):
    return jnp.dot(a, b, preferred_element_type=f32)


def _mod_kernel(cc_ref, w_ref, b_ref, o_ref):
    s = _silu(cc_ref[...])
    o_ref[...] = _dot(s.astype(bf16), w_ref[...].astype(bf16)) + b_ref[...]


def _modulation(cc, w_mod, b_mod):
    nn = (N_MOD * D) // TN_MOD
    return pl.pallas_call(
        _mod_kernel,
        grid=(DEPTH, nn),
        in_specs=[
            pl.BlockSpec((MOD_ROWS, D), lambda l, n: (0, 0)),
            pl.BlockSpec((None, D, TN_MOD), lambda l, n: (l, 0, n)),
            pl.BlockSpec((None, 1, TN_MOD), lambda l, n: (l, 0, n)),
        ],
        out_specs=pl.BlockSpec((None, MOD_ROWS, TN_MOD), lambda l, n: (l, 0, n)),
        out_shape=jax.ShapeDtypeStruct((DEPTH, MOD_ROWS, N_MOD * D), f32),
        compiler_params=_cparams(("arbitrary", "arbitrary")),
        name="modulation",
    )(cc, w_mod, b_mod.reshape(DEPTH, 1, N_MOD * D))


def _win_kernel(h_ref, ng_ref, sh_ref, sc_ref, w_ref, z_ref, u_ref):
    m = pl.program_id(0)

    @pl.when(pl.program_id(1) == 0)
    def _():
        def store(rows, u):
            u_ref[rows, :] = u.astype(bf16)
        _norm_mod_tile(h_ref, ng_ref[0:1, :], sh_ref, sc_ref, m, store)

    z_ref[...] = _dot(u_ref[...], w_ref[...].astype(bf16)).astype(bf16)


def _in_proj(h, norm_g, mod, w_in, layer):
    return pl.pallas_call(
        _win_kernel,
        grid=(NM, IN_WIDTH // TN_IN),
        in_specs=[
            pl.BlockSpec((TM, D), lambda m, n: (m, 0)),
            pl.BlockSpec((None, 4, D), lambda m, n: (layer, 0, 0)),
            pl.BlockSpec((None, MOD_ROWS, D), lambda m, n: (layer, 0, 0)),
            pl.BlockSpec((None, MOD_ROWS, D), lambda m, n: (layer, 0, 1)),
            pl.BlockSpec((None, D, TN_IN), lambda m, n: (layer, 0, n)),
        ],
        out_specs=pl.BlockSpec((TM, TN_IN), lambda m, n: (m, n)),
        out_shape=jax.ShapeDtypeStruct((T, IN_WIDTH), bf16),
        scratch_shapes=[pltpu.VMEM((TM, D), bf16)],
        compiler_params=_cparams(("arbitrary", "arbitrary")),
        name="in_proj",
    )(h, norm_g, mod, mod, w_in)


def _seqmix_kernel(zp_c, zp_p, zp_n, zc_c, zc_p, zc_n, pw_ref, ps_ref, dw_ref, db_ref,
                   lng_ref, lnb_ref, po_ref, co_ref, xe, ge, gs):
    i = pl.program_id(0)
    has_prev = jnp.where(jnp.logical_and(i != 0, i != NS_LAT), 1.0, 0.0).astype(f32)
    has_next = jnp.where(i < NS_LAT - 1, 1.0, 0.0).astype(f32)
    is_lat = i < NS_LAT
    pos = jnp.where(is_lat, i * TS, 0) + lax.broadcasted_iota(jnp.int32, (TS, 1), 0)
    seq_len = jnp.where(is_lat, S, C)

    xe[0:HALO, :] = zp_p[...].astype(f32) * has_prev
    xe[HALO:HALO + TS, :] = zp_c[...].astype(f32)
    xe[HALO + TS:, :] = zp_n[...].astype(f32) * has_next
    outs = []
    for g, w in enumerate(POOL_WINDOWS):
        cols = slice(g * POOL_GROUP, (g + 1) * POOL_GROUP)
        acc = xe[pl.ds(HALO - w // 2, TS), cols]
        for d in range(-w // 2 + 1, w // 2):
            acc = acc + xe[pl.ds(HALO + d, TS), cols]
        lo = jnp.clip(pos - w // 2, 0, seq_len)
        hi = jnp.clip(pos - w // 2 + w, 0, seq_len)
        diff = acc / (hi - lo).astype(f32) - xe[HALO:HALO + TS, cols]
        outs.append(_dot(diff.astype(bf16), pw_ref[g].astype(bf16)))
    po_ref[...] = (jnp.concatenate(outs, axis=1) * ps_ref[...]).astype(bf16)

    def glu(ref):
        zz = ref[...].astype(f32)
        return zz[:, :CONV_WIDTH] * _sigmoid(zz[:, CONV_WIDTH:])

    ge[0:HALO, :] = glu(zc_p) * has_prev
    ge[HALO:HALO + TS, :] = glu(zc_c)
    ge[HALO + TS:, :] = glu(zc_n) * has_next
    span = TS + 2 * HALO - SUBLANES
    for s in range(SUBLANES):
        gs[s] = ge[pl.ds(s, span), :]
    base = HALO - CONV_K // 2
    acc = None
    for k in range(CONV_K):
        shift, start = (base + k) % SUBLANES, (base + k) // SUBLANES * SUBLANES
        term = gs[shift, pl.ds(start, TS), :] * dw_ref[k:k + 1, :]
        acc = term if acc is None else acc + term
    y = acc + db_ref[...]
    yc = y - jnp.mean(y, axis=-1, keepdims=True)
    yn = yc * lax.rsqrt(jnp.mean(yc * yc, axis=-1, keepdims=True) + EPS) * lng_ref[...] + lnb_ref[...]
    co_ref[...] = _silu(yn).astype(bf16)


def _seq_mixers(z, pool_w, pool_scale, conv_dw, conv_db, conv_ln_g, conv_ln_b, layer):
    rb = TS // HALO
    last = T // HALO - 1
    pcol = POOL_OFF // POOL_WIDTH
    ccol = CONV_OFF // (2 * CONV_WIDTH)

    def prev(i):
        return jnp.maximum(i * rb - 1, 0)

    def nxt(i):
        return jnp.minimum((i + 1) * rb, last)

    vec = lambda width: pl.BlockSpec((None, 1, width), lambda i: (layer, 0, 0))
    return pl.pallas_call(
        _seqmix_kernel,
        grid=(NS,),
        in_specs=[
            pl.BlockSpec((TS, POOL_WIDTH), lambda i: (i, pcol)),
            pl.BlockSpec((HALO, POOL_WIDTH), lambda i: (prev(i), pcol)),
            pl.BlockSpec((HALO, POOL_WIDTH), lambda i: (nxt(i), pcol)),
            pl.BlockSpec((TS, 2 * CONV_WIDTH), lambda i: (i, ccol)),
            pl.BlockSpec((HALO, 2 * CONV_WIDTH), lambda i: (prev(i), ccol)),
            pl.BlockSpec((HALO, 2 * CONV_WIDTH), lambda i: (nxt(i), ccol)),
            pl.BlockSpec((None, 4, POOL_GROUP, POOL_GROUP), lambda i: (layer, 0, 0, 0)),
            vec(POOL_WIDTH),
            pl.BlockSpec((None, CONV_K, CONV_WIDTH), lambda i: (layer, 0, 0)),
            vec(CONV_WIDTH), vec(CONV_WIDTH), vec(CONV_WIDTH),
        ],
        out_specs=[
            pl.BlockSpec((TS, POOL_WIDTH), lambda i: (i, 0)),
            pl.BlockSpec((TS, CONV_WIDTH), lambda i: (i, 0)),
        ],
        out_shape=[
            jax.ShapeDtypeStruct((T, POOL_WIDTH), bf16),
            jax.ShapeDtypeStruct((T, CONV_WIDTH), bf16),
        ],
        scratch_shapes=[
            pltpu.VMEM((TS + 2 * HALO, POOL_WIDTH), f32),
            pltpu.VMEM((TS + 2 * HALO, CONV_WIDTH), f32),
            pltpu.VMEM((SUBLANES, TS + 2 * HALO - SUBLANES, CONV_WIDTH), f32),
        ],
        compiler_params=_cparams(("arbitrary",)),
        name="seq_mixers",
    )(z, z, z, z, z, z, pool_w, pool_scale.reshape(DEPTH, 1, POOL_WIDTH), conv_dw,
      conv_db.reshape(DEPTH, 1, CONV_WIDTH), conv_ln_g.reshape(DEPTH, 1, CONV_WIDTH),
      conv_ln_b.reshape(DEPTH, 1, CONV_WIDTH))


def _rope(x, cos_t, sin_t):
    lane = lax.broadcasted_iota(jnp.int32, x.shape, 1) & (2 * ROPE_FREQS - 1)
    partner = jnp.where(lane < ROPE_FREQS, pltpu.roll(x, HEAD_DIM - ROPE_FREQS, 1),
                        pltpu.roll(x, ROPE_FREQS, 1))
    return x * cos_t + partner * sin_t


def _attn_kernel(sink_ref, q_ref, kc_ref, kp_ref, kn_ref, vc_ref, vp_ref, vn_ref, kx_ref, vx_ref,
                 cq_ref, sq_ref, cp_ref, sp_ref, cn_ref, sn_ref, o_ref, *, layer):
    hk = pl.program_id(0)
    i = pl.program_id(1)
    cq = cq_ref[...]
    sq = sq_ref[...]
    qs = []
    for j in range(GQA):
        qj = q_ref[:, j * HEAD_DIM:(j + 1) * HEAD_DIM].astype(f32)
        qs.append(_rope(qj, cq, sq).astype(bf16))
    q = jnp.concatenate(qs, axis=0)
    k = jnp.concatenate([
        _rope(kp_ref[...].astype(f32), cp_ref[...], sp_ref[...]).astype(bf16),
        _rope(kc_ref[...].astype(f32), cq, sq).astype(bf16),
        _rope(kn_ref[...].astype(f32), cn_ref[...], sn_ref[...]).astype(bf16)], axis=0)
    v = jnp.concatenate([vp_ref[...], vc_ref[...], vn_ref[...]], axis=0)
    kw = k.shape[0]

    nt = (((1,), (1,)), ((), ()))
    s_loc = lax.dot_general(q, k, nt, preferred_element_type=f32) * ATTN_SCALE
    s_ctx = lax.dot_general(q, kx_ref[...], nt, preferred_element_type=f32) * ATTN_SCALE

    qrow = lax.broadcasted_iota(jnp.int32, (GQA * TS, 1), 0) & (TS - 1)
    krel = lax.broadcasted_iota(jnp.int32, (1, kw), 1) - WINDOW
    kpos = i * TS + krel
    k_ok = jnp.logical_and(jnp.logical_and(kpos >= 0, kpos < S), i < NS_LAT)
    valid = jnp.logical_and(jnp.abs(krel - qrow) <= WINDOW, k_ok)
    s_loc = jnp.where(valid, s_loc, NEG_INF)

    sink = jnp.concatenate(
        [jnp.full((TS, 1), sink_ref[layer, hk * GQA + j], f32) for j in range(GQA)], axis=0)
    mx = jnp.maximum(jnp.maximum(jnp.max(s_loc, axis=-1, keepdims=True),
                                 jnp.max(s_ctx, axis=-1, keepdims=True)), sink)
    p_loc = jnp.exp(s_loc - mx)
    p_ctx = jnp.exp(s_ctx - mx)
    denom = (jnp.sum(p_loc, axis=-1, keepdims=True) + jnp.sum(p_ctx, axis=-1, keepdims=True)
             + jnp.exp(sink - mx))
    o = (_dot(p_loc.astype(bf16), v) + _dot(p_ctx.astype(bf16), vx_ref[...])) / denom
    for j in range(GQA):
        o_ref[:, j * HEAD_DIM:(j + 1) * HEAD_DIM] = o[j * TS:(j + 1) * TS, :].astype(bf16)


def _attention(z, attn_sink, cos_t, sin_t, layer):
    qcol = Q_OFF // (GQA * HEAD_DIM)
    kcol = K_OFF // HEAD_DIM
    vcol = V_OFF // HEAD_DIM
    hb = TS // WINDOW
    last = T // WINDOW - 1
    ctx_blk = S // TS

    def prev(i):
        return jnp.maximum(i * hb - 1, 0)

    def nxt(i):
        return jnp.minimum((i + 1) * hb, last)

    def kv_specs(col):
        return [
            pl.BlockSpec((TS, HEAD_DIM), lambda h, i: (i, col + h)),
            pl.BlockSpec((WINDOW, HEAD_DIM), lambda h, i: (prev(i), col + h)),
            pl.BlockSpec((WINDOW, HEAD_DIM), lambda h, i: (nxt(i), col + h)),
        ]

    tab_specs = [
        pl.BlockSpec((TS, HEAD_DIM), lambda h, i: (i, 0)),
        pl.BlockSpec((TS, HEAD_DIM), lambda h, i: (i, 0)),
        pl.BlockSpec((WINDOW, HEAD_DIM), lambda h, i: (prev(i), 0)),
        pl.BlockSpec((WINDOW, HEAD_DIM), lambda h, i: (prev(i), 0)),
        pl.BlockSpec((WINDOW, HEAD_DIM), lambda h, i: (nxt(i), 0)),
        pl.BlockSpec((WINDOW, HEAD_DIM), lambda h, i: (nxt(i), 0)),
    ]
    return pl.pallas_call(
        functools.partial(_attn_kernel, layer=layer),
        grid=(N_KV_HEADS, NS),
        in_specs=[pl.BlockSpec(memory_space=pltpu.SMEM),
                  pl.BlockSpec((TS, GQA * HEAD_DIM), lambda h, i: (i, qcol + h))]
                 + kv_specs(kcol) + kv_specs(vcol)
                 + [pl.BlockSpec((TS, HEAD_DIM), lambda h, i: (ctx_blk, kcol + h)),
                    pl.BlockSpec((TS, HEAD_DIM), lambda h, i: (ctx_blk, vcol + h))]
                 + tab_specs,
        out_specs=pl.BlockSpec((TS, GQA * HEAD_DIM), lambda h, i: (i, h)),
        out_shape=jax.ShapeDtypeStruct((T, N_Q_HEADS * HEAD_DIM), bf16),
        compiler_params=_cparams(("arbitrary", "arbitrary")),
        name="attention",
    )(attn_sink, z, z, z, z, z, z, z, z, z, cos_t, sin_t, cos_t, sin_t, cos_t, sin_t)


NB_MERGE = D // TN_MERGE


def _merge_kernel(h_ref, p_ref, a_ref, c_ref, zg0, zg1, zg2, bg0, bg1, bg2, wp, wa, wc, wo,
                  ng_ref, g1_ref, o_ref, act_ref):
    m = pl.program_id(0)
    n = pl.program_id(1)

    def gate(zg, bg):
        return _sigmoid(zg[...].astype(f32) + bg[...])

    def up():
        merged = (gate(zg0, bg0) * _dot(p_ref[...], wp[...].astype(bf16))
                  + gate(zg1, bg1) * _dot(a_ref[...], wa[...].astype(bf16))
                  + gate(zg2, bg2) * _dot(c_ref[...], wc[...].astype(bf16)))
        return merged.astype(bf16)

    def down():
        return _dot(act_ref[...], wo[...].astype(bf16))

    _skewed_pipeline(n, NB_MERGE, up, down, act_ref, o_ref)

    @pl.when(n == NB_MERGE)
    def _():
        _residual_tile(h_ref, o_ref, o_ref, ng_ref[1:2, :], g1_ref, m)


def _merge(h, z, pool_o, attn_o, conv_o, b_gate, w_pool_up, w_attn_up, w_conv_up, w_out,
           norm_g, mod, layer):
    nb = NB_MERGE
    bg = b_gate.reshape(DEPTH, 1, GATE_WIDTH)
    upc = lambda n: _up_chunk(n, nb)
    zgate = lambda b: pl.BlockSpec((TM, TN_MERGE), lambda m, n: (m, b * nb + upc(n)))
    bgate = lambda b: pl.BlockSpec((None, 1, TN_MERGE), lambda m, n: (layer, 0, b * nb + upc(n)))
    wup = lambda k: pl.BlockSpec((None, k, TN_MERGE), lambda m, n: (layer, 0, upc(n)))
    return pl.pallas_call(
        _merge_kernel,
        grid=(NM, nb + 1),
        in_specs=[
            pl.BlockSpec((TM, D), lambda m, n: (m, 0), pipeline_mode=pl.Buffered(1)),
            pl.BlockSpec((TM, POOL_WIDTH), lambda m, n: (m, 0)),
            pl.BlockSpec((TM, N_Q_HEADS * HEAD_DIM), lambda m, n: (m, 0)),
            pl.BlockSpec((TM, CONV_WIDTH), lambda m, n: (m, 0)),
            zgate(0), zgate(1), zgate(2), bgate(0), bgate(1), bgate(2),
            wup(POOL_WIDTH), wup(N_Q_HEADS * HEAD_DIM), wup(CONV_WIDTH),
            pl.BlockSpec((None, TN_MERGE, D), lambda m, n: (layer, _down_chunk(n, nb), 0)),
            pl.BlockSpec((None, 4, D), lambda m, n: (layer, 0, 0)),
            pl.BlockSpec((None, MOD_ROWS, D), lambda m, n: (layer, 0, 2)),
        ],
        out_specs=pl.BlockSpec((TM, D), lambda m, n: (m, 0)),
        out_shape=jax.ShapeDtypeStruct((T, D), f32),
        scratch_shapes=[pltpu.VMEM((TM, TN_MERGE), bf16)],
        compiler_params=_cparams(("arbitrary", "arbitrary")),
        name="merge",
    )(h, pool_o, attn_o, conv_o, z, z, z, bg, bg, bg, w_pool_up, w_attn_up, w_conv_up, w_out,
      norm_g, mod)


def _ffn_prologue(h_ref, ng_ref, sh_ref, sc_ref, u_ref, m):
    def store(rows, u):
        u_ref[rows, :] = u.astype(bf16)
    _norm_mod_tile(h_ref, ng_ref[2:3, :], sh_ref, sc_ref, m, store)


def _ffn_epilogue(h_ref, ng_ref, g2_ref, y_ref, o_ref, m):
    _residual_tile(h_ref, y_ref, o_ref, ng_ref[3:4, :], g2_ref, m)


NF = FFN // TF


def _skewed_pipeline(step, n_chunks, up, down, act_ref, o_ref):
    @pl.when(step == 0)
    def _():
        act_ref[...] = up()

    @pl.when(step == 1)
    def _():
        part = down()
        act = up()
        o_ref[...] = part
        act_ref[...] = act

    @pl.when(jnp.logical_and(step > 1, step < n_chunks))
    def _():
        part = down()
        act = up()
        o_ref[...] += part
        act_ref[...] = act

    @pl.when(step == n_chunks)
    def _():
        o_ref[...] += down()


def _up_chunk(step, n_chunks):
    return jnp.minimum(step, n_chunks - 1)


def _down_chunk(step, n_chunks):
    del n_chunks
    return jnp.maximum(step - 1, 0)


def _swiglu_pipeline(f, u_ref, w1, w3, w2, act_ref, o_ref):
    def up():
        u = u_ref[...]
        act = _silu(_dot(u, w1[...].astype(bf16))) * _dot(u, w3[...].astype(bf16))
        return act.astype(bf16)

    def down():
        return _dot(act_ref[...], w2[...].astype(bf16))

    _skewed_pipeline(f, NF, up, down, act_ref, o_ref)


def _ffn_kernel(h_ref, ng_ref, sh_ref, sc_ref, g2_ref, w1, w3, w2, o_ref, u_ref, act_ref, acc_ref):
    m = pl.program_id(0)
    f = pl.program_id(1)

    @pl.when(f == 0)
    def _():
        _ffn_prologue(h_ref, ng_ref, sh_ref, sc_ref, u_ref, m)

    _swiglu_pipeline(f, u_ref, w1, w3, w2, act_ref, acc_ref)

    @pl.when(f == NF)
    def _():
        _ffn_epilogue(h_ref, ng_ref, g2_ref, acc_ref, o_ref, m)


def _mod_specs(layer, chunks):
    specs = [pl.BlockSpec((TM, D), lambda m, *r: (m, 0), pipeline_mode=pl.Buffered(1)),
             pl.BlockSpec((None, 4, D), lambda m, *r: (layer, 0, 0))]
    for ch in chunks:
        specs.append(pl.BlockSpec((None, MOD_ROWS, D), lambda m, *r, ch=ch: (layer, 0, ch)))
    return specs


def _dense_ffn(h, norm_g, mod, w1, w3, w2, layer):
    j = layer // 2
    return pl.pallas_call(
        _ffn_kernel,
        grid=(NM, NF + 1),
        in_specs=_mod_specs(layer, (3, 4, 5)) + [
            pl.BlockSpec((None, D, TF), lambda m, f: (j, 0, _up_chunk(f, NF))),
            pl.BlockSpec((None, D, TF), lambda m, f: (j, 0, _up_chunk(f, NF))),
            pl.BlockSpec((None, TF, D), lambda m, f: (j, _down_chunk(f, NF), 0)),
        ],
        out_specs=pl.BlockSpec((TM, D), lambda m, f: (m, 0)),
        out_shape=jax.ShapeDtypeStruct((T, D), f32),
        scratch_shapes=[pltpu.VMEM((TM, D), bf16), pltpu.VMEM((TM, TF), bf16), pltpu.VMEM((TM, D), f32)],
        compiler_params=_cparams(("arbitrary", "arbitrary")),
        name="dense_ffn",
    )(h, norm_g, mod, mod, mod, w1, w3, w2)


def _split_bf16(x):
    hi = x.astype(bf16)
    lo = (x - hi.astype(f32)).astype(bf16)
    return hi, lo


R_E0, R_E1, R_G0, R_G1, R_R0, R_R1 = range(6)


def _router_kernel(h_ref, ng_ref, sh_ref, sc_ref, rw_ref, route_ref, cnt_ref, u_ref, base_ref, lg_ref):
    m = pl.program_id(0)

    @pl.when(m == 0)
    def _():
        base_ref[...] = jnp.zeros_like(base_ref)

    def store(rows, u):
        u_ref[rows, :] = u

    _norm_mod_tile(h_ref, ng_ref[2:3, :], sh_ref, sc_ref, m, store)

    wh, wl = _split_bf16(rw_ref[...])
    for half in range(2):
        sl = pl.ds(half * (TM // 2), TM // 2)
        uh, ul = _split_bf16(u_ref[sl, :])
        lg_ref[sl, :] = _dot(uh, wh) + (_dot(uh, wl) + _dot(ul, wh))

    ri = lax.broadcasted_iota(jnp.int32, (ROW_CHUNK, ROW_CHUNK), 0)
    ci = lax.broadcasted_iota(jnp.int32, (ROW_CHUNK, ROW_CHUNK), 1)
    earlier = jnp.where(ri > ci, 1.0, 0.0).astype(bf16)

    def chunk(rows, off):
        del off
        logits = lg_ref[rows, :]
        lane = lax.broadcasted_iota(jnp.int32, logits.shape, 1)
        lg = jnp.where(lane < N_EXPERTS, logits, -jnp.inf)
        m1 = jnp.max(lg, axis=-1, keepdims=True)
        i1 = jnp.min(jnp.where(lg == m1, lane, ROUTER_PAD), axis=-1, keepdims=True)
        lg2 = jnp.where(lane == i1, -jnp.inf, lg)
        m2 = jnp.max(lg2, axis=-1, keepdims=True)
        i2 = jnp.min(jnp.where(lg2 == m2, lane, ROUTER_PAD), axis=-1, keepdims=True)
        e2 = jnp.exp(m2 - m1)
        den = 1.0 + e2
        hit1 = lane == i1
        hit2 = lane == i2
        onehot = jnp.where(jnp.logical_or(hit1, hit2), 1.0, 0.0)
        before = _dot(earlier, onehot.astype(bf16)) + base_ref[...]
        r1 = jnp.sum(jnp.where(hit1, before, 0.0), axis=-1, keepdims=True)
        r2 = jnp.sum(jnp.where(hit2, before, 0.0), axis=-1, keepdims=True)
        base_ref[...] = base_ref[...] + jnp.sum(onehot, axis=0, keepdims=True)
        vals = (i1.astype(f32), i2.astype(f32), 1.0 / den, e2 / den, r1, r2)
        route = jnp.zeros(logits.shape, f32)
        for col, val in enumerate(vals):
            route = jnp.where(lane == col, val, route)
        route_ref[rows, :] = route

    _for_row_chunks(chunk)
    cnt_ref[...] = jnp.broadcast_to(base_ref[...], cnt_ref.shape)


def _router(h, norm_g, mod, router_w_pad, layer):
    j = layer // 2
    return pl.pallas_call(
        _router_kernel,
        grid=(NM,),
        in_specs=_mod_specs(layer, (3, 4)) + [
            pl.BlockSpec((None, D, ROUTER_PAD), lambda m: (j, 0, 0)),
        ],
        out_specs=[
            pl.BlockSpec((TM, ROUTER_PAD), lambda m: (m, 0)),
            pl.BlockSpec((8, ROUTER_PAD), lambda m: (0, 0)),
            pl.BlockSpec((TM, D), lambda m: (m, 0)),
        ],
        out_shape=[
            jax.ShapeDtypeStruct((T, ROUTER_PAD), f32),
            jax.ShapeDtypeStruct((8, ROUTER_PAD), f32),
            jax.ShapeDtypeStruct((T, D), f32),
        ],
        scratch_shapes=[pltpu.VMEM((1, ROUTER_PAD), f32), pltpu.VMEM((TM, ROUTER_PAD), f32)],
        compiler_params=_cparams(("arbitrary",)),
        name="router",
    )(h, norm_g, mod, mod, router_w_pad)


def _dispatch_plan(route, counts):
    expert = route[:, R_E0:R_E1 + 1].astype(jnp.int32)
    rank = route[:, R_R0:R_R1 + 1].astype(jnp.int32)
    cnt = counts[0, :N_EXPERTS].astype(jnp.int32)
    tiles = (cnt + TG - 1) // TG
    tile_end = jnp.cumsum(tiles)
    start = (tile_end - tiles) * TG
    dest = (start[expert] + rank).reshape(2 * T)
    n_used = tile_end[-1]
    tile_id = jnp.minimum(jnp.arange(NT, dtype=jnp.int32), n_used - 1)
    tile_expert = jnp.sum(tile_id[:, None] >= tile_end[None, :], axis=1).astype(jnp.int32)
    rows_used = cnt[tile_expert] - (tile_id - (tile_end - tiles)[tile_expert]) * TG
    tile_mode = jnp.where(jnp.arange(NT, dtype=jnp.int32) >= n_used, TILE_UNUSED,
                          jnp.where(rows_used <= TG // 2, TILE_HALF, TILE_FULL)).astype(jnp.int32)
    return dest, tile_expert, tile_mode


def _row_copy(src_ref, src_row, dst_ref, dst_row, sem):
    return pltpu.make_async_copy(src_ref.at[pl.ds(src_row, 1)], dst_ref.at[pl.ds(dst_row, 1)], sem)


def _dispatch_kernel(dest_ref, u_ref, xs_in_ref, xs_ref, sem):
    del xs_in_ref
    m = pl.program_id(0)

    def issue(r, carry):
        t = m * TM + r
        _row_copy(u_ref, r, xs_ref, dest_ref[2 * t], sem).start()
        _row_copy(u_ref, r, xs_ref, dest_ref[2 * t + 1], sem).start()
        return carry

    lax.fori_loop(0, TM, issue, 0, unroll=8)
    for _ in range(2):
        pltpu.make_async_copy(u_ref, xs_ref.at[pl.ds(0, TM)], sem).wait()


def _dispatch(dest, u):
    xs0 = jnp.zeros((NT * TG, D), f32)
    return pl.pallas_call(
        _dispatch_kernel,
        grid_spec=pltpu.PrefetchScalarGridSpec(
            num_scalar_prefetch=1,
            grid=(NM,),
            in_specs=[pl.BlockSpec((TM, D), lambda m, d: (m, 0)), pl.BlockSpec(memory_space=pl.ANY)],
            out_specs=pl.BlockSpec(memory_space=pl.ANY),
            scratch_shapes=[pltpu.SemaphoreType.DMA(())],
        ),
        out_shape=jax.ShapeDtypeStruct((NT * TG, D), f32),
        input_output_aliases={2: 0},
        compiler_params=_cparams(("arbitrary",)),
        name="moe_dispatch",
    )(dest, u, xs0)


TILE_UNUSED, TILE_HALF, TILE_FULL = 0, 1, 2


def _group_kernel(te_ref, tm_ref, xs_ref, w1, w3, w2, ys_ref, u_ref, act_ref):
    del te_ref
    j = pl.program_id(0)
    f = pl.program_id(1)
    mode = tm_ref[j]

    @pl.when(jnp.logical_and(mode != TILE_FULL, f == 0))
    def _():
        ys_ref[...] = jnp.zeros_like(ys_ref)

    def run(rows):
        sl = pl.ds(0, rows)

        @pl.when(f == 0)
        def _():
            u_ref[sl, :] = xs_ref[sl, :].astype(bf16)

        _swiglu_pipeline(f, u_ref.at[sl], w1, w3, w2, act_ref.at[sl], ys_ref.at[sl])

    @pl.when(mode == TILE_FULL)
    def _():
        run(TG)

    @pl.when(mode == TILE_HALF)
    def _():
        run(TG // 2)


def _grouped_ffn(tile_expert, tile_mode, xs, w1, w3, w2, layer):
    jl = layer // 2

    def chunk(j, f, tmode, which):
        return jnp.where(tmode[j] != TILE_UNUSED, which(f, NF), NF - 1)

    return pl.pallas_call(
        _group_kernel,
        grid_spec=pltpu.PrefetchScalarGridSpec(
            num_scalar_prefetch=2,
            grid=(NT, NF + 1),
            in_specs=[
                pl.BlockSpec((TG, D), lambda j, f, te, tu: (j, 0)),
                pl.BlockSpec((None, None, D, TF),
                             lambda j, f, te, tu: (jl, te[j], 0, chunk(j, f, tu, _up_chunk))),
                pl.BlockSpec((None, None, D, TF),
                             lambda j, f, te, tu: (jl, te[j], 0, chunk(j, f, tu, _up_chunk))),
                pl.BlockSpec((None, None, TF, D),
                             lambda j, f, te, tu: (jl, te[j], chunk(j, f, tu, _down_chunk), 0)),
            ],
            out_specs=pl.BlockSpec((TG, D), lambda j, f, te, tu: (j, 0)),
            scratch_shapes=[pltpu.VMEM((TG, D), bf16), pltpu.VMEM((TG, TF), bf16)],
        ),
        out_shape=jax.ShapeDtypeStruct((NT * TG, D), f32),
        compiler_params=_cparams(("arbitrary", "arbitrary")),
        name="moe_grouped_ffn",
    )(tile_expert, tile_mode, xs, w1, w3, w2)


COMBINE_PHASES = ((0, 384), (384, 384), (768, 288))
ISSUE_UNROLL = 8
assert sum(n for _, n in COMBINE_PHASES) == TM
assert all(n % ISSUE_UNROLL == 0 and n % (NORM_CHUNK * NORM_UNROLL) == 0 for _, n in COMBINE_PHASES)


def _combine_kernel(dest_ref, h_ref, route_ref, ng_ref, g2_ref, ys_ref, o_ref, ybuf, sems):
    m = pl.program_id(0)

    for p, (start, n) in enumerate(COMBINE_PHASES):
        def issue(r, carry, start=start, p=p):
            row = start + r
            t = m * TM + row
            _row_copy(ys_ref, dest_ref[2 * t], ybuf.at[0], row, sems.at[p]).start()
            _row_copy(ys_ref, dest_ref[2 * t + 1], ybuf.at[1], row, sems.at[p]).start()
            return carry

        lax.fori_loop(0, n, issue, 0, unroll=ISSUE_UNROLL)

    for p, (start, n) in enumerate(COMBINE_PHASES):
        for k in range(2):
            pltpu.make_async_copy(ys_ref.at[pl.ds(0, n)], ybuf.at[k, pl.ds(start, n)], sems.at[p]).wait()

        def chunk(c, carry, start=start):
            off = pl.multiple_of(start + c * NORM_CHUNK, NORM_CHUNK)
            rows = pl.ds(off, NORM_CHUNK)
            route = route_ref[rows, :]
            y = route[:, R_G0:R_G0 + 1] * ybuf[0, rows, :] + route[:, R_G1:R_G1 + 1] * ybuf[1, rows, :]
            o_ref[rows, :] = h_ref[rows, :] + _mod_row(g2_ref, m * TM + off) * _rms(y, ng_ref[3:4, :])
            return carry

        lax.fori_loop(0, n // NORM_CHUNK, chunk, 0, unroll=NORM_UNROLL)


def _combine(dest, h, route, norm_g, mod, ys, layer, out_rows):
    return pl.pallas_call(
        _combine_kernel,
        grid_spec=pltpu.PrefetchScalarGridSpec(
            num_scalar_prefetch=1,
            grid=(NM,),
            in_specs=[
                pl.BlockSpec((TM, D), lambda m, d: (m, 0), pipeline_mode=pl.Buffered(1)),
                pl.BlockSpec((TM, ROUTER_PAD), lambda m, d: (m, 0)),
                pl.BlockSpec((None, 4, D), lambda m, d: (layer, 0, 0)),
                pl.BlockSpec((None, MOD_ROWS, D), lambda m, d: (layer, 0, 5)),
                pl.BlockSpec(memory_space=pl.ANY),
            ],
            out_specs=pl.BlockSpec((TM, D), lambda m, d: (m, 0)),
            scratch_shapes=[pltpu.VMEM((2, TM, D), f32),
                            pltpu.SemaphoreType.DMA((len(COMBINE_PHASES),))],
        ),
        out_shape=jax.ShapeDtypeStruct((out_rows, D), f32),
        compiler_params=_cparams(("arbitrary",)),
        name="moe_combine",
    )(dest, h, route, norm_g, mod, ys)


def _moe_ffn(h, norm_g, mod, router_w_pad, w1, w3, w2, layer, out_rows):
    route, counts, u = _router(h, norm_g, mod, router_w_pad, layer)
    dest, tile_expert, tile_mode = _dispatch_plan(route, counts)
    xs = _dispatch(dest, u)
    ys = _grouped_ffn(tile_expert, tile_mode, xs, w1, w3, w2, layer)
    return _combine(dest, h, route, norm_g, mod, ys, layer, out_rows)


def _rope_tables():
    t = jnp.arange(S)
    row = (t // GRID_W).astype(f32)
    col = (t % GRID_W).astype(f32)
    inv = ROPE_BASE ** (-jnp.arange(ROPE_FREQS, dtype=f32) / ROPE_FREQS)
    a0 = row[:, None] * inv
    a1 = col[:, None] * inv
    cos_t = jnp.concatenate([jnp.cos(a0), jnp.cos(a0), jnp.cos(a1), jnp.cos(a1)], axis=1)
    sin_t = jnp.concatenate([-jnp.sin(a0), jnp.sin(a0), -jnp.sin(a1), jnp.sin(a1)], axis=1)
    cos_t = jnp.concatenate([cos_t, jnp.ones((C, HEAD_DIM), f32)], axis=0)
    sin_t = jnp.concatenate([sin_t, jnp.zeros((C, HEAD_DIM), f32)], axis=0)
    return cos_t, sin_t


def kernel(x, c, ctx, c_ctx, w_mod, b_mod, norm_g, w_in, b_gate, pool_w, pool_scale, w_pool_up,
           attn_sink, w_attn_up, conv_dw, conv_db, conv_ln_g, conv_ln_b, w_conv_up, w_out,
           ffn_w1, ffn_w3, ffn_w2, router_w, moe_w1, moe_w3, moe_w2):
    assert x.shape == (1, S, D) and ctx.shape == (1, C, D)
    cos_t, sin_t = _rope_tables()
    cc = jnp.concatenate([c, c_ctx[None, :], jnp.zeros((MOD_ROWS - 2, D), f32)], axis=0)
    mod = _modulation(cc, w_mod, b_mod)
    router_w_pad = jnp.pad(router_w, ((0, 0), (0, 0), (0, ROUTER_PAD - N_EXPERTS)))
    h = jnp.concatenate([x[0], ctx[0]], axis=0)
    for i in range(DEPTH):
        z = _in_proj(h, norm_g, mod, w_in, i)
        pool_o, conv_o = _seq_mixers(z, pool_w, pool_scale, conv_dw, conv_db, conv_ln_g, conv_ln_b, i)
        attn_o = _attention(z, attn_sink, cos_t, sin_t, i)
        h = _merge(h, z, pool_o, attn_o, conv_o, b_gate, w_pool_up, w_attn_up, w_conv_up, w_out,
                   norm_g, mod, i)
        if i % 2 == 0:
            h = _dense_ffn(h, norm_g, mod, ffn_w1, ffn_w3, ffn_w2, i)
        else:
            h = _moe_ffn(h, norm_g, mod, router_w_pad, moe_w1, moe_w3, moe_w2, i,
                         out_rows=S if i == DEPTH - 1 else T)
    return h[None]
```

```python
import functools

import jax
import jax.numpy as jnp
from jax import lax
from jax.experimental import pallas as pl
from jax.experimental.pallas import tpu as pltpu

f32 = jnp.float32
bf16 = jnp.bfloat16

D = 2048
S = 8192
C = 256
T = S + C
DEPTH = 4
GRID_W = 64
HEAD_DIM = 128
POOL_WIDTH = 512
POOL_WINDOWS = (2, 4, 8, 16)
POOL_GROUP = 128
N_Q_HEADS = 8
N_KV_HEADS = 2
GQA = 4
WINDOW = 128
ATTN_SCALE = HEAD_DIM ** -0.5
ROPE_BASE = 10000.0
ROPE_FREQS = 32
CONV_WIDTH = 512
CONV_K = 31
GATE_WIDTH = 3 * D
POOL_OFF = GATE_WIDTH
Q_OFF = POOL_OFF + POOL_WIDTH
K_OFF = Q_OFF + N_Q_HEADS * HEAD_DIM
V_OFF = K_OFF + N_KV_HEADS * HEAD_DIM
CONV_OFF = V_OFF + N_KV_HEADS * HEAD_DIM
IN_WIDTH = CONV_OFF + 2 * CONV_WIDTH
FFN = 5632
N_EXPERTS = 8
N_MOD = 6
EPS = 1e-6
NEG_INF = -1e30

VMEM_LIMIT_BYTES = 56 * 1024 * 1024

TM = 1056
NM = T // TM
TS = 256
NS = T // TS
NS_LAT = S // TS
HALO = 16
SUBLANES = 8
TN_IN = 1024
TN_MERGE = 256
TF = 256
TN_MOD = 1024
ROUTER_PAD = 128
ROW_CHUNK = 96
NORM_CHUNK = 32
NORM_UNROLL = 3
assert S % NORM_CHUNK == 0 and TM % (NORM_CHUNK * NORM_UNROLL) == 0
TG = 1024
NT = (2 * T + N_EXPERTS * (TG - 1) + TG - 1) // TG
MOD_ROWS = 16


def _cparams(semantics):
    return pltpu.CompilerParams(dimension_semantics=semantics,
                                vmem_limit_bytes=VMEM_LIMIT_BYTES)


def _sigmoid(x):
    return 1.0 / (1.0 + jnp.exp(-x))


def _silu(x):
    return x * _sigmoid(x)


def _mod_row(mod_ref, row0):
    return mod_ref[pl.ds(jnp.where(row0 >= S, 1, 0), 1), :]


def _rms(x, g):
    return x * lax.rsqrt(jnp.mean(x * x, axis=-1, keepdims=True) + EPS) * g


def _for_row_chunks(fn):
    def step(r, carry):
        off = pl.multiple_of(r * ROW_CHUNK, ROW_CHUNK)
        fn(pl.ds(off, ROW_CHUNK), off)
        return carry
    lax.fori_loop(0, TM // ROW_CHUNK, step, 0, unroll=True)


def _for_norm_chunks(fn):
    def step(r, carry):
        off = pl.multiple_of(r * NORM_CHUNK, NORM_CHUNK)
        fn(pl.ds(off, NORM_CHUNK), off)
        return carry
    lax.fori_loop(0, TM // NORM_CHUNK, step, 0, unroll=NORM_UNROLL)


def _norm_mod_tile(h_ref, g, sh_ref, sc_ref, m, store):
    def chunk(rows, off):
        row0 = m * TM + off
        u = _rms(h_ref[rows, :], g) * (1.0 + _mod_row(sc_ref, row0)) + _mod_row(sh_ref, row0)
        store(rows, u)
    _for_norm_chunks(chunk)


def _residual_tile(h_ref, y_ref, o_ref, g, gate_ref, m):
    def chunk(rows, off):
        o_ref[rows, :] = h_ref[rows, :] + _mod_row(gate_ref, m * TM + off) * _rms(y_ref[rows, :], g)
    _for_norm_chunks(chunk)


def _dot(a, b):
    return jnp.dot(a, b, preferred_element_type=f32)


def _mod_kernel(cc_ref, w_ref, b_ref, o_ref):
    s = _silu(cc_ref[...])
    o_ref[...] = _dot(s.astype(bf16), w_ref[...].astype(bf16)) + b_ref[...]


def _modulation(cc, w_mod, b_mod):
    nn = (N_MOD * D) // TN_MOD
    return pl.pallas_call(
        _mod_kernel,
        grid=(DEPTH, nn),
        in_specs=[
            pl.BlockSpec((MOD_ROWS, D), lambda l, n: (0, 0)),
            pl.BlockSpec((None, D, TN_MOD), lambda l, n: (l, 0, n)),
            pl.BlockSpec((None, 1, TN_MOD), lambda l, n: (l, 0, n)),
        ],
        out_specs=pl.BlockSpec((None, MOD_ROWS, TN_MOD), lambda l, n: (l, 0, n)),
        out_shape=jax.ShapeDtypeStruct((DEPTH, MOD_ROWS, N_MOD * D), f32),
        compiler_params=_cparams(("arbitrary", "arbitrary")),
        name="modulation",
    )(cc, w_mod, b_mod.reshape(DEPTH, 1, N_MOD * D))


def _win_kernel(h_ref, ng_ref, sh_ref, sc_ref, w_ref, z_ref, u_ref):
    m = pl.program_id(0)

    @pl.when(pl.program_id(1) == 0)
    def _():
        def store(rows, u):
            u_ref[rows, :] = u.astype(bf16)
        _norm_mod_tile(h_ref, ng_ref[0:1, :], sh_ref, sc_ref, m, store)

    z_ref[...] = _dot(u_ref[...], w_ref[...].astype(bf16)).astype(bf16)


def _in_proj(h, norm_g, mod, w_in, layer):
    return pl.pallas_call(
        _win_kernel,
        grid=(NM, IN_WIDTH // TN_IN),
        in_specs=[
            pl.BlockSpec((TM, D), lambda m, n: (m, 0)),
            pl.BlockSpec((None, 4, D), lambda m, n: (layer, 0, 0)),
            pl.BlockSpec((None, MOD_ROWS, D), lambda m, n: (layer, 0, 0)),
            pl.BlockSpec((None, MOD_ROWS, D), lambda m, n: (layer, 0, 1)),
            pl.BlockSpec((None, D, TN_IN), lambda m, n: (layer, 0, n)),
        ],
        out_specs=pl.BlockSpec((TM, TN_IN), lambda m, n: (m, n)),
        out_shape=jax.ShapeDtypeStruct((T, IN_WIDTH), bf16),
        scratch_shapes=[pltpu.VMEM((TM, D), bf16)],
        compiler_params=_cparams(("arbitrary", "arbitrary")),
        name="in_proj",
    )(h, norm_g, mod, mod, w_in)


def _seqmix_kernel(zp_c, zp_p, zp_n, zc_c, zc_p, zc_n, pw_ref, ps_ref, dw_ref, db_ref,
                   lng_ref, lnb_ref, po_ref, co_ref, xe, ge, gs):
    i = pl.program_id(0)
    has_prev = jnp.where(jnp.logical_and(i != 0, i != NS_LAT), 1.0, 0.0).astype(f32)
    has_next = jnp.where(i < NS_LAT - 1, 1.0, 0.0).astype(f32)
    is_lat = i < NS_LAT
    pos = jnp.where(is_lat, i * TS, 0) + lax.broadcasted_iota(jnp.int32, (TS, 1), 0)
    seq_len = jnp.where(is_lat, S, C)

    xe[0:HALO, :] = zp_p[...].astype(f32) * has_prev
    xe[HALO:HALO + TS, :] = zp_c[...].astype(f32)
    xe[HALO + TS:, :] = zp_n[...].astype(f32) * has_next
    outs = []
    for g, w in enumerate(POOL_WINDOWS):
        cols = slice(g * POOL_GROUP, (g + 1) * POOL_GROUP)
        acc = xe[pl.ds(HALO - w // 2, TS), cols]
        for d in range(-w // 2 + 1, w // 2):
            acc = acc + xe[pl.ds(HALO + d, TS), cols]
        lo = jnp.clip(pos - w // 2, 0, seq_len)
        hi = jnp.clip(pos - w // 2 + w, 0, seq_len)
        diff = acc / (hi - lo).astype(f32) - xe[HALO:HALO + TS, cols]
        outs.append(_dot(diff.astype(bf16), pw_ref[g].astype(bf16)))
    po_ref[...] = (jnp.concatenate(outs, axis=1) * ps_ref[...]).astype(bf16)

    def glu(ref):
        zz = ref[...].astype(f32)
        return zz[:, :CONV_WIDTH] * _sigmoid(zz[:, CONV_WIDTH:])

    ge[0:HALO, :] = glu(zc_p) * has_prev
    ge[HALO:HALO + TS, :] = glu(zc_c)
    ge[HALO + TS:, :] = glu(zc_n) * has_next
    span = TS + 2 * HALO - SUBLANES
    for s in range(SUBLANES):
        gs[s] = ge[pl.ds(s, span), :]
    base = HALO - CONV_K // 2
    acc = None
    for k in range(CONV_K):
        shift, start = (base + k) % SUBLANES, (base + k) // SUBLANES * SUBLANES
        term = gs[shift, pl.ds(start, TS), :] * dw_ref[k:k + 1, :]
        acc = term if acc is None else acc + term
    y = acc + db_ref[...]
    yc = y - jnp.mean(y, axis=-1, keepdims=True)
    yn = yc * lax.rsqrt(jnp.mean(yc * yc, axis=-1, keepdims=True) + EPS) * lng_ref[...] + lnb_ref[...]
    co_ref[...] = _silu(yn).astype(bf16)


def _seq_mixers(z, pool_w, pool_scale, conv_dw, conv_db, conv_ln_g, conv_ln_b, layer):
    rb = TS // HALO
    last = T // HALO - 1
    pcol = POOL_OFF // POOL_WIDTH
    ccol = CONV_OFF // (2 * CONV_WIDTH)

    def prev(i):
        return jnp.maximum(i * rb - 1, 0)

    def nxt(i):
        return jnp.minimum((i + 1) * rb, last)

    vec = lambda width: pl.BlockSpec((None, 1, width), lambda i: (layer, 0, 0))
    return pl.pallas_call(
        _seqmix_kernel,
        grid=(NS,),
        in_specs=[
            pl.BlockSpec((TS, POOL_WIDTH), lambda i: (i, pcol)),
            pl.BlockSpec((HALO, POOL_WIDTH), lambda i: (prev(i), pcol)),
            pl.BlockSpec((HALO, POOL_WIDTH), lambda i: (nxt(i), pcol)),
            pl.BlockSpec((TS, 2 * CONV_WIDTH), lambda i: (i, ccol)),
            pl.BlockSpec((HALO, 2 * CONV_WIDTH), lambda i: (prev(i), ccol)),
            pl.BlockSpec((HALO, 2 * CONV_WIDTH), lambda i: (nxt(i), ccol)),
            pl.BlockSpec((None, 4, POOL_GROUP, POOL_GROUP), lambda i: (layer, 0, 0, 0)),
            vec(POOL_WIDTH),
            pl.BlockSpec((None, CONV_K, CONV_WIDTH), lambda i: (layer, 0, 0)),
            vec(CONV_WIDTH), vec(CONV_WIDTH), vec(CONV_WIDTH),
        ],
        out_specs=[
            pl.BlockSpec((TS, POOL_WIDTH), lambda i: (i, 0)),
            pl.BlockSpec((TS, CONV_WIDTH), lambda i: (i, 0)),
        ],
        out_shape=[
            jax.ShapeDtypeStruct((T, POOL_WIDTH), bf16),
            jax.ShapeDtypeStruct((T, CONV_WIDTH), bf16),
        ],
        scratch_shapes=[
            pltpu.VMEM((TS + 2 * HALO, POOL_WIDTH), f32),
            pltpu.VMEM((TS + 2 * HALO, CONV_WIDTH), f32),
            pltpu.VMEM((SUBLANES, TS + 2 * HALO - SUBLANES, CONV_WIDTH), f32),
        ],
        compiler_params=_cparams(("arbitrary",)),
        name="seq_mixers",
    )(z, z, z, z, z, z, pool_w, pool_scale.reshape(DEPTH, 1, POOL_WIDTH), conv_dw,
      conv_db.reshape(DEPTH, 1, CONV_WIDTH), conv_ln_g.reshape(DEPTH, 1, CONV_WIDTH),
      conv_ln_b.reshape(DEPTH, 1, CONV_WIDTH))


def _rope(x, cos_t, sin_t):
    lane = lax.broadcasted_iota(jnp.int32, x.shape, 1) & (2 * ROPE_FREQS - 1)
    partner = jnp.where(lane < ROPE_FREQS, pltpu.roll(x, HEAD_DIM - ROPE_FREQS, 1),
                        pltpu.roll(x, ROPE_FREQS, 1))
    return x * cos_t + partner * sin_t


def _attn_kernel(sink_ref, q_ref, kc_ref, kp_ref, kn_ref, vc_ref, vp_ref, vn_ref, kx_ref, vx_ref,
                 cq_ref, sq_ref, cp_ref, sp_ref, cn_ref, sn_ref, o_ref, *, layer):
    hk = pl.program_id(0)
    i = pl.program_id(1)
    cq = cq_ref[...]
    sq = sq_ref[...]
    qs = []
    for j in range(GQA):
        qj = q_ref[:, j * HEAD_DIM:(j + 1) * HEAD_DIM].astype(f32)
        qs.append(_rope(qj, cq, sq).astype(bf16))
    q = jnp.concatenate(qs, axis=0)
    k = jnp.concatenate([
        _rope(kp_ref[...].astype(f32), cp_ref[...], sp_ref[...]).astype(bf16),
        _rope(kc_ref[...].astype(f32), cq, sq).astype(bf16),
        _rope(kn_ref[...].astype(f32), cn_ref[...], sn_ref[...]).astype(bf16)], axis=0)
    v = jnp.concatenate([vp_ref[...], vc_ref[...], vn_ref[...]], axis=0)
    kw = k.shape[0]

    nt = (((1,), (1,)), ((), ()))
    s_loc = lax.dot_general(q, k, nt, preferred_element_type=f32) * ATTN_SCALE
    s_ctx = lax.dot_general(q, kx_ref[...], nt, preferred_element_type=f32) * ATTN_SCALE

    qrow = lax.broadcasted_iota(jnp.int32, (GQA * TS, 1), 0) & (TS - 1)
    krel = lax.broadcasted_iota(jnp.int32, (1, kw), 1) - WINDOW
    kpos = i * TS + krel
    k_ok = jnp.logical_and(jnp.logical_and(kpos >= 0, kpos < S), i < NS_LAT)
    valid = jnp.logical_and(jnp.abs(krel - qrow) <= WINDOW, k_ok)
    s_loc = jnp.where(valid, s_loc, NEG_INF)

    sink = jnp.concatenate(
        [jnp.full((TS, 1), sink_ref[layer, hk * GQA + j], f32) for j in range(GQA)], axis=0)
    mx = jnp.maximum(jnp.maximum(jnp.max(s_loc, axis=-1, keepdims=True),
                                 jnp.max(s_ctx, axis=-1, keepdims=True)), sink)
    p_loc = jnp.exp(s_loc - mx)
    p_ctx = jnp.exp(s_ctx - mx)
    denom = (jnp.sum(p_loc, axis=-1, keepdims=True) + jnp.sum(p_ctx, axis=-1, keepdims=True)
             + jnp.exp(sink - mx))
    o = (_dot(p_loc.astype(bf16), v) + _dot(p_ctx.astype(bf16), vx_ref[...])) / denom
    for j in range(GQA):
        o_ref[:, j * HEAD_DIM:(j + 1) * HEAD_DIM] = o[j * TS:(j + 1) * TS, :].astype(bf16)


def _attention(z, attn_sink, cos_t, sin_t, layer):
    qcol = Q_OFF // (GQA * HEAD_DIM)
    kcol = K_OFF // HEAD_DIM
    vcol = V_OFF // HEAD_DIM
    hb = TS // WINDOW
    last = T // WINDOW - 1
    ctx_blk = S // TS

    def prev(i):
        return jnp.maximum(i * hb - 1, 0)

    def nxt(i):
        return jnp.minimum((i + 1) * hb, last)

    def kv_specs(col):
        return [
            pl.BlockSpec((TS, HEAD_DIM), lambda h, i: (i, col + h)),
            pl.BlockSpec((WINDOW, HEAD_DIM), lambda h, i: (prev(i), col + h)),
            pl.BlockSpec((WINDOW, HEAD_DIM), lambda h, i: (nxt(i), col + h)),
        ]

    tab_specs = [
        pl.BlockSpec((TS, HEAD_DIM), lambda h, i: (i, 0)),
        pl.BlockSpec((TS, HEAD_DIM), lambda h, i: (i, 0)),
        pl.BlockSpec((WINDOW, HEAD_DIM), lambda h, i: (prev(i), 0)),
        pl.BlockSpec((WINDOW, HEAD_DIM), lambda h, i: (prev(i), 0)),
        pl.BlockSpec((WINDOW, HEAD_DIM), lambda h, i: (nxt(i), 0)),
        pl.BlockSpec((WINDOW, HEAD_DIM), lambda h, i: (nxt(i), 0)),
    ]
    return pl.pallas_call(
        functools.partial(_attn_kernel, layer=layer),
        grid=(N_KV_HEADS, NS),
        in_specs=[pl.BlockSpec(memory_space=pltpu.SMEM),
                  pl.BlockSpec((TS, GQA * HEAD_DIM), lambda h, i: (i, qcol + h))]
                 + kv_specs(kcol) + kv_specs(vcol)
                 + [pl.BlockSpec((TS, HEAD_DIM), lambda h, i: (ctx_blk, kcol + h)),
                    pl.BlockSpec((TS, HEAD_DIM), lambda h, i: (ctx_blk, vcol + h))]
                 + tab_specs,
        out_specs=pl.BlockSpec((TS, GQA * HEAD_DIM), lambda h, i: (i, h)),
        out_shape=jax.ShapeDtypeStruct((T, N_Q_HEADS * HEAD_DIM), bf16),
        compiler_params=_cparams(("arbitrary", "arbitrary")),
        name="attention",
    )(attn_sink, z, z, z, z, z, z, z, z, z, cos_t, sin_t, cos_t, sin_t, cos_t, sin_t)


NB_MERGE = D // TN_MERGE


def _merge_kernel(h_ref, p_ref, a_ref, c_ref, zg0, zg1, zg2, bg0, bg1, bg2, wp, wa, wc, wo,
                  ng_ref, g1_ref, o_ref, act_ref, acc_ref):
    m = pl.program_id(0)
    n = pl.program_id(1)

    def gate(zg, bg):
        return _sigmoid(zg[...].astype(f32) + bg[...])

    def up():
        merged = (gate(zg0, bg0) * _dot(p_ref[...], wp[...].astype(bf16))
                  + gate(zg1, bg1) * _dot(a_ref[...], wa[...].astype(bf16))
                  + gate(zg2, bg2) * _dot(c_ref[...], wc[...].astype(bf16)))
        return merged.astype(bf16)

    def down():
        return _dot(act_ref[...], wo[...].astype(bf16))

    _skewed_pipeline(n, NB_MERGE, up, down, act_ref, acc_ref)

    @pl.when(n == NB_MERGE)
    def _():
        _residual_tile(h_ref, acc_ref, o_ref, ng_ref[1:2, :], g1_ref, m)


def _merge(h, z, pool_o, attn_o, conv_o, b_gate, w_pool_up, w_attn_up, w_conv_up, w_out,
           norm_g, mod, layer):
    nb = NB_MERGE
    bg = b_gate.reshape(DEPTH, 1, GATE_WIDTH)
    upc = lambda n: _up_chunk(n, nb)
    zgate = lambda b: pl.BlockSpec((TM, TN_MERGE), lambda m, n: (m, b * nb + upc(n)))
    bgate = lambda b: pl.BlockSpec((None, 1, TN_MERGE), lambda m, n: (layer, 0, b * nb + upc(n)))
    wup = lambda k: pl.BlockSpec((None, k, TN_MERGE), lambda m, n: (layer, 0, upc(n)))
    return pl.pallas_call(
        _merge_kernel,
        grid=(NM, nb + 1),
        in_specs=[
            pl.BlockSpec((TM, D), lambda m, n: (m, 0), pipeline_mode=pl.Buffered(1)),
            pl.BlockSpec((TM, POOL_WIDTH), lambda m, n: (m, 0), pipeline_mode=pl.Buffered(1)),
            pl.BlockSpec((TM, N_Q_HEADS * HEAD_DIM), lambda m, n: (m, 0), pipeline_mode=pl.Buffered(1)),
            pl.BlockSpec((TM, CONV_WIDTH), lambda m, n: (m, 0), pipeline_mode=pl.Buffered(1)),
            zgate(0), zgate(1), zgate(2), bgate(0), bgate(1), bgate(2),
            wup(POOL_WIDTH), wup(N_Q_HEADS * HEAD_DIM), wup(CONV_WIDTH),
            pl.BlockSpec((None, TN_MERGE, D), lambda m, n: (layer, _down_chunk(n, nb), 0)),
            pl.BlockSpec((None, 4, D), lambda m, n: (layer, 0, 0)),
            pl.BlockSpec((None, MOD_ROWS, D), lambda m, n: (layer, 0, 2)),
        ],
        out_specs=pl.BlockSpec((TM, D), lambda m, n: (m, 0)),
        out_shape=jax.ShapeDtypeStruct((T, D), f32),
        scratch_shapes=[pltpu.VMEM((TM, TN_MERGE), bf16), pltpu.VMEM((TM, D), f32)],
        compiler_params=_cparams(("arbitrary", "arbitrary")),
        name="merge",
    )(h, pool_o, attn_o, conv_o, z, z, z, bg, bg, bg, w_pool_up, w_attn_up, w_conv_up, w_out,
      norm_g, mod)


def _ffn_prologue(h_ref, ng_ref, sh_ref, sc_ref, u_ref, m):
    def store(rows, u):
        u_ref[rows, :] = u.astype(bf16)
    _norm_mod_tile(h_ref, ng_ref[2:3, :], sh_ref, sc_ref, m, store)


def _ffn_epilogue(h_ref, ng_ref, g2_ref, y_ref, o_ref, m):
    _residual_tile(h_ref, y_ref, o_ref, ng_ref[3:4, :], g2_ref, m)


NF = FFN // TF


def _skewed_pipeline(step, n_chunks, up, down, act_ref, o_ref):
    @pl.when(step == 0)
    def _():
        act_ref[...] = up()

    @pl.when(step == 1)
    def _():
        part = down()
        act = up()
        o_ref[...] = part
        act_ref[...] = act

    @pl.when(jnp.logical_and(step > 1, step < n_chunks))
    def _():
        part = down()
        act = up()
        o_ref[...] += part
        act_ref[...] = act

    @pl.when(step == n_chunks)
    def _():
        o_ref[...] += down()


def _up_chunk(step, n_chunks):
    return jnp.minimum(step, n_chunks - 1)


def _down_chunk(step, n_chunks):
    del n_chunks
    return jnp.maximum(step - 1, 0)


def _swiglu_pipeline(f, u_ref, w1, w3, w2, act_ref, o_ref):
    def up():
        u = u_ref[...]
        act = _silu(_dot(u, w1[...].astype(bf16))) * _dot(u, w3[...].astype(bf16))
        return act.astype(bf16)

    def down():
        return _dot(act_ref[...], w2[...].astype(bf16))

    _skewed_pipeline(f, NF, up, down, act_ref, o_ref)


def _ffn_kernel(h_ref, ng_ref, sh_ref, sc_ref, g2_ref, w1, w3, w2, o_ref, u_ref, act_ref, acc_ref):
    m = pl.program_id(0)
    f = pl.program_id(1)

    @pl.when(f == 0)
    def _():
        _ffn_prologue(h_ref, ng_ref, sh_ref, sc_ref, u_ref, m)

    _swiglu_pipeline(f, u_ref, w1, w3, w2, act_ref, acc_ref)

    @pl.when(f == NF)
    def _():
        _ffn_epilogue(h_ref, ng_ref, g2_ref, acc_ref, o_ref, m)


def _mod_specs(layer, chunks):
    specs = [pl.BlockSpec((TM, D), lambda m, *r: (m, 0), pipeline_mode=pl.Buffered(1)),
             pl.BlockSpec((None, 4, D), lambda m, *r: (layer, 0, 0))]
    for ch in chunks:
        specs.append(pl.BlockSpec((None, MOD_ROWS, D), lambda m, *r, ch=ch: (layer, 0, ch)))
    return specs


def _dense_ffn(h, norm_g, mod, w1, w3, w2, layer):
    j = layer // 2
    return pl.pallas_call(
        _ffn_kernel,
        grid=(NM, NF + 1),
        in_specs=_mod_specs(layer, (3, 4, 5)) + [
            pl.BlockSpec((None, D, TF), lambda m, f: (j, 0, _up_chunk(f, NF))),
            pl.BlockSpec((None, D, TF), lambda m, f: (j, 0, _up_chunk(f, NF))),
            pl.BlockSpec((None, TF, D), lambda m, f: (j, _down_chunk(f, NF), 0)),
        ],
        out_specs=pl.BlockSpec((TM, D), lambda m, f: (m, 0)),
        out_shape=jax.ShapeDtypeStruct((T, D), f32),
        scratch_shapes=[pltpu.VMEM((TM, D), bf16), pltpu.VMEM((TM, TF), bf16), pltpu.VMEM((TM, D), f32)],
        compiler_params=_cparams(("arbitrary", "arbitrary")),
        name="dense_ffn",
    )(h, norm_g, mod, mod, mod, w1, w3, w2)


def _split_bf16(x):
    hi = x.astype(bf16)
    lo = (x - hi.astype(f32)).astype(bf16)
    return hi, lo


R_E0, R_E1, R_G0, R_G1, R_R0, R_R1 = range(6)


def _router_kernel(h_ref, ng_ref, sh_ref, sc_ref, rw_ref, route_ref, cnt_ref, u_ref, base_ref, lg_ref):
    m = pl.program_id(0)

    @pl.when(m == 0)
    def _():
        base_ref[...] = jnp.zeros_like(base_ref)

    def store(rows, u):
        u_ref[rows, :] = u

    _norm_mod_tile(h_ref, ng_ref[2:3, :], sh_ref, sc_ref, m, store)

    wh, wl = _split_bf16(rw_ref[...])
    for half in range(2):
        sl = pl.ds(half * (TM // 2), TM // 2)
        uh, ul = _split_bf16(u_ref[sl, :])
        lg_ref[sl, :] = _dot(uh, wh) + (_dot(uh, wl) + _dot(ul, wh))

    ri = lax.broadcasted_iota(jnp.int32, (ROW_CHUNK, ROW_CHUNK), 0)
    ci = lax.broadcasted_iota(jnp.int32, (ROW_CHUNK, ROW_CHUNK), 1)
    earlier = jnp.where(ri > ci, 1.0, 0.0).astype(bf16)

    def chunk(rows, off):
        del off
        logits = lg_ref[rows, :]
        lane = lax.broadcasted_iota(jnp.int32, logits.shape, 1)
        lg = jnp.where(lane < N_EXPERTS, logits, -jnp.inf)
        m1 = jnp.max(lg, axis=-1, keepdims=True)
        i1 = jnp.min(jnp.where(lg == m1, lane, ROUTER_PAD), axis=-1, keepdims=True)
        lg2 = jnp.where(lane == i1, -jnp.inf, lg)
        m2 = jnp.max(lg2, axis=-1, keepdims=True)
        i2 = jnp.min(jnp.where(lg2 == m2, lane, ROUTER_PAD), axis=-1, keepdims=True)
        e2 = jnp.exp(m2 - m1)
        den = 1.0 + e2
        hit1 = lane == i1
        hit2 = lane == i2
        onehot = jnp.where(jnp.logical_or(hit1, hit2), 1.0, 0.0)
        before = _dot(earlier, onehot.astype(bf16)) + base_ref[...]
        r1 = jnp.sum(jnp.where(hit1, before, 0.0), axis=-1, keepdims=True)
        r2 = jnp.sum(jnp.where(hit2, before, 0.0), axis=-1, keepdims=True)
        base_ref[...] = base_ref[...] + jnp.sum(onehot, axis=0, keepdims=True)
        vals = (i1.astype(f32), i2.astype(f32), 1.0 / den, e2 / den, r1, r2)
        route = jnp.zeros(logits.shape, f32)
        for col, val in enumerate(vals):
            route = jnp.where(lane == col, val, route)
        route_ref[rows, :] = route

    _for_row_chunks(chunk)
    cnt_ref[...] = jnp.broadcast_to(base_ref[...], cnt_ref.shape)


def _router(h, norm_g, mod, router_w_pad, layer):
    j = layer // 2
    return pl.pallas_call(
        _router_kernel,
        grid=(NM,),
        in_specs=_mod_specs(layer, (3, 4)) + [
            pl.BlockSpec((None, D, ROUTER_PAD), lambda m: (j, 0, 0)),
        ],
        out_specs=[
            pl.BlockSpec((TM, ROUTER_PAD), lambda m: (m, 0)),
            pl.BlockSpec((8, ROUTER_PAD), lambda m: (0, 0)),
            pl.BlockSpec((TM, D), lambda m: (m, 0)),
        ],
        out_shape=[
            jax.ShapeDtypeStruct((T, ROUTER_PAD), f32),
            jax.ShapeDtypeStruct((8, ROUTER_PAD), f32),
            jax.ShapeDtypeStruct((T, D), f32),
        ],
        scratch_shapes=[pltpu.VMEM((1, ROUTER_PAD), f32), pltpu.VMEM((TM, ROUTER_PAD), f32)],
        compiler_params=_cparams(("arbitrary",)),
        name="router",
    )(h, norm_g, mod, mod, router_w_pad)


def _dispatch_plan(route, counts):
    expert = route[:, R_E0:R_E1 + 1].astype(jnp.int32)
    rank = route[:, R_R0:R_R1 + 1].astype(jnp.int32)
    cnt = counts[0, :N_EXPERTS].astype(jnp.int32)
    tiles = (cnt + TG - 1) // TG
    tile_end = jnp.cumsum(tiles)
    start = (tile_end - tiles) * TG
    dest = (start[expert] + rank).reshape(2 * T)
    n_used = tile_end[-1]
    tile_id = jnp.minimum(jnp.arange(NT, dtype=jnp.int32), n_used - 1)
    tile_expert = jnp.sum(tile_id[:, None] >= tile_end[None, :], axis=1).astype(jnp.int32)
    rows_used = cnt[tile_expert] - (tile_id - (tile_end - tiles)[tile_expert]) * TG
    tile_mode = jnp.where(jnp.arange(NT, dtype=jnp.int32) >= n_used, TILE_UNUSED,
                          jnp.where(rows_used <= TG // 2, TILE_HALF, TILE_FULL)).astype(jnp.int32)
    return dest, tile_expert, tile_mode


def _row_copy(src_ref, src_row, dst_ref, dst_row, sem):
    return pltpu.make_async_copy(src_ref.at[pl.ds(src_row, 1)], dst_ref.at[pl.ds(dst_row, 1)], sem)


def _dispatch_kernel(dest_ref, u_ref, xs_in_ref, xs_ref, sem):
    del xs_in_ref
    m = pl.program_id(0)

    def issue(r, carry):
        t = m * TM + r
        _row_copy(u_ref, r, xs_ref, dest_ref[2 * t], sem).start()
        _row_copy(u_ref, r, xs_ref, dest_ref[2 * t + 1], sem).start()
        return carry

    lax.fori_loop(0, TM, issue, 0, unroll=8)
    for _ in range(2):
        pltpu.make_async_copy(u_ref, xs_ref.at[pl.ds(0, TM)], sem).wait()


def _dispatch(dest, u):
    xs0 = jnp.zeros((NT * TG, D), f32)
    return pl.pallas_call(
        _dispatch_kernel,
        grid_spec=pltpu.PrefetchScalarGridSpec(
            num_scalar_prefetch=1,
            grid=(NM,),
            in_specs=[pl.BlockSpec((TM, D), lambda m, d: (m, 0)), pl.BlockSpec(memory_space=pl.ANY)],
            out_specs=pl.BlockSpec(memory_space=pl.ANY),
            scratch_shapes=[pltpu.SemaphoreType.DMA(())],
        ),
        out_shape=jax.ShapeDtypeStruct((NT * TG, D), f32),
        input_output_aliases={2: 0},
        compiler_params=_cparams(("arbitrary",)),
        name="moe_dispatch",
    )(dest, u, xs0)


TILE_UNUSED, TILE_HALF, TILE_FULL = 0, 1, 2


def _group_kernel(te_ref, tm_ref, xs_ref, w1, w3, w2, ys_ref, u_ref, act_ref):
    del te_ref
    j = pl.program_id(0)
    f = pl.program_id(1)
    mode = tm_ref[j]

    @pl.when(jnp.logical_and(mode != TILE_FULL, f == 0))
    def _():
        ys_ref[...] = jnp.zeros_like(ys_ref)

    def run(rows):
        sl = pl.ds(0, rows)

        @pl.when(f == 0)
        def _():
            u_ref[sl, :] = xs_ref[sl, :].astype(bf16)

        _swiglu_pipeline(f, u_ref.at[sl], w1, w3, w2, act_ref.at[sl], ys_ref.at[sl])

    @pl.when(mode == TILE_FULL)
    def _():
        run(TG)

    @pl.when(mode == TILE_HALF)
    def _():
        run(TG // 2)


def _grouped_ffn(tile_expert, tile_mode, xs, w1, w3, w2, layer):
    jl = layer // 2

    def chunk(j, f, tmode, which):
        return jnp.where(tmode[j] != TILE_UNUSED, which(f, NF), NF - 1)

    return pl.pallas_call(
        _group_kernel,
        grid_spec=pltpu.PrefetchScalarGridSpec(
            num_scalar_prefetch=2,
            grid=(NT, NF + 1),
            in_specs=[
                pl.BlockSpec((TG, D), lambda j, f, te, tu: (j, 0)),
                pl.BlockSpec((None, None, D, TF),
                             lambda j, f, te, tu: (jl, te[j], 0, chunk(j, f, tu, _up_chunk))),
                pl.BlockSpec((None, None, D, TF),
                             lambda j, f, te, tu: (jl, te[j], 0, chunk(j, f, tu, _up_chunk))),
                pl.BlockSpec((None, None, TF, D),
                             lambda j, f, te, tu: (jl, te[j], chunk(j, f, tu, _down_chunk), 0)),
            ],
            out_specs=pl.BlockSpec((TG, D), lambda j, f, te, tu: (j, 0)),
            scratch_shapes=[pltpu.VMEM((TG, D), bf16), pltpu.VMEM((TG, TF), bf16)],
        ),
        out_shape=jax.ShapeDtypeStruct((NT * TG, D), f32),
        compiler_params=_cparams(("arbitrary", "arbitrary")),
        name="moe_grouped_ffn",
    )(tile_expert, tile_mode, xs, w1, w3, w2)


COMBINE_PHASES = ((0, 384), (384, 384), (768, 288))
ISSUE_UNROLL = 8
assert sum(n for _, n in COMBINE_PHASES) == TM
assert all(n % ISSUE_UNROLL == 0 and n % (NORM_CHUNK * NORM_UNROLL) == 0 for _, n in COMBINE_PHASES)


def _combine_kernel(dest_ref, h_ref, route_ref, ng_ref, g2_ref, ys_ref, o_ref, ybuf, sems):
    m = pl.program_id(0)

    for p, (start, n) in enumerate(COMBINE_PHASES):
        def issue(r, carry, start=start, p=p):
            row = start + r
            t = m * TM + row
            _row_copy(ys_ref, dest_ref[2 * t], ybuf.at[0], row, sems.at[p]).start()
            _row_copy(ys_ref, dest_ref[2 * t + 1], ybuf.at[1], row, sems.at[p]).start()
            return carry

        lax.fori_loop(0, n, issue, 0, unroll=ISSUE_UNROLL)

    for p, (start, n) in enumerate(COMBINE_PHASES):
        for k in range(2):
            pltpu.make_async_copy(ys_ref.at[pl.ds(0, n)], ybuf.at[k, pl.ds(start, n)], sems.at[p]).wait()

        def chunk(c, carry, start=start):
            off = pl.multiple_of(start + c * NORM_CHUNK, NORM_CHUNK)
            rows = pl.ds(off, NORM_CHUNK)
            route = route_ref[rows, :]
            y = route[:, R_G0:R_G0 + 1] * ybuf[0, rows, :] + route[:, R_G1:R_G1 + 1] * ybuf[1, rows, :]
            o_ref[rows, :] = h_ref[rows, :] + _mod_row(g2_ref, m * TM + off) * _rms(y, ng_ref[3:4, :])
            return carry

        lax.fori_loop(0, n // NORM_CHUNK, chunk, 0, unroll=NORM_UNROLL)


def _combine(dest, h, route, norm_g, mod, ys, layer, out_rows):
    return pl.pallas_call(
        _combine_kernel,
        grid_spec=pltpu.PrefetchScalarGridSpec(
            num_scalar_prefetch=1,
            grid=(NM,),
            in_specs=[
                pl.BlockSpec((TM, D), lambda m, d: (m, 0), pipeline_mode=pl.Buffered(1)),
                pl.BlockSpec((TM, ROUTER_PAD), lambda m, d: (m, 0)),
                pl.BlockSpec((None, 4, D), lambda m, d: (layer, 0, 0)),
                pl.BlockSpec((None, MOD_ROWS, D), lambda m, d: (layer, 0, 5)),
                pl.BlockSpec(memory_space=pl.ANY),
            ],
            out_specs=pl.BlockSpec((TM, D), lambda m, d: (m, 0)),
            scratch_shapes=[pltpu.VMEM((2, TM, D), f32),
                            pltpu.SemaphoreType.DMA((len(COMBINE_PHASES),))],
        ),
        out_shape=jax.ShapeDtypeStruct((out_rows, D), f32),
        compiler_params=_cparams(("arbitrary",)),
        name="moe_combine",
    )(dest, h, route, norm_g, mod, ys)


def _moe_ffn(h, norm_g, mod, router_w_pad, w1, w3, w2, layer, out_rows):
    route, counts, u = _router(h, norm_g, mod, router_w_pad, layer)
    dest, tile_expert, tile_mode = _dispatch_plan(route, counts)
    xs = _dispatch(dest, u)
    ys = _grouped_ffn(tile_expert, tile_mode, xs, w1, w3, w2, layer)
    return _combine(dest, h, route, norm_g, mod, ys, layer, out_rows)


def _rope_tables():
    t = jnp.arange(S)
    row = (t // GRID_W).astype(f32)
    col = (t % GRID_W).astype(f32)
    inv = ROPE_BASE ** (-jnp.arange(ROPE_FREQS, dtype=f32) / ROPE_FREQS)
    a0 = row[:, None] * inv
    a1 = col[:, None] * inv
    cos_t = jnp.concatenate([jnp.cos(a0), jnp.cos(a0), jnp.cos(a1), jnp.cos(a1)], axis=1)
    sin_t = jnp.concatenate([-jnp.sin(a0), jnp.sin(a0), -jnp.sin(a1), jnp.sin(a1)], axis=1)
    cos_t = jnp.concatenate([cos_t, jnp.ones((C, HEAD_DIM), f32)], axis=0)
    sin_t = jnp.concatenate([sin_t, jnp.zeros((C, HEAD_DIM), f32)], axis=0)
    return cos_t, sin_t


def kernel(x, c, ctx, c_ctx, w_mod, b_mod, norm_g, w_in, b_gate, pool_w, pool_scale, w_pool_up,
           attn_sink, w_attn_up, conv_dw, conv_db, conv_ln_g, conv_ln_b, w_conv_up, w_out,
           ffn_w1, ffn_w3, ffn_w2, router_w, moe_w1, moe_w3, moe_w2):
    assert x.shape == (1, S, D) and ctx.shape == (1, C, D)
    cos_t, sin_t = _rope_tables()
    cc = jnp.concatenate([c, c_ctx[None, :], jnp.zeros((MOD_ROWS - 2, D), f32)], axis=0)
    mod = _modulation(cc, w_mod, b_mod)
    router_w_pad = jnp.pad(router_w, ((0, 0), (0, 0), (0, ROUTER_PAD - N_EXPERTS)))
    h = jnp.concatenate([x[0], ctx[0]], axis=0)
    for i in range(DEPTH):
        z = _in_proj(h, norm_g, mod, w_in, i)
        pool_o, conv_o = _seq_mixers(z, pool_w, pool_scale, conv_dw, conv_db, conv_ln_g, conv_ln_b, i)
        attn_o = _attention(z, attn_sink, cos_t, sin_t, i)
        h = _merge(h, z, pool_o, attn_o, conv_o, b_gate, w_pool_up, w_attn_up, w_conv_up, w_out,
                   norm_g, mod, i)
        if i % 2 == 0:
            h = _dense_ffn(h, norm_g, mod, ffn_w1, ffn_w3, ffn_w2, i)
        else:
            h = _moe_ffn(h, norm_g, mod, router_w_pad, moe_w1, moe_w3, moe_w2, i,
                         out_rows=S if i == DEPTH - 1 else T)
    return h[None]
```
